```python
import math
import jax
import jax.numpy as jnp
from jax import lax
import numpy as np

D_MODEL = 2048
BATCH = 4
SEQ = 2048
DEPTH = 4
DEC_BATCH = 128
DEC_SEQ = 1
PAST_LEN = 16384
PAGE_SIZE = 128

N_MIXERS = 2
N_S5_LAYERS = (DEPTH + 1) // 2
N_ML_LAYERS = DEPTH // 2
S5_GROUP = 16
S5_GROUPS = D_MODEL // S5_GROUP
S5_STATE = 64
S5_DT_MIN = 1e-3
S5_DT_MAX = 1e-1
ML_HEADS = 4
ML_DV = D_MODEL // ML_HEADS
ML_DK = ML_DV // 2
ML_CHUNK = 128
ML_PROJ_SPLITS = [ML_HEADS * ML_DK, ML_HEADS * ML_DK, ML_HEADS * ML_DV, ML_HEADS * ML_DV, ML_HEADS, ML_HEADS]
ML_PROJ = sum(ML_PROJ_SPLITS)
D_FF = 5504
DN_ALPHA = (2.0 * DEPTH) ** 0.25
DN_BETA = (8.0 * DEPTH) ** -0.25
LN_EPS = 1e-5

kernel_name = 'hybrid_s5_mlstm_macaron_deepnorm_step'


def _layer_norm(x, g, b):
    xf = x.astype(jnp.float32)
    mu = jnp.mean(xf, axis=-1, keepdims=True)
    xc = xf - mu
    var = jnp.mean(xc * xc, axis=-1, keepdims=True)
    return (xc * lax.rsqrt(var + LN_EPS) * g + b).astype(x.dtype)


def _swiglu(x, w_gate, w_up, w_down):
    return (jax.nn.silu(x @ w_gate) * (x @ w_up)) @ w_down


def _complex_affine_combine(e1, e2):
    a1r, a1i, b1r, b1i = e1
    a2r, a2i, b2r, b2i = e2
    return (a2r * a1r - a2i * a1i,
            a2r * a1i + a2i * a1r,
            a2r * b1r - a2i * b1i + b2r,
            a2r * b1i + a2i * b1r + b2i)


def _s5_mix(u, h0_re, h0_im, a_re, a_im, log_dt, b_re, b_im, c_re, c_im, d_skip, w_glu_a, w_glu_b):
    bsz, s, _ = u.shape
    uf = u.astype(jnp.float32)
    ug = uf.reshape(bsz, s, S5_GROUPS, S5_GROUP)
    ar = a_re.astype(jnp.float32)
    ai = a_im.astype(jnp.float32)
    dt = jnp.exp(log_dt.astype(jnp.float32))[:, None]
    mag = jnp.exp(dt * ar)
    abr = mag * jnp.cos(dt * ai)
    abi = mag * jnp.sin(dt * ai)
    zr = abr - 1.0
    zi = abi
    den = ar * ar + ai * ai
    gr = (zr * ar + zi * ai) / den
    gi = (zi * ar - zr * ai) / den
    br = b_re.astype(jnp.float32)
    bi = b_im.astype(jnp.float32)
    bbr = gr[..., None] * br - gi[..., None] * bi
    bbi = gr[..., None] * bi + gi[..., None] * br
    bu_r = jnp.einsum('bsgc,gpc->bsgp', ug, bbr)
    bu_i = jnp.einsum('bsgc,gpc->bsgp', ug, bbi)
    h0r = h0_re.astype(jnp.float32)
    h0i = h0_im.astype(jnp.float32)
    bu_r = bu_r.at[:, 0].add(abr * h0r - abi * h0i)
    bu_i = bu_i.at[:, 0].add(abr * h0i + abi * h0r)
    a_r = jnp.broadcast_to(abr, bu_r.shape)
    a_i = jnp.broadcast_to(abi, bu_i.shape)
    _, _, hr, hi = lax.associative_scan(_complex_affine_combine, (a_r, a_i, bu_r, bu_i), axis=1)
    y = (jnp.einsum('bsgp,gcp->bsgc', hr, c_re.astype(jnp.float32))
         - jnp.einsum('bsgp,gcp->bsgc', hi, c_im.astype(jnp.float32)))
    y = y.reshape(bsz, s, D_MODEL) + d_skip.astype(jnp.float32) * uf
    z = jax.nn.gelu(y).astype(u.dtype)
    out = (z @ w_glu_a) * jax.nn.sigmoid(z @ w_glu_b)
    return out, hr[:, -1], hi[:, -1]


def _to_chunks(t, n_chunks, chunk):
    bsz = t.shape[0]
    t = t.reshape((bsz, n_chunks, chunk) + t.shape[2:])
    perm = (1, 0, 3, 2) + tuple(range(4, t.ndim))
    return jnp.transpose(t, perm)


def _mlstm_mix(x, c0, n0, m0, w_in, b_i, b_f, norm_g, w_out):
    bsz, s, _ = x.shape
    proj = x @ w_in
    cuts = [int(c) for c in np.cumsum(ML_PROJ_SPLITS)[:-1]]
    q, k, v, o_pre, i_pre, f_pre = jnp.split(proj, cuts, axis=-1)
    q = q.reshape(bsz, s, ML_HEADS, ML_DK).astype(jnp.float32)
    k = k.reshape(bsz, s, ML_HEADS, ML_DK).astype(jnp.float32) * (ML_DK ** -0.5)
    v = v.reshape(bsz, s, ML_HEADS, ML_DV).astype(jnp.float32)
    ig = (i_pre + b_i).astype(jnp.float32)
    lf = jax.nn.log_sigmoid((f_pre + b_f).astype(jnp.float32))
    chunk = math.gcd(s, ML_CHUNK)
    n_chunks = s // chunk
    xs = (_to_chunks(q, n_chunks, chunk), _to_chunks(k, n_chunks, chunk), _to_chunks(v, n_chunks, chunk),
          _to_chunks(ig, n_chunks, chunk), _to_chunks(lf, n_chunks, chunk))
    causal = jnp.tril(jnp.ones((chunk, chunk), dtype=bool))

    def step(carry, inp):
        c, n, m = carry
        qc, kc, vc, igc, lfc = inp
        bcum = jnp.cumsum(lfc, axis=-1)
        inter = bcum + m[..., None]
        dmat = bcum[..., :, None] - bcum[..., None, :] + igc[..., None, :]
        dmat = jnp.where(causal, dmat, -jnp.inf)
        mt = jnp.maximum(inter, jnp.max(dmat, axis=-1))
        wts = jnp.exp(dmat - mt[..., None])
        sc = jnp.einsum('bhtd,bhsd->bhts', qc, kc) * wts
        scale = jnp.exp(inter - mt)
        num = scale[..., None] * jnp.einsum('bhtd,bhdv->bhtv', qc, c) + jnp.einsum('bhts,bhsv->bhtv', sc, vc)
        den = scale * jnp.einsum('bhtd,bhd->bht', qc, n) + jnp.sum(sc, axis=-1)
        h = num / jnp.maximum(jnp.abs(den), jnp.exp(-mt))[..., None]
        m_last = mt[..., -1]
        dec = jnp.exp(bcum[..., -1:] - bcum + igc - m_last[..., None])
        carry_scale = jnp.exp(bcum[..., -1] + m - m_last)
        c_new = carry_scale[..., None, None] * c + jnp.einsum('bhs,bhsd,bhsv->bhdv', dec, kc, vc)
        n_new = carry_scale[..., None] * n + jnp.einsum('bhs,bhsd->bhd', dec, kc)
        return (c_new, n_new, m_last), h

    init = (c0.astype(jnp.float32), n0.astype(jnp.float32), m0.astype(jnp.float32))
    (c_f, n_f, m_f), hs = lax.scan(step, init, xs)
    h = jnp.transpose(hs, (1, 0, 3, 2, 4)).reshape(bsz, s, ML_HEADS, ML_DV)
    mu = jnp.mean(h, axis=-1, keepdims=True)
    hc = h - mu
    var = jnp.mean(hc * hc, axis=-1, keepdims=True)
    hn = hc * lax.rsqrt(var + LN_EPS) * norm_g.astype(jnp.float32).reshape(ML_HEADS, ML_DV)
    o = jax.nn.sigmoid(o_pre.astype(jnp.float32)).reshape(bsz, s, ML_HEADS, ML_DV)
    out = (o * hn).reshape(bsz, s, ML_HEADS * ML_DV).astype(x.dtype) @ w_out
    return out, c_f, n_f, m_f


def _trunk(x, s5_re0, s5_im0, ml_c0, ml_n0, ml_m0,
           ln_g, ln_b, ffn_w_gate, ffn_w_up, ffn_w_down,
           s5_a_re, s5_a_im, s5_log_dt, s5_b_re, s5_b_im, s5_c_re, s5_c_im, s5_d, s5_w_a, s5_w_b,
           ml_w_in, ml_b_i, ml_b_f, ml_norm_g, ml_w_out):
    s5_r, s5_i, ml_c, ml_n, ml_m = [], [], [], [], []
    for li in range(DEPTH):
        x = _layer_norm(DN_ALPHA * x + 0.5 * _swiglu(x, ffn_w_gate[li, 0], ffn_w_up[li, 0], ffn_w_down[li, 0]),
                        ln_g[li, 0], ln_b[li, 0])
        j = li // N_MIXERS
        if li % N_MIXERS == 0:
            mix, hr, hi = _s5_mix(x, s5_re0[j], s5_im0[j], s5_a_re[j], s5_a_im[j], s5_log_dt[j],
                                  s5_b_re[j], s5_b_im[j], s5_c_re[j], s5_c_im[j], s5_d[j], s5_w_a[j], s5_w_b[j])
            s5_r.append(hr)
            s5_i.append(hi)
        else:
            mix, c, n, m = _mlstm_mix(x, ml_c0[j], ml_n0[j], ml_m0[j], ml_w_in[j], ml_b_i[j], ml_b_f[j],
                                      ml_norm_g[j], ml_w_out[j])
            ml_c.append(c)
            ml_n.append(n)
            ml_m.append(m)
        x = _layer_norm(DN_ALPHA * x + mix, ln_g[li, 1], ln_b[li, 1])
        x = _layer_norm(DN_ALPHA * x + 0.5 * _swiglu(x, ffn_w_gate[li, 1], ffn_w_up[li, 1], ffn_w_down[li, 1]),
                        ln_g[li, 2], ln_b[li, 2])
    return x, jnp.stack(s5_r), jnp.stack(s5_i), jnp.stack(ml_c), jnp.stack(ml_n), jnp.stack(ml_m)


def setup_inputs(seed: int = 0) -> dict:
    key = jax.random.key(seed)
    ks = iter(jax.random.split(key, 40))
    nrm = lambda shape, scale: jax.random.normal(next(ks), shape, jnp.float32) * scale
    g, p, h = S5_GROUPS, S5_STATE, ML_HEADS
    inp = {}
    inp['x_prompt'] = nrm((BATCH, SEQ, D_MODEL), 1.0)
    inp['x_sample'] = nrm((DEC_BATCH, DEC_SEQ, D_MODEL), 1.0)
    inp['state_s5_re'] = nrm((N_S5_LAYERS, DEC_BATCH, g, p), 0.5)
    inp['state_s5_im'] = nrm((N_S5_LAYERS, DEC_BATCH, g, p), 0.5)
    inp['state_mlstm_C'] = nrm((N_ML_LAYERS, DEC_BATCH, h, ML_DK, ML_DV), ML_DK ** -0.5)
    inp['state_mlstm_n'] = nrm((N_ML_LAYERS, DEC_BATCH, h, ML_DK), ML_DK ** -0.5)
    inp['state_mlstm_m'] = nrm((N_ML_LAYERS, DEC_BATCH, h), 1.0)
    inp['ln_g'] = 1.0 + nrm((DEPTH, 3, D_MODEL), 0.02)
    inp['ln_b'] = nrm((DEPTH, 3, D_MODEL), 0.02)
    inp['ffn_w_gate'] = nrm((DEPTH, 2, D_MODEL, D_FF), D_MODEL ** -0.5)
    inp['ffn_w_up'] = nrm((DEPTH, 2, D_MODEL, D_FF), D_MODEL ** -0.5)
    inp['ffn_w_down'] = nrm((DEPTH, 2, D_FF, D_MODEL), D_FF ** -0.5 * DN_BETA)
    inp['s5_a_re'] = -0.5 + nrm((N_S5_LAYERS, g, p), 0.01)
    inp['s5_a_im'] = jnp.pi * jnp.arange(p, dtype=jnp.float32) + nrm((N_S5_LAYERS, g, p), 0.01)
    inp['s5_log_dt'] = jax.random.uniform(next(ks), (N_S5_LAYERS, g), jnp.float32,
                                          math.log(S5_DT_MIN), math.log(S5_DT_MAX))
    inp['s5_b_re'] = nrm((N_S5_LAYERS, g, p, S5_GROUP), (2.0 * S5_GROUP) ** -0.5)
    inp['s5_b_im'] = nrm((N_S5_LAYERS, g, p, S5_GROUP), (2.0 * S5_GROUP) ** -0.5)
    inp['s5_c_re'] = nrm((N_S5_LAYERS, g, S5_GROUP, p), p ** -0.5)
    inp['s5_c_im'] = nrm((N_S5_LAYERS, g, S5_GROUP, p), p ** -0.5)
    inp['s5_d'] = nrm((N_S5_LAYERS, D_MODEL), 1.0)
    inp['s5_w_a'] = nrm((N_S5_LAYERS, D_MODEL, D_MODEL), D_MODEL ** -0.5 * DN_BETA)
    inp['s5_w_b'] = nrm((N_S5_LAYERS, D_MODEL, D_MODEL), D_MODEL ** -0.5)
    inp['ml_w_in'] = nrm((N_ML_LAYERS, D_MODEL, ML_PROJ), D_MODEL ** -0.5)
    inp['ml_b_i'] = nrm((N_ML_LAYERS, h), 0.1)
    inp['ml_b_f'] = jnp.linspace(3.0, 6.0, h, dtype=jnp.float32) + nrm((N_ML_LAYERS, h), 0.1)
    inp['ml_norm_g'] = 1.0 + nrm((N_ML_LAYERS, h * ML_DV), 0.02)
    inp['ml_w_out'] = nrm((N_ML_LAYERS, h * ML_DV, D_MODEL), (h * ML_DV) ** -0.5 * DN_BETA)
    return inp


def reference(x_prompt, x_sample, state_s5_re, state_s5_im, state_mlstm_C, state_mlstm_n, state_mlstm_m,
              ln_g, ln_b, ffn_w_gate, ffn_w_up, ffn_w_down,
              s5_a_re, s5_a_im, s5_log_dt, s5_b_re, s5_b_im, s5_c_re, s5_c_im, s5_d, s5_w_a, s5_w_b,
              ml_w_in, ml_b_i, ml_b_f, ml_norm_g, ml_w_out):
    weights = (ln_g, ln_b, ffn_w_gate, ffn_w_up, ffn_w_down,
               s5_a_re, s5_a_im, s5_log_dt, s5_b_re, s5_b_im, s5_c_re, s5_c_im, s5_d, s5_w_a, s5_w_b,
               ml_w_in, ml_b_i, ml_b_f, ml_norm_g, ml_w_out)
    bp = x_prompt.shape[0]
    z_s5 = jnp.zeros((N_S5_LAYERS, bp, S5_GROUPS, S5_STATE), jnp.float32)
    z_c = jnp.zeros((N_ML_LAYERS, bp, ML_HEADS, ML_DK, ML_DV), jnp.float32)
    z_n = jnp.zeros((N_ML_LAYERS, bp, ML_HEADS, ML_DK), jnp.float32)
    z_m = jnp.zeros((N_ML_LAYERS, bp, ML_HEADS), jnp.float32)
    y_prompt, p_s5_re, p_s5_im, p_c, p_n, p_m = _trunk(x_prompt, z_s5, z_s5, z_c, z_n, z_m, *weights)
    y_sample, s_s5_re, s_s5_im, s_c, s_n, s_m = _trunk(x_sample, state_s5_re, state_s5_im, state_mlstm_C,
                                                       state_mlstm_n, state_mlstm_m, *weights)
    return (y_prompt, y_sample, p_s5_re, p_s5_im, p_c, p_n, p_m, s_s5_re, s_s5_im, s_c, s_n, s_m)
```

```python
import functools

import jax
import jax.numpy as jnp
from jax import lax
from jax.experimental import pallas as pl
from jax.experimental.pallas import tpu as pltpu

F32 = jnp.float32
BF16 = jnp.bfloat16

D_MODEL = 2048
BATCH = 4
SEQ = 2048
DEPTH = 4
DEC_BATCH = 128
D_FF = 5504
S5_GROUPS = 128
S5_GROUP = 16
S5_STATE = 64
S5_LANES = S5_GROUPS * S5_STATE
ML_HEADS = 4
ML_DK = 256
ML_DV = 512
ML_CHUNK = 128
ML_QKVO = 2 * ML_HEADS * ML_DK + 2 * ML_HEADS * ML_DV
DN_ALPHA = (2.0 * DEPTH) ** 0.25
LN_EPS = 1e-5
K_SCALE = ML_DK ** -0.5

LANE = 128
ROWS_PROMPT = SEQ * BATCH
ROWS_ALL = ROWS_PROMPT + DEC_BATCH
TM = 640
TF = 512
FF_PAD = 11 * TF
TN = 512
S5_OCT = 8
S5_NOCT = S5_GROUPS // S5_OCT
S5_OCT_CH = S5_OCT * S5_GROUP
S5_OCT_ST = S5_OCT * S5_STATE
S5_STEPS = 256
ML_DEC_BB = 8
VMEM_LIMIT = 56 * 1024 * 1024


def _dot(a, b):
    return jnp.dot(a, b, preferred_element_type=F32)


def _layer_norm_rows(y, g, b):
    mu = jnp.mean(y, axis=-1, keepdims=True)
    yc = y - mu
    var = jnp.mean(yc * yc, axis=-1, keepdims=True)
    return yc * lax.rsqrt(var + LN_EPS) * g + b


def _log_sigmoid(x):
    return jnp.minimum(x, 0.0) - jnp.log1p(jnp.exp(-jnp.abs(x)))


def _ffn_kernel(x_ref, wg_ref, wu_ref, wd_ref, g_ref, b_ref, o_ref, xb_ref, *, nf):
    f = pl.program_id(1)

    @pl.when(f == 0)
    def _():
        xb_ref[...] = x_ref[...].astype(BF16)

    xb = xb_ref[...]
    hg = _dot(xb, wg_ref[...])
    hu = _dot(xb, wu_ref[...])
    act = (hg * jax.nn.sigmoid(hg) * hu).astype(BF16)
    part = _dot(act, wd_ref[...])

    @pl.when(f == 0)
    def _():
        o_ref[...] = part

    @pl.when(f > 0)
    def _():
        o_ref[...] += part

    @pl.when(f == nf - 1)
    def _():
        y = DN_ALPHA * x_ref[...] + 0.5 * o_ref[...]
        o_ref[...] = _layer_norm_rows(y, g_ref[...], b_ref[...])


def _ffn_ln(x, wg, wu, wd, ln_g, ln_b, li, k, kn):
    m = x.shape[0]
    nf = FF_PAD // TF
    return pl.pallas_call(
        functools.partial(_ffn_kernel, nf=nf),
        grid=(m // TM, nf),
        in_specs=[
            pl.BlockSpec((TM, D_MODEL), lambda i, f: (i, 0)),
            pl.BlockSpec((None, None, D_MODEL, TF), lambda i, f: (li, k, 0, f)),
            pl.BlockSpec((None, None, D_MODEL, TF), lambda i, f: (li, k, 0, f)),
            pl.BlockSpec((None, None, TF, D_MODEL), lambda i, f: (li, k, f, 0)),
            pl.BlockSpec((None, None, 1, D_MODEL), lambda i, f: (li, kn, 0, 0)),
            pl.BlockSpec((None, None, 1, D_MODEL), lambda i, f: (li, kn, 0, 0)),
        ],
        out_specs=pl.BlockSpec((TM, D_MODEL), lambda i, f: (i, 0)),
        out_shape=jax.ShapeDtypeStruct((m, D_MODEL), F32),
        scratch_shapes=[pltpu.VMEM((TM, D_MODEL), BF16)],
        compiler_params=pltpu.CompilerParams(
            dimension_semantics=("parallel", "arbitrary"), vmem_limit_bytes=VMEM_LIMIT),
        name="ffn_ln",
    )(x, wg, wu, wd, ln_g, ln_b)


def _proj_ln_kernel(x_ref, z_ref, *refs, gated, nn):
    w_refs, (g_ref, b_ref, o_ref, mix_ref) = refs[:-4], refs[-4:]
    n = pl.program_id(1)
    z = z_ref[...]
    p = _dot(z, w_refs[0][...])
    if gated:
        p = p * jax.nn.sigmoid(_dot(z, w_refs[1][...]))
    mix_ref[n] = p

    @pl.when(n == nn - 1)
    def _():
        ys = [DN_ALPHA * x_ref[:, j * TN:(j + 1) * TN] + mix_ref[j] for j in range(nn)]
        mu = sum(jnp.sum(y, axis=-1, keepdims=True) for y in ys) * (1.0 / D_MODEL)
        var = sum(jnp.sum((y - mu) * (y - mu), axis=-1, keepdims=True) for y in ys) * (1.0 / D_MODEL)
        r = lax.rsqrt(var + LN_EPS)
        for j in range(nn):
            sl = slice(j * TN, (j + 1) * TN)
            o_ref[:, sl] = (ys[j] - mu) * r * g_ref[:, sl] + b_ref[:, sl]


def _proj_ln(x, z, ws, ln_g, ln_b, lj, li):
    m = x.shape[0]
    nn = D_MODEL // TN
    gated = len(ws) == 2
    row_spec = pl.BlockSpec((TM, D_MODEL), lambda i, n: (i, 0))
    w_spec = pl.BlockSpec((None, D_MODEL, TN), lambda i, n: (lj, 0, n))
    ln_spec = pl.BlockSpec((None, None, 1, D_MODEL), lambda i, n: (li, 1, 0, 0))
    return pl.pallas_call(
        functools.partial(_proj_ln_kernel, gated=gated, nn=nn),
        grid=(m // TM, nn),
        in_specs=[row_spec, row_spec] + [w_spec] * len(ws) + [ln_spec, ln_spec],
        out_specs=row_spec,
        out_shape=jax.ShapeDtypeStruct((m, D_MODEL), F32),
        scratch_shapes=[pltpu.VMEM((nn, TM, TN), F32)],
        compiler_params=pltpu.CompilerParams(
            dimension_semantics=("parallel", "arbitrary"), vmem_limit_bytes=VMEM_LIMIT),
        name="glu_ln" if gated else "out_ln",
    )(x, z, *ws, ln_g, ln_b)


def _in_proj_kernel(x_ref, w_ref, wg_ref, p_ref, gt_ref, xb_ref):
    n = pl.program_id(1)

    @pl.when(n == 0)
    def _():
        xb = x_ref[...].astype(BF16)
        xb_ref[...] = xb
        gt_ref[...] = _dot(xb, wg_ref[...])

    p_ref[...] = _dot(xb_ref[...], w_ref[...])


def _in_proj(x, w_main, w_gate, lj):
    m = x.shape[0]
    return pl.pallas_call(
        _in_proj_kernel,
        grid=(m // TM, ML_QKVO // TN),
        in_specs=[
            pl.BlockSpec((TM, D_MODEL), lambda i, n: (i, 0)),
            pl.BlockSpec((None, D_MODEL, TN), lambda i, n: (lj, 0, n)),
            pl.BlockSpec((None, D_MODEL, LANE), lambda i, n: (lj, 0, 0)),
        ],
        out_specs=[
            pl.BlockSpec((TM, TN), lambda i, n: (i, n)),
            pl.BlockSpec((TM, LANE), lambda i, n: (i, 0)),
        ],
        out_shape=[jax.ShapeDtypeStruct((m, ML_QKVO), F32), jax.ShapeDtypeStruct((m, LANE), F32)],
        scratch_shapes=[pltpu.VMEM((TM, D_MODEL), BF16)],
        compiler_params=pltpu.CompilerParams(
            dimension_semantics=("parallel", "arbitrary"), vmem_limit_bytes=VMEM_LIMIT),
        name="ml_in_proj",
    )(x, w_main, w_gate)


def _s5_disc_kernel(ar_ref, ai_ref, ldt_ref, br_ref, bi_ref, abr_ref, abi_ref, bbr_ref, bbi_ref):
    ar = ar_ref[...]
    ai = ai_ref[...]
    dt = jnp.exp(ldt_ref[...])
    mag = jnp.exp(dt * ar)
    abr = mag * jnp.cos(dt * ai)
    abi = mag * jnp.sin(dt * ai)
    zr = abr - 1.0
    zi = abi
    den = ar * ar + ai * ai
    gr = (zr * ar + zi * ai) / den
    gi = (zi * ar - zr * ai) / den
    br = br_ref[...]
    bi = bi_ref[...]
    abr_ref[...] = abr
    abi_ref[...] = abi
    bbr_ref[...] = gr * br - gi * bi
    bbi_ref[...] = gr * bi + gi * br


def _s5_discretise(a_re, a_im, log_dt, b_re, b_im):
    g, p, c = S5_GROUPS, S5_STATE, S5_GROUP
    outs = pl.pallas_call(
        _s5_disc_kernel,
        out_shape=[jax.ShapeDtypeStruct((g, 1, p), F32), jax.ShapeDtypeStruct((g, 1, p), F32),
                   jax.ShapeDtypeStruct((g, c, p), F32), jax.ShapeDtypeStruct((g, c, p), F32)],
        name="s5_discretise",
    )(a_re.reshape(g, 1, p), a_im.reshape(g, 1, p), log_dt.reshape(g, 1, 1),
      jnp.swapaxes(b_re, 1, 2), jnp.swapaxes(b_im, 1, 2))
    abr, abi, bbr, bbi = outs
    return abr.reshape(1, g * p), abi.reshape(1, g * p), bbr, bbi


def _block_diag_in(bb):
    eye = jnp.eye(S5_OCT, dtype=F32)
    w = bb.reshape(S5_NOCT, S5_OCT, S5_GROUP, 1, S5_STATE) * eye[None, :, None, :, None]
    return w.reshape(S5_NOCT, S5_OCT_CH, S5_OCT_ST).astype(BF16)


def _block_diag_out(c):
    eye = jnp.eye(S5_OCT, dtype=F32)
    cx = jnp.swapaxes(c.reshape(S5_NOCT, S5_OCT, S5_GROUP, S5_STATE), 2, 3)
    w = cx[:, :, :, None, :] * eye[None, :, None, :, None]
    return w.reshape(S5_NOCT, S5_OCT_ST, S5_OCT_CH).astype(BF16)


def _s5_seq_kernel(u_ref, h0r_ref, h0i_ref, ar_ref, ai_ref, wbr_ref, wbi_ref, wcr_ref, wci_ref, d_ref,
                   z_ref, hr_ref, hi_ref, xr_s, xi_s, sr_s, si_s, *, npairs, nchunks):
    c = pl.program_id(1)

    @pl.when(c == 0)
    def _():
        sr_s[...] = h0r_ref[...]
        si_s[...] = h0i_ref[...]

    u = u_ref[...]
    ub = u.astype(BF16)
    xr_s[...] = _dot(ub, wbr_ref[...])
    xi_s[...] = _dot(ub, wbi_ref[...])

    ar = jnp.broadcast_to(ar_ref[...], (8, S5_OCT_ST))
    ai = jnp.broadcast_to(ai_ref[...], (8, S5_OCT_ST))
    top = lax.broadcasted_iota(jnp.int32, (8, S5_OCT_ST), 0) < BATCH

    def two_steps(i, carry):
        hr, hi = carry
        r0 = pl.multiple_of(i * 8, 8)
        xr = xr_s[pl.ds(r0, 8), :]
        xi = xi_s[pl.ds(r0, 8), :]
        pr = pltpu.roll(hr, BATCH, 0)
        pi = pltpu.roll(hi, BATCH, 0)
        tr = ar * pr - ai * pi + xr
        ti = ar * pi + ai * pr + xi
        qr = pltpu.roll(tr, BATCH, 0)
        qi = pltpu.roll(ti, BATCH, 0)
        vr = ar * qr - ai * qi + xr
        vi = ar * qi + ai * qr + xi
        xr_s[pl.ds(r0, 8), :] = jnp.where(top, tr, vr)
        xi_s[pl.ds(r0, 8), :] = jnp.where(top, ti, vi)
        return vr, vi

    hr, hi = lax.fori_loop(0, npairs, two_steps, (sr_s[...], si_s[...]))
    sr_s[...] = hr
    si_s[...] = hi

    y = _dot(xr_s[...].astype(BF16), wcr_ref[...]) - _dot(xi_s[...].astype(BF16), wci_ref[...])
    y = y + d_ref[...] * u
    z_ref[...] = jax.nn.gelu(y).astype(BF16)

    @pl.when(c == nchunks - 1)
    def _():
        hr_ref[...] = hr
        hi_ref[...] = hi


def _s5_scan_prompt(x_all, h0, abr, abi, wbr, wbi, wcr, wci, d_row):
    rows = S5_STEPS * BATCH
    nchunks = SEQ // S5_STEPS
    oct_w = pl.BlockSpec((None, S5_OCT_CH, S5_OCT_ST), lambda o, c: (o, 0, 0))
    oct_c = pl.BlockSpec((None, S5_OCT_ST, S5_OCT_CH), lambda o, c: (o, 0, 0))
    st_spec = pl.BlockSpec((8, S5_OCT_ST), lambda o, c: (0, o))
    return pl.pallas_call(
        functools.partial(_s5_seq_kernel, npairs=S5_STEPS // 2, nchunks=nchunks),
        grid=(S5_NOCT, nchunks),
        in_specs=[
            pl.BlockSpec((rows, S5_OCT_CH), lambda o, c: (c, o)),
            st_spec, st_spec,
            pl.BlockSpec((1, S5_OCT_ST), lambda o, c: (0, o)),
            pl.BlockSpec((1, S5_OCT_ST), lambda o, c: (0, o)),
            oct_w, oct_w, oct_c, oct_c,
            pl.BlockSpec((1, S5_OCT_CH), lambda o, c: (0, o)),
        ],
        out_specs=[pl.BlockSpec((rows, S5_OCT_CH), lambda o, c: (c, o)), st_spec, st_spec],
        out_shape=[jax.ShapeDtypeStruct((ROWS_PROMPT, D_MODEL), BF16),
                   jax.ShapeDtypeStruct((8, S5_LANES), F32), jax.ShapeDtypeStruct((8, S5_LANES), F32)],
        scratch_shapes=[pltpu.VMEM((rows, S5_OCT_ST), F32), pltpu.VMEM((rows, S5_OCT_ST), F32),
                        pltpu.VMEM((8, S5_OCT_ST), F32), pltpu.VMEM((8, S5_OCT_ST), F32)],
        compiler_params=pltpu.CompilerParams(
            dimension_semantics=("parallel", "arbitrary"), vmem_limit_bytes=VMEM_LIMIT),
        name="s5_scan_prompt",
    )(x_all, h0, h0, abr, abi, wbr, wbi, wcr, wci, d_row)


def _s5_step_kernel(u_ref, h0r_ref, h0i_ref, ar_ref, ai_ref, wbr_ref, wbi_ref, wcr_ref, wci_ref, d_ref,
                    z_ref, hr_ref, hi_ref):
    u = u_ref[...]
    ub = u.astype(BF16)
    xr = _dot(ub, wbr_ref[...])
    xi = _dot(ub, wbi_ref[...])
    ar = ar_ref[...]
    ai = ai_ref[...]
    h0r = h0r_ref[...]
    h0i = h0i_ref[...]
    hr = xr + (ar * h0r - ai * h0i)
    hi = xi + (ar * h0i + ai * h0r)
    hr_ref[...] = hr
    hi_ref[...] = hi
    y = _dot(hr.astype(BF16), wcr_ref[...]) - _dot(hi.astype(BF16), wci_ref[...])
    y = y + d_ref[...] * u
    z_ref[...] = jax.nn.gelu(y).astype(BF16)


def _s5_step_sample(x_all, h0r, h0i, lj, abr, abi, wbr, wbi, wcr, wci, d_row):
    row_blk = ROWS_PROMPT // DEC_BATCH
    oct_w = pl.BlockSpec((None, S5_OCT_CH, S5_OCT_ST), lambda o: (o, 0, 0))
    oct_c = pl.BlockSpec((None, S5_OCT_ST, S5_OCT_CH), lambda o: (o, 0, 0))
    st_in = pl.BlockSpec((None, DEC_BATCH, S5_OCT_ST), lambda o: (lj, 0, o))
    st_out = pl.BlockSpec((DEC_BATCH, S5_OCT_ST), lambda o: (0, o))
    return pl.pallas_call(
        _s5_step_kernel,
        grid=(S5_NOCT,),
        in_specs=[
            pl.BlockSpec((DEC_BATCH, S5_OCT_CH), lambda o: (row_blk, o)),
            st_in, st_in,
            pl.BlockSpec((1, S5_OCT_ST), lambda o: (0, o)),
            pl.BlockSpec((1, S5_OCT_ST), lambda o: (0, o)),
            oct_w, oct_w, oct_c, oct_c,
            pl.BlockSpec((1, S5_OCT_CH), lambda o: (0, o)),
        ],
        out_specs=[pl.BlockSpec((DEC_BATCH, S5_OCT_CH), lambda o: (0, o)), st_out, st_out],
        out_shape=[jax.ShapeDtypeStruct((DEC_BATCH, D_MODEL), BF16),
                   jax.ShapeDtypeStruct((DEC_BATCH, S5_LANES), F32),
                   jax.ShapeDtypeStruct((DEC_BATCH, S5_LANES), F32)],
        compiler_params=pltpu.CompilerParams(
            dimension_semantics=("parallel",), vmem_limit_bytes=VMEM_LIMIT),
        name="s5_step_sample",
    )(x_all, h0r, h0i, abr, abi, wbr, wbi, wcr, wci, d_row)


def _mlstm_seq_kernel(q_ref, k_ref, v_ref, o_ref, gt_ref, bias_ref, ng_ref,
                      hn_ref, cf_ref, nf_ref, mf_ref, c_s, n_s, m_s, *, nchunks):
    ci = pl.program_id(1)
    L = ML_CHUNK

    @pl.when(ci == 0)
    def _():
        c_s[...] = jnp.zeros_like(c_s)
        n_s[...] = jnp.zeros_like(n_s)
        m_s[...] = jnp.zeros_like(m_s)

    gb = gt_ref[...] + bias_ref[...]
    lfa = _log_sigmoid(gb)
    gbt = gb.T
    lft = lfa.T
    row = lax.broadcasted_iota(jnp.int32, (L, L), 0)
    col = lax.broadcasted_iota(jnp.int32, (L, L), 1)
    causal = col <= row
    tril = causal.astype(F32)
    triu = (row <= col).astype(F32)
    bc_cols = jnp.dot(tril, lfa, precision=lax.Precision.HIGHEST, preferred_element_type=F32)
    bc_rows = jnp.dot(lft[0:8, :], triu, precision=lax.Precision.HIGHEST, preferred_element_type=F32)

    for h in range(ML_HEADS):
        ig_row = gbt[h:h + 1, :]
        bc_row = bc_rows[ML_HEADS + h:ML_HEADS + h + 1, :]
        ig_col = gb[:, h:h + 1]
        bc_col = bc_cols[:, ML_HEADS + h:ML_HEADS + h + 1]
        m_prev = m_s[h:h + 1, 0:1]
        inter = bc_col + m_prev
        dmat = jnp.where(causal, bc_col - bc_row + ig_row, -jnp.inf)
        mt = jnp.maximum(inter, jnp.max(dmat, axis=1, keepdims=True))
        wts = jnp.exp(dmat - mt)

        qf = q_ref[:, h * ML_DK:(h + 1) * ML_DK]
        kf = k_ref[:, h * ML_DK:(h + 1) * ML_DK] * K_SCALE
        qb = qf.astype(BF16)
        vb = v_ref[:, h * ML_DV:(h + 1) * ML_DV].astype(BF16)
        qk = lax.dot_general(qb, kf.astype(BF16), (((1,), (1,)), ((), ())), preferred_element_type=F32)
        sc = qk * wts
        scale = jnp.exp(inter - mt)
        cmat = c_s[h]
        n_row = n_s[h:h + 1, :]
        num = scale * _dot(qb, cmat.astype(BF16)) + _dot(sc.astype(BF16), vb)
        qn = jnp.sum(qb.astype(F32) * n_row, axis=1, keepdims=True)
        den = scale * qn + jnp.sum(sc, axis=1, keepdims=True)
        hh = num * (1.0 / jnp.maximum(jnp.abs(den), jnp.exp(-mt)))

        mu = jnp.mean(hh, axis=1, keepdims=True)
        hc = hh - mu
        var = jnp.mean(hc * hc, axis=1, keepdims=True)
        hn = hc * lax.rsqrt(var + LN_EPS) * ng_ref[:, h * ML_DV:(h + 1) * ML_DV]
        og = jax.nn.sigmoid(o_ref[:, h * ML_DV:(h + 1) * ML_DV])
        hn_ref[:, h * ML_DV:(h + 1) * ML_DV] = (og * hn).astype(BF16)

        m_last = mt[L - 1:L, :]
        b_last = bc_col[L - 1:L, :]
        dec = jnp.exp(b_last - bc_col + ig_col - m_last)
        cscale = jnp.exp(b_last + m_prev - m_last)
        kd = kf * dec
        c_new = cscale * cmat + _dot(kd.T.astype(BF16), vb)
        n_new = cscale * n_row + jnp.sum(kd, axis=0, keepdims=True)
        c_s[h] = c_new
        n_s[h:h + 1, :] = n_new
        m_s[h:h + 1, :] = jnp.broadcast_to(m_last, (1, LANE))

        @pl.when(ci == nchunks - 1)
        def _():
            cf_ref[h] = c_new
            nf_ref[h:h + 1, :] = n_new

    @pl.when(ci == nchunks - 1)
    def _():
        mf_ref[...] = m_s[...]


def _mlstm_prompt(proj, gates, bias_row, norm_g, lj):
    nchunks = SEQ // ML_CHUNK
    proj2 = proj.reshape(ROWS_ALL // BATCH, BATCH * ML_QKVO)
    gates2 = gates.reshape(ROWS_ALL // BATCH, BATCH * LANE)
    qk_w = ML_HEADS * ML_DK
    v_w = ML_HEADS * ML_DV
    per_b_qk = ML_QKVO // qk_w
    per_b_v = ML_QKVO // v_w
    return pl.pallas_call(
        functools.partial(_mlstm_seq_kernel, nchunks=nchunks),
        grid=(BATCH, nchunks),
        in_specs=[
            pl.BlockSpec((ML_CHUNK, qk_w), lambda b, c: (c, b * per_b_qk)),
            pl.BlockSpec((ML_CHUNK, qk_w), lambda b, c: (c, b * per_b_qk + 1)),
            pl.BlockSpec((ML_CHUNK, v_w), lambda b, c: (c, b * per_b_v + 1)),
            pl.BlockSpec((ML_CHUNK, v_w), lambda b, c: (c, b * per_b_v + 2)),
            pl.BlockSpec((ML_CHUNK, LANE), lambda b, c: (c, b)),
            pl.BlockSpec((1, LANE), lambda b, c: (0, 0)),
            pl.BlockSpec((None, 1, v_w), lambda b, c: (lj, 0, 0)),
        ],
        out_specs=[
            pl.BlockSpec((ML_CHUNK, v_w), lambda b, c: (c, b)),
            pl.BlockSpec((None, ML_HEADS, ML_DK, ML_DV), lambda b, c: (b, 0, 0, 0)),
            pl.BlockSpec((None, ML_HEADS, ML_DK), lambda b, c: (b, 0, 0)),
            pl.BlockSpec((None, 8, LANE), lambda b, c: (b, 0, 0)),
        ],
        out_shape=[
            jax.ShapeDtypeStruct((SEQ, BATCH * v_w), BF16),
            jax.ShapeDtypeStruct((BATCH, ML_HEADS, ML_DK, ML_DV), F32),
            jax.ShapeDtypeStruct((BATCH, ML_HEADS, ML_DK), F32),
            jax.ShapeDtypeStruct((BATCH, 8, LANE), F32),
        ],
        scratch_shapes=[pltpu.VMEM((ML_HEADS, ML_DK, ML_DV), F32), pltpu.VMEM((8, ML_DK), F32),
                        pltpu.VMEM((8, LANE), F32)],
        compiler_params=pltpu.CompilerParams(
            dimension_semantics=("parallel", "arbitrary"), vmem_limit_bytes=VMEM_LIMIT),
        name="mlstm_prompt",
    )(proj2, proj2, proj2, proj2, gates2, bias_row, norm_g)


def _mlstm_step_kernel(q_ref, k_ref, v_ref, o_ref, ka_ref, va_ref, gt_ref, bias_ref, ng_ref,
                       c_ref, n_ref, m_ref, hn_ref, co_ref, no_ref, mo_ref, kt_s, num_s):
    h = pl.program_id(0)
    i = pl.program_id(1)
    bb = ML_DEC_BB

    @pl.when(i == 0)
    def _():
        kt_s[...] = (ka_ref[...] * K_SCALE).T

    gb = gt_ref[...] + bias_ref[...]
    lane = lax.broadcasted_iota(jnp.int32, (bb, LANE), 1)
    ig = jnp.sum(jnp.where(lane == h, gb, 0.0), axis=1, keepdims=True)
    fp = jnp.sum(jnp.where(lane == h + ML_HEADS, gb, 0.0), axis=1, keepdims=True)
    lf = _log_sigmoid(fp)
    m4 = m_ref[...]
    hl = lax.broadcasted_iota(jnp.int32, (bb, ML_HEADS), 1)
    m_prev = jnp.sum(jnp.where(hl == h, m4, 0.0), axis=1, keepdims=True)
    inter = lf + m_prev
    mt = jnp.maximum(inter, ig)
    wts = jnp.exp(ig - mt)
    scale = jnp.exp(inter - mt)

    qb = q_ref[...].astype(BF16)
    qf = qb.astype(F32)
    kf = k_ref[...] * K_SCALE
    vf = v_ref[...].astype(BF16).astype(F32)
    n_prev = n_ref[...]
    qk = jnp.sum(qf * kf.astype(BF16).astype(F32), axis=1, keepdims=True)
    sc = qk * wts
    qn = jnp.sum(qf * n_prev, axis=1, keepdims=True)
    den = scale * qn + sc
    rden = 1.0 / jnp.maximum(jnp.abs(den), jnp.exp(-mt))

    ktb = kt_s[...].astype(BF16)
    va = va_ref[...]
    rowi = lax.broadcasted_iota(jnp.int32, (DEC_BATCH, ML_DV), 0)
    for j in range(bb):
        cj = c_ref[j]
        qc = _dot(qb, cj.astype(BF16))[j:j + 1, :]
        vsel = jnp.where(rowi == i * bb + j, va, 0.0).astype(BF16)
        outer = _dot(ktb, vsel)
        co_ref[j] = scale[j:j + 1, :] * cj + wts[j:j + 1, :] * outer
        num_s[j:j + 1, :] = scale[j:j + 1, :] * qc + sc[j:j + 1, :] * vf[j:j + 1, :]

    hh = num_s[...] * rden
    mu = jnp.mean(hh, axis=1, keepdims=True)
    hc = hh - mu
    var = jnp.mean(hc * hc, axis=1, keepdims=True)
    hn = hc * lax.rsqrt(var + LN_EPS) * ng_ref[...]
    hn_ref[...] = jax.nn.sigmoid(o_ref[...]) * hn
    no_ref[...] = scale * n_prev + wts * kf
    mo_ref[...] = jnp.broadcast_to(mt, (bb, LANE))


def _mlstm_sample(proj, gates, bias_row, norm_g, c_in, n_in, m_in, c_acc, lj):
    bb = ML_DEC_BB
    r8 = ROWS_PROMPT // bb
    r128 = ROWS_PROMPT // DEC_BATCH
    nh = ML_HEADS
    in_specs = [
        pl.BlockSpec((bb, ML_DK), lambda h, i: (r8 + i, h)),
        pl.BlockSpec((bb, ML_DK), lambda h, i: (r8 + i, nh + h)),
        pl.BlockSpec((bb, ML_DV), lambda h, i: (r8 + i, nh + h)),
        pl.BlockSpec((bb, ML_DV), lambda h, i: (r8 + i, 2 * nh + h)),
        pl.BlockSpec((DEC_BATCH, ML_DK), lambda h, i: (r128, nh + h)),
        pl.BlockSpec((DEC_BATCH, ML_DV), lambda h, i: (r128, nh + h)),
        pl.BlockSpec((bb, LANE), lambda h, i: (r8 + i, 0)),
        pl.BlockSpec((1, LANE), lambda h, i: (0, 0)),
        pl.BlockSpec((None, 1, ML_DV), lambda h, i: (lj, 0, h)),
        pl.BlockSpec((None, bb, None, ML_DK, ML_DV), lambda h, i: (lj, i, h, 0, 0)),
        pl.BlockSpec((None, bb, ML_DK), lambda h, i: (lj, i, h)),
        pl.BlockSpec((None, bb, nh), lambda h, i: (lj, i, 0)),
    ]
    args = [proj, proj, proj, proj, proj, proj, gates, bias_row, norm_g, c_in, n_in, m_in]
    aliases = {}
    if c_acc is not None:
        in_specs.append(pl.BlockSpec(memory_space=pl.ANY))
        args.append(c_acc)
        aliases = {len(args) - 1: 1}

    def body(*refs):
        ins, rest = refs[:12], refs[12 + (c_acc is not None):]
        _mlstm_step_kernel(*ins, *rest)

    return pl.pallas_call(
        body,
        grid=(nh, DEC_BATCH // bb),
        in_specs=in_specs,
        out_specs=[
            pl.BlockSpec((bb, ML_DV), lambda h, i: (i, h)),
            pl.BlockSpec((None, bb, None, ML_DK, ML_DV), lambda h, i: (lj, i, h, 0, 0)),
            pl.BlockSpec((bb, ML_DK), lambda h, i: (i, h)),
            pl.BlockSpec((None, bb, LANE), lambda h, i: (h, i, 0)),
        ],
        out_shape=[
            jax.ShapeDtypeStruct((DEC_BATCH, nh * ML_DV), F32),
            jax.ShapeDtypeStruct(c_in.shape, F32),
            jax.ShapeDtypeStruct((DEC_BATCH, nh * ML_DK), F32),
            jax.ShapeDtypeStruct((nh, DEC_BATCH, LANE), F32),
        ],
        scratch_shapes=[pltpu.VMEM((ML_DK, DEC_BATCH), F32), pltpu.VMEM((bb, ML_DV), F32)],
        input_output_aliases=aliases,
        compiler_params=pltpu.CompilerParams(
            dimension_semantics=("arbitrary", "arbitrary"), vmem_limit_bytes=VMEM_LIMIT),
        name="mlstm_step_sample",
    )(*args)


def kernel(x_prompt, x_sample, state_s5_re, state_s5_im, state_mlstm_C, state_mlstm_n, state_mlstm_m,
           ln_g, ln_b, ffn_w_gate, ffn_w_up, ffn_w_down,
           s5_a_re, s5_a_im, s5_log_dt, s5_b_re, s5_b_im, s5_c_re, s5_c_im, s5_d, s5_w_a, s5_w_b,
           ml_w_in, ml_b_i, ml_b_f, ml_norm_g, ml_w_out):
    n_s5 = s5_a_re.shape[0]
    n_ml = ml_w_in.shape[0]

    fpad = FF_PAD - D_FF
    wg = jnp.pad(ffn_w_gate.astype(BF16), ((0, 0), (0, 0), (0, 0), (0, fpad)))
    wu = jnp.pad(ffn_w_up.astype(BF16), ((0, 0), (0, 0), (0, 0), (0, fpad)))
    wd = jnp.pad(ffn_w_down.astype(BF16), ((0, 0), (0, 0), (0, fpad), (0, 0)))
    glu_a = s5_w_a.astype(BF16)
    glu_b = s5_w_b.astype(BF16)
    w_in_main = ml_w_in[:, :, :ML_QKVO].astype(BF16)
    w_in_gate = jnp.pad(ml_w_in[:, :, ML_QKVO:].astype(BF16), ((0, 0), (0, 0), (0, LANE - 2 * ML_HEADS)))
    w_out = ml_w_out.astype(BF16)
    ln_g4 = ln_g.reshape(DEPTH, 3, 1, D_MODEL)
    ln_b4 = ln_b.reshape(DEPTH, 3, 1, D_MODEL)
    ml_bias = jnp.pad(jnp.concatenate([ml_b_i, ml_b_f], axis=1), ((0, 0), (0, LANE - 2 * ML_HEADS)))
    ml_ng = ml_norm_g.reshape(n_ml, 1, ML_HEADS * ML_DV)
    st_r = state_s5_re.reshape(n_s5, DEC_BATCH, S5_LANES)
    st_i = state_s5_im.reshape(n_s5, DEC_BATCH, S5_LANES)
    st_n = state_mlstm_n.reshape(n_ml, DEC_BATCH, ML_HEADS * ML_DK)

    x = jnp.concatenate([jnp.swapaxes(x_prompt, 0, 1).reshape(ROWS_PROMPT, D_MODEL),
                         x_sample.reshape(DEC_BATCH, D_MODEL)], axis=0)

    p_s5r, p_s5i, s_s5r, s_s5i = [], [], [], []
    p_c, p_n, p_m, s_n, s_m = [], [], [], [], []
    s_c = None
    zero_state = jnp.zeros((8, S5_LANES), F32)
    for li in range(DEPTH):
        x = _ffn_ln(x, wg, wu, wd, ln_g4, ln_b4, li, 0, 0)
        j = li // 2
        if li % 2 == 0:
            abr, abi, bbr, bbi = _s5_discretise(s5_a_re[j], s5_a_im[j], s5_log_dt[j], s5_b_re[j], s5_b_im[j])
            wbr, wbi = _block_diag_in(bbr), _block_diag_in(bbi)
            wcr, wci = _block_diag_out(s5_c_re[j]), _block_diag_out(s5_c_im[j])
            d_row = s5_d[j].reshape(1, D_MODEL)
            zp, hpr, hpi = _s5_scan_prompt(x, zero_state, abr, abi, wbr, wbi, wcr, wci, d_row)
            zs, hsr, hsi = _s5_step_sample(x, st_r, st_i, j, abr, abi, wbr, wbi, wcr, wci, d_row)
            p_s5r.append(hpr[BATCH:].reshape(BATCH, S5_GROUPS, S5_STATE))
            p_s5i.append(hpi[BATCH:].reshape(BATCH, S5_GROUPS, S5_STATE))
            s_s5r.append(hsr.reshape(DEC_BATCH, S5_GROUPS, S5_STATE))
            s_s5i.append(hsi.reshape(DEC_BATCH, S5_GROUPS, S5_STATE))
            z = jnp.concatenate([zp, zs], axis=0)
            x = _proj_ln(x, z, (glu_a, glu_b), ln_g4, ln_b4, j, li)
        else:
            proj, gates = _in_proj(x, w_in_main, w_in_gate, j)
            bias_row = ml_bias[j].reshape(1, LANE)
            hp, cp, npr, mp = _mlstm_prompt(proj, gates, bias_row, ml_ng, j)
            hs, s_c, ns, ms = _mlstm_sample(proj, gates, bias_row, ml_ng, state_mlstm_C, st_n,
                                            state_mlstm_m, s_c, j)
            p_c.append(cp)
            p_n.append(npr)
            p_m.append(mp[:, :ML_HEADS, 0])
            s_n.append(ns.reshape(DEC_BATCH, ML_HEADS, ML_DK))
            s_m.append(jnp.swapaxes(ms[:, :, 0], 0, 1))
            z = jnp.concatenate([hp.reshape(ROWS_PROMPT, ML_HEADS * ML_DV), hs.astype(BF16)], axis=0)
            x = _proj_ln(x, z, (w_out,), ln_g4, ln_b4, j, li)
        x = _ffn_ln(x, wg, wu, wd, ln_g4, ln_b4, li, 1, 2)

    y_prompt = jnp.swapaxes(x[:ROWS_PROMPT].reshape(SEQ, BATCH, D_MODEL), 0, 1)
    y_sample = x[ROWS_PROMPT:].reshape(DEC_BATCH, 1, D_MODEL)
    return (y_prompt, y_sample,
            jnp.stack(p_s5r), jnp.stack(p_s5i), jnp.stack(p_c), jnp.stack(p_n), jnp.stack(p_m),
            jnp.stack(s_s5r), jnp.stack(s_s5i), s_c, jnp.stack(s_n), jnp.stack(s_m))
```

```python
import functools

import jax
import jax.numpy as jnp
from jax import lax
from jax.experimental import pallas as pl
from jax.experimental.pallas import tpu as pltpu

F32 = jnp.float32
BF16 = jnp.bfloat16

D_MODEL = 2048
BATCH = 4
SEQ = 2048
DEPTH = 4
DEC_BATCH = 128
D_FF = 5504
S5_GROUPS = 128
S5_GROUP = 16
S5_STATE = 64
S5_LANES = S5_GROUPS * S5_STATE
ML_HEADS = 4
ML_DK = 256
ML_DV = 512
ML_CHUNK = 128
ML_QKVO = 2 * ML_HEADS * ML_DK + 2 * ML_HEADS * ML_DV
DN_ALPHA = (2.0 * DEPTH) ** 0.25
LN_EPS = 1e-5
K_SCALE = ML_DK ** -0.5

LANE = 128
ROWS_PROMPT = SEQ * BATCH
ROWS_ALL = ROWS_PROMPT + DEC_BATCH
TM = 640
TF = 512
TN = 512
TM_IN = 1040
TN_IN = 1024
S5_OCT = 8
S5_NOCT = S5_GROUPS // S5_OCT
S5_OCT_CH = S5_OCT * S5_GROUP
S5_OCT_ST = S5_OCT * S5_STATE
S5_STEPS = 256
ML_DEC_BB = 16
VMEM_LIMIT = 56 * 1024 * 1024


def _dot(a, b):
    return jnp.dot(a, b, preferred_element_type=F32)


def _layer_norm_rows(y, g, b):
    mu = jnp.mean(y, axis=-1, keepdims=True)
    yc = y - mu
    var = jnp.mean(yc * yc, axis=-1, keepdims=True)
    return yc * lax.rsqrt(var + LN_EPS) * g + b


def _log_sigmoid(x):
    return jnp.minimum(x, 0.0) - jnp.log1p(jnp.exp(-jnp.abs(x)))


def _ffn_kernel(x_ref, wg_ref, wu_ref, wd_ref, g_ref, b_ref, o_ref, xb_ref, *, nf):
    f = pl.program_id(1)

    @pl.when(f == 0)
    def _():
        xb_ref[...] = x_ref[...].astype(BF16)
        o_ref[...] = jnp.zeros_like(o_ref)

    def hidden_tile(width):
        xb = xb_ref[...]
        hg = _dot(xb, wg_ref[:, :width])
        hu = _dot(xb, wu_ref[:, :width])
        act = (hg * jax.nn.sigmoid(hg) * hu).astype(BF16)
        o_ref[...] += _dot(act, wd_ref[:width, :])

    @pl.when(f < nf - 1)
    def _():
        hidden_tile(TF)

    @pl.when(f == nf - 1)
    def _():
        hidden_tile(D_FF - (nf - 1) * TF)
        y = DN_ALPHA * x_ref[...] + 0.5 * o_ref[...]
        o_ref[...] = _layer_norm_rows(y, g_ref[...], b_ref[...])


def _ffn_ln(x, wg, wu, wd, ln_g, ln_b, li, k, kn):
    m = x.shape[0]
    nf = pl.cdiv(D_FF, TF)
    return pl.pallas_call(
        functools.partial(_ffn_kernel, nf=nf),
        grid=(m // TM, nf),
        in_specs=[
            pl.BlockSpec((TM, D_MODEL), lambda i, f: (i, 0)),
            pl.BlockSpec((None, None, D_MODEL, TF), lambda i, f: (li, k, 0, f)),
            pl.BlockSpec((None, None, D_MODEL, TF), lambda i, f: (li, k, 0, f)),
            pl.BlockSpec((None, None, TF, D_MODEL), lambda i, f: (li, k, f, 0)),
            pl.BlockSpec((None, None, 1, D_MODEL), lambda i, f: (li, kn, 0, 0)),
            pl.BlockSpec((None, None, 1, D_MODEL), lambda i, f: (li, kn, 0, 0)),
        ],
        out_specs=pl.BlockSpec((TM, D_MODEL), lambda i, f: (i, 0)),
        out_shape=jax.ShapeDtypeStruct((m, D_MODEL), F32),
        scratch_shapes=[pltpu.VMEM((TM, D_MODEL), BF16)],
        compiler_params=pltpu.CompilerParams(
            dimension_semantics=("parallel", "arbitrary"), vmem_limit_bytes=VMEM_LIMIT),
        name="ffn_ln",
    )(x, wg, wu, wd, ln_g, ln_b)


def _proj_ln_kernel(x_ref, z_ref, *refs, gated, nn):
    w_refs, (g_ref, b_ref, o_ref, mix_ref) = refs[:-4], refs[-4:]
    n = pl.program_id(1)
    z = z_ref[...]
    p = _dot(z, w_refs[0][...])
    if gated:
        p = p * jax.nn.sigmoid(_dot(z, w_refs[1][...]))
    mix_ref[n] = p

    @pl.when(n == nn - 1)
    def _():
        ys = [DN_ALPHA * x_ref[:, j * TN:(j + 1) * TN] + mix_ref[j] for j in range(nn)]
        mu = sum(jnp.sum(y, axis=-1, keepdims=True) for y in ys) * (1.0 / D_MODEL)
        var = sum(jnp.sum((y - mu) * (y - mu), axis=-1, keepdims=True) for y in ys) * (1.0 / D_MODEL)
        r = lax.rsqrt(var + LN_EPS)
        for j in range(nn):
            sl = slice(j * TN, (j + 1) * TN)
            o_ref[:, sl] = (ys[j] - mu) * r * g_ref[:, sl] + b_ref[:, sl]


def _proj_ln(x, z, ws, ln_g, ln_b, lj, li):
    m = x.shape[0]
    nn = D_MODEL // TN
    gated = len(ws) == 2
    row_spec = pl.BlockSpec((TM, D_MODEL), lambda i, n: (i, 0))
    w_spec = pl.BlockSpec((None, D_MODEL, TN), lambda i, n: (lj, 0, n))
    ln_spec = pl.BlockSpec((None, None, 1, D_MODEL), lambda i, n: (li, 1, 0, 0))
    return pl.pallas_call(
        functools.partial(_proj_ln_kernel, gated=gated, nn=nn),
        grid=(m // TM, nn),
        in_specs=[row_spec, row_spec] + [w_spec] * len(ws) + [ln_spec, ln_spec],
        out_specs=row_spec,
        out_shape=jax.ShapeDtypeStruct((m, D_MODEL), F32),
        scratch_shapes=[pltpu.VMEM((nn, TM, TN), F32)],
        compiler_params=pltpu.CompilerParams(
            dimension_semantics=("parallel", "arbitrary"), vmem_limit_bytes=VMEM_LIMIT),
        name="glu_ln" if gated else "out_ln",
    )(x, z, *ws, ln_g, ln_b)


def _in_proj_kernel(x_ref, w_ref, wg_ref, p_ref, gt_ref, xb_ref):
    n = pl.program_id(1)

    @pl.when(n == 0)
    def _():
        xb = x_ref[...].astype(BF16)
        xb_ref[...] = xb
        gt_ref[...] = _dot(xb, wg_ref[...])

    p = _dot(xb_ref[...], w_ref[...])
    for s in range(TN_IN // LANE):
        p_ref[s] = p[:, s * LANE:(s + 1) * LANE]


def _in_proj(x, w_in, w_gate, lj):
    m = x.shape[0]
    slabs = TN_IN // LANE
    return pl.pallas_call(
        _in_proj_kernel,
        grid=(m // TM_IN, ML_QKVO // TN_IN),
        in_specs=[
            pl.BlockSpec((TM_IN, D_MODEL), lambda i, n: (i, 0)),
            pl.BlockSpec((None, D_MODEL, TN_IN), lambda i, n: (lj, 0, n)),
            pl.BlockSpec((None, D_MODEL, LANE), lambda i, n: (lj, 0, 0)),
        ],
        out_specs=[
            pl.BlockSpec((slabs, TM_IN, LANE), lambda i, n: (n, i, 0)),
            pl.BlockSpec((TM_IN, LANE), lambda i, n: (i, 0)),
        ],
        out_shape=[jax.ShapeDtypeStruct((ML_QKVO // LANE, m, LANE), F32), jax.ShapeDtypeStruct((m, LANE), F32)],
        scratch_shapes=[pltpu.VMEM((TM_IN, D_MODEL), BF16)],
        compiler_params=pltpu.CompilerParams(
            dimension_semantics=("parallel", "arbitrary"), vmem_limit_bytes=VMEM_LIMIT),
        name="ml_in_proj",
    )(x, w_in, w_gate)


def _s5_disc_kernel(ar_ref, ai_ref, ldt_ref, br_ref, bi_ref, abr_ref, abi_ref, bbr_ref, bbi_ref):
    ar = ar_ref[...]
    ai = ai_ref[...]
    dt = jnp.exp(ldt_ref[...])
    mag = jnp.exp(dt * ar)
    abr = mag * jnp.cos(dt * ai)
    abi = mag * jnp.sin(dt * ai)
    zr = abr - 1.0
    zi = abi
    den = ar * ar + ai * ai
    gr = (zr * ar + zi * ai) / den
    gi = (zi * ar - zr * ai) / den
    br = br_ref[...]
    bi = bi_ref[...]
    abr_ref[...] = abr
    abi_ref[...] = abi
    bbr_ref[...] = gr * br - gi * bi
    bbi_ref[...] = gr * bi + gi * br


def _s5_discretise(a_re, a_im, log_dt, b_re, b_im):
    g, p, c = S5_GROUPS, S5_STATE, S5_GROUP
    outs = pl.pallas_call(
        _s5_disc_kernel,
        out_shape=[jax.ShapeDtypeStruct((g, 1, p), F32), jax.ShapeDtypeStruct((g, 1, p), F32),
                   jax.ShapeDtypeStruct((g, c, p), F32), jax.ShapeDtypeStruct((g, c, p), F32)],
        name="s5_discretise",
    )(a_re.reshape(g, 1, p), a_im.reshape(g, 1, p), log_dt.reshape(g, 1, 1),
      jnp.swapaxes(b_re, 1, 2), jnp.swapaxes(b_im, 1, 2))
    abr, abi, bbr, bbi = outs
    return abr.reshape(1, g * p), abi.reshape(1, g * p), bbr, bbi


def _block_diag_in(bb):
    eye = jnp.eye(S5_OCT, dtype=F32)
    w = bb.reshape(S5_NOCT, S5_OCT, S5_GROUP, 1, S5_STATE) * eye[None, :, None, :, None]
    return w.reshape(S5_NOCT, S5_OCT_CH, S5_OCT_ST).astype(BF16)


def _block_diag_out(c):
    eye = jnp.eye(S5_OCT, dtype=F32)
    cx = jnp.swapaxes(c.reshape(S5_NOCT, S5_OCT, S5_GROUP, S5_STATE), 2, 3)
    w = cx[:, :, :, None, :] * eye[None, :, None, :, None]
    return w.reshape(S5_NOCT, S5_OCT_ST, S5_OCT_CH).astype(BF16)


def _s5_seq_kernel(u_ref, h0r_ref, h0i_ref, ar_ref, ai_ref, wbr_ref, wbi_ref, wcr_ref, wci_ref, d_ref,
                   z_ref, hr_ref, hi_ref, xr_s, xi_s, sr_s, si_s, *, npairs, nchunks):
    c = pl.program_id(1)

    @pl.when(c == 0)
    def _():
        sr_s[...] = h0r_ref[...]
        si_s[...] = h0i_ref[...]

    u = u_ref[...]
    ub = u.astype(BF16)
    xr_s[...] = _dot(ub, wbr_ref[...])
    xi_s[...] = _dot(ub, wbi_ref[...])

    ar = jnp.broadcast_to(ar_ref[...], (8, S5_OCT_ST))
    ai = jnp.broadcast_to(ai_ref[...], (8, S5_OCT_ST))
    top = lax.broadcasted_iota(jnp.int32, (8, S5_OCT_ST), 0) < BATCH

    def two_steps(i, carry):
        hr, hi = carry
        r0 = pl.multiple_of(i * 8, 8)
        xr = xr_s[pl.ds(r0, 8), :]
        xi = xi_s[pl.ds(r0, 8), :]
        pr = pltpu.roll(hr, BATCH, 0)
        pi = pltpu.roll(hi, BATCH, 0)
        tr = ar * pr - ai * pi + xr
        ti = ar * pi + ai * pr + xi
        qr = pltpu.roll(tr, BATCH, 0)
        qi = pltpu.roll(ti, BATCH, 0)
        vr = ar * qr - ai * qi + xr
        vi = ar * qi + ai * qr + xi
        xr_s[pl.ds(r0, 8), :] = jnp.where(top, tr, vr)
        xi_s[pl.ds(r0, 8), :] = jnp.where(top, ti, vi)
        return vr, vi

    hr, hi = lax.fori_loop(0, npairs, two_steps, (sr_s[...], si_s[...]))
    sr_s[...] = hr
    si_s[...] = hi

    y = _dot(xr_s[...].astype(BF16), wcr_ref[...]) - _dot(xi_s[...].astype(BF16), wci_ref[...])
    y = y + d_ref[...] * u
    z_ref[...] = jax.nn.gelu(y).astype(BF16)

    @pl.when(c == nchunks - 1)
    def _():
        hr_ref[...] = hr
        hi_ref[...] = hi


def _s5_scan_prompt(x_all, h0, abr, abi, wbr, wbi, wcr, wci, d_row):
    rows = S5_STEPS * BATCH
    nchunks = SEQ // S5_STEPS
    oct_w = pl.BlockSpec((None, S5_OCT_CH, S5_OCT_ST), lambda o, c: (o, 0, 0))
    oct_c = pl.BlockSpec((None, S5_OCT_ST, S5_OCT_CH), lambda o, c: (o, 0, 0))
    st_spec = pl.BlockSpec((8, S5_OCT_ST), lambda o, c: (0, o))
    return pl.pallas_call(
        functools.partial(_s5_seq_kernel, npairs=S5_STEPS // 2, nchunks=nchunks),
        grid=(S5_NOCT, nchunks),
        in_specs=[
            pl.BlockSpec((rows, S5_OCT_CH), lambda o, c: (c, o)),
            st_spec, st_spec,
            pl.BlockSpec((1, S5_OCT_ST), lambda o, c: (0, o)),
            pl.BlockSpec((1, S5_OCT_ST), lambda o, c: (0, o)),
            oct_w, oct_w, oct_c, oct_c,
            pl.BlockSpec((1, S5_OCT_CH), lambda o, c: (0, o)),
        ],
        out_specs=[pl.BlockSpec((rows, S5_OCT_CH), lambda o, c: (c, o)), st_spec, st_spec],
        out_shape=[jax.ShapeDtypeStruct((ROWS_ALL, D_MODEL), BF16),
                   jax.ShapeDtypeStruct((8, S5_LANES), F32), jax.ShapeDtypeStruct((8, S5_LANES), F32)],
        scratch_shapes=[pltpu.VMEM((rows, S5_OCT_ST), F32), pltpu.VMEM((rows, S5_OCT_ST), F32),
                        pltpu.VMEM((8, S5_OCT_ST), F32), pltpu.VMEM((8, S5_OCT_ST), F32)],
        compiler_params=pltpu.CompilerParams(
            dimension_semantics=("parallel", "arbitrary"), vmem_limit_bytes=VMEM_LIMIT),
        name="s5_scan_prompt",
    )(x_all, h0, h0, abr, abi, wbr, wbi, wcr, wci, d_row)


def _s5_step_kernel(u_ref, h0r_ref, h0i_ref, ar_ref, ai_ref, wbr_ref, wbi_ref, wcr_ref, wci_ref, d_ref,
                    z_all_ref, z_ref, hr_ref, hi_ref):
    del z_all_ref
    u = u_ref[...]
    ub = u.astype(BF16)
    xr = _dot(ub, wbr_ref[...])
    xi = _dot(ub, wbi_ref[...])
    ar = ar_ref[...]
    ai = ai_ref[...]
    h0r = h0r_ref[...]
    h0i = h0i_ref[...]
    hr = xr + (ar * h0r - ai * h0i)
    hi = xi + (ar * h0i + ai * h0r)
    hr_ref[...] = hr
    hi_ref[...] = hi
    y = _dot(hr.astype(BF16), wcr_ref[...]) - _dot(hi.astype(BF16), wci_ref[...])
    y = y + d_ref[...] * u
    z_ref[...] = jax.nn.gelu(y).astype(BF16)


def _s5_step_sample(x_all, h0r, h0i, lj, abr, abi, wbr, wbi, wcr, wci, d_row, z_all):
    row_blk = ROWS_PROMPT // DEC_BATCH
    oct_w = pl.BlockSpec((None, S5_OCT_CH, S5_OCT_ST), lambda o: (o, 0, 0))
    oct_c = pl.BlockSpec((None, S5_OCT_ST, S5_OCT_CH), lambda o: (o, 0, 0))
    st_in = pl.BlockSpec((None, DEC_BATCH, S5_OCT_ST), lambda o: (lj, 0, o))
    st_out = pl.BlockSpec((DEC_BATCH, S5_OCT_ST), lambda o: (0, o))
    return pl.pallas_call(
        _s5_step_kernel,
        grid=(S5_NOCT,),
        in_specs=[
            pl.BlockSpec((DEC_BATCH, S5_OCT_CH), lambda o: (row_blk, o)),
            st_in, st_in,
            pl.BlockSpec((1, S5_OCT_ST), lambda o: (0, o)),
            pl.BlockSpec((1, S5_OCT_ST), lambda o: (0, o)),
            oct_w, oct_w, oct_c, oct_c,
            pl.BlockSpec((1, S5_OCT_CH), lambda o: (0, o)),
            pl.BlockSpec(memory_space=pl.ANY),
        ],
        out_specs=[pl.BlockSpec((DEC_BATCH, S5_OCT_CH), lambda o: (row_blk, o)), st_out, st_out],
        out_shape=[jax.ShapeDtypeStruct((ROWS_ALL, D_MODEL), BF16),
                   jax.ShapeDtypeStruct((DEC_BATCH, S5_LANES), F32),
                   jax.ShapeDtypeStruct((DEC_BATCH, S5_LANES), F32)],
        input_output_aliases={10: 0},
        compiler_params=pltpu.CompilerParams(
            dimension_semantics=("parallel",), vmem_limit_bytes=VMEM_LIMIT),
        name="s5_step_sample",
    )(x_all, h0r, h0i, abr, abi, wbr, wbi, wcr, wci, d_row, z_all)


def _mlstm_seq_kernel(q_ref, k_ref, v_ref, o_ref, gt_ref, bias_ref, ng_ref,
                      hn_ref, cf_ref, nf_ref, mf_ref, c_s, n_s, m_s, hn_s, *, nchunks):
    ci = pl.program_id(0)
    b = pl.program_id(1)
    L = ML_CHUNK
    rows = pl.ds(b, L, stride=BATCH)

    def seq_rows(ref, h, width):
        per_head = width // LANE
        return jnp.concatenate([ref[h * per_head + j, rows, :] for j in range(per_head)], axis=1)

    @pl.when(ci == 0)
    def _():
        c_s[b] = jnp.zeros(c_s.shape[1:], F32)
        n_s[b] = jnp.zeros(n_s.shape[1:], F32)
        m_s[b] = jnp.zeros(m_s.shape[1:], F32)

    gb = gt_ref[rows, :] + bias_ref[...]
    lfa = _log_sigmoid(gb)
    gbt = gb.T
    lft = lfa.T
    row = lax.broadcasted_iota(jnp.int32, (L, L), 0)
    col = lax.broadcasted_iota(jnp.int32, (L, L), 1)
    causal = col <= row
    tril = causal.astype(F32)
    triu = (row <= col).astype(F32)
    bc_cols = jnp.dot(tril, lfa, precision=lax.Precision.HIGHEST, preferred_element_type=F32)
    bc_rows = jnp.dot(lft[0:8, :], triu, precision=lax.Precision.HIGHEST, preferred_element_type=F32)

    for h in range(ML_HEADS):
        ig_row = gbt[h:h + 1, :]
        bc_row = bc_rows[ML_HEADS + h:ML_HEADS + h + 1, :]
        ig_col = gb[:, h:h + 1]
        bc_col = bc_cols[:, ML_HEADS + h:ML_HEADS + h + 1]
        m_prev = m_s[b, h:h + 1, 0:1]
        inter = bc_col + m_prev
        dmat = jnp.where(causal, bc_col - bc_row + ig_row, -jnp.inf)
        mt = jnp.maximum(inter, jnp.max(dmat, axis=1, keepdims=True))
        wts = jnp.exp(dmat - mt)

        qf = seq_rows(q_ref, h, ML_DK)
        kf = seq_rows(k_ref, h, ML_DK) * K_SCALE
        qb = qf.astype(BF16)
        vb = seq_rows(v_ref, h, ML_DV).astype(BF16)
        qk = lax.dot_general(qb, kf.astype(BF16), (((1,), (1,)), ((), ())), preferred_element_type=F32)
        sc = qk * wts
        scale = jnp.exp(inter - mt)
        cmat = c_s[b, h]
        n_row = n_s[b, h:h + 1, :]
        num = scale * _dot(qb, cmat.astype(BF16)) + _dot(sc.astype(BF16), vb)
        qn = jnp.sum(qb.astype(F32) * n_row, axis=1, keepdims=True)
        den = scale * qn + jnp.sum(sc, axis=1, keepdims=True)
        hh = num * (1.0 / jnp.maximum(jnp.abs(den), jnp.exp(-mt)))

        mu = jnp.mean(hh, axis=1, keepdims=True)
        hc = hh - mu
        var = jnp.mean(hc * hc, axis=1, keepdims=True)
        hn = hc * lax.rsqrt(var + LN_EPS) * ng_ref[:, h * ML_DV:(h + 1) * ML_DV]
        gated = jax.nn.sigmoid(seq_rows(o_ref, h, ML_DV)) * hn
        for j in range(ML_DV // LANE):
            hn_s[h * (ML_DV // LANE) + j, rows, :] = gated[:, j * LANE:(j + 1) * LANE]

        m_last = mt[L - 1:L, :]
        b_last = bc_col[L - 1:L, :]
        dec = jnp.exp(b_last - bc_col + ig_col - m_last)
        cscale = jnp.exp(b_last + m_prev - m_last)
        kd = kf * dec
        c_new = cscale * cmat + _dot(kd.T.astype(BF16), vb)
        n_new = cscale * n_row + jnp.sum(kd, axis=0, keepdims=True)
        c_s[b, h] = c_new
        n_s[b, h:h + 1, :] = n_new
        m_s[b, h:h + 1, :] = jnp.broadcast_to(m_last, (1, LANE))

    @pl.when(ci == nchunks - 1)
    def _():
        cf_ref[...] = c_s[b]
        nf_ref[...] = n_s[b, 0:ML_HEADS, :]
        mf_ref[...] = m_s[b]

    @pl.when(b == BATCH - 1)
    def _():
        for s in range(ML_HEADS * ML_DV // LANE):
            hn_ref[:, s * LANE:(s + 1) * LANE] = hn_s[s].astype(BF16)


def _mlstm_prompt(proj, gates, bias_row, norm_g, lj):
    nchunks = SEQ // ML_CHUNK
    rows = ML_CHUNK * BATCH
    qk_s = ML_HEADS * ML_DK // LANE
    v_s = ML_HEADS * ML_DV // LANE
    v_w = ML_HEADS * ML_DV
    last = nchunks - 1

    def state_idx(c, b):
        return jnp.where(c == last, b, 0)

    return pl.pallas_call(
        functools.partial(_mlstm_seq_kernel, nchunks=nchunks),
        grid=(nchunks, BATCH),
        in_specs=[
            pl.BlockSpec((qk_s, rows, LANE), lambda c, b: (0, c, 0)),
            pl.BlockSpec((qk_s, rows, LANE), lambda c, b: (1, c, 0)),
            pl.BlockSpec((v_s, rows, LANE), lambda c, b: (1, c, 0)),
            pl.BlockSpec((v_s, rows, LANE), lambda c, b: (2, c, 0)),
            pl.BlockSpec((rows, LANE), lambda c, b: (c, 0)),
            pl.BlockSpec((1, LANE), lambda c, b: (0, 0)),
            pl.BlockSpec((None, 1, v_w), lambda c, b: (lj, 0, 0)),
        ],
        out_specs=[
            pl.BlockSpec((rows, v_w), lambda c, b: (c, 0)),
            pl.BlockSpec((None, ML_HEADS, ML_DK, ML_DV), lambda c, b: (state_idx(c, b), 0, 0, 0)),
            pl.BlockSpec((None, ML_HEADS, ML_DK), lambda c, b: (state_idx(c, b), 0, 0)),
            pl.BlockSpec((None, 8, LANE), lambda c, b: (state_idx(c, b), 0, 0)),
        ],
        out_shape=[
            jax.ShapeDtypeStruct((ROWS_ALL, v_w), BF16),
            jax.ShapeDtypeStruct((BATCH, ML_HEADS, ML_DK, ML_DV), F32),
            jax.ShapeDtypeStruct((BATCH, ML_HEADS, ML_DK), F32),
            jax.ShapeDtypeStruct((BATCH, 8, LANE), F32),
        ],
        scratch_shapes=[pltpu.VMEM((BATCH, ML_HEADS, ML_DK, ML_DV), F32), pltpu.VMEM((BATCH, 8, ML_DK), F32),
                        pltpu.VMEM((BATCH, 8, LANE), F32), pltpu.VMEM((v_s, rows, LANE), F32)],
        compiler_params=pltpu.CompilerParams(
            dimension_semantics=("arbitrary", "arbitrary"), vmem_limit_bytes=VMEM_LIMIT),
        name="mlstm_prompt",
    )(proj, proj, proj, proj, gates, bias_row, norm_g)


def _mlstm_step_kernel(q_ref, k_ref, v_ref, o_ref, ka_ref, va_ref, gt_ref, bias_ref, ng_ref,
                       c_ref, n_ref, m_ref, hn_ref, co_ref, no_ref, mo_ref, kt_s, num_s):
    h = pl.program_id(0)
    i = pl.program_id(1)
    bb = ML_DEC_BB

    def cols(ref):
        return jnp.concatenate([ref[s] for s in range(ref.shape[0])], axis=1)

    @pl.when(i == 0)
    def _():
        kt_s[...] = (cols(ka_ref) * K_SCALE).T

    gb = gt_ref[...] + bias_ref[...]
    lane = lax.broadcasted_iota(jnp.int32, (bb, LANE), 1)
    ig = jnp.sum(jnp.where(lane == h, gb, 0.0), axis=1, keepdims=True)
    fp = jnp.sum(jnp.where(lane == h + ML_HEADS, gb, 0.0), axis=1, keepdims=True)
    lf = _log_sigmoid(fp)
    m4 = m_ref[...]
    hl = lax.broadcasted_iota(jnp.int32, (bb, ML_HEADS), 1)
    m_prev = jnp.sum(jnp.where(hl == h, m4, 0.0), axis=1, keepdims=True)
    inter = lf + m_prev
    mt = jnp.maximum(inter, ig)
    wts = jnp.exp(ig - mt)
    scale = jnp.exp(inter - mt)

    qb = cols(q_ref).astype(BF16)
    qf = qb.astype(F32)
    kf = cols(k_ref) * K_SCALE
    vf = cols(v_ref).astype(BF16).astype(F32)
    n_prev = n_ref[...]
    qk = jnp.sum(qf * kf.astype(BF16).astype(F32), axis=1, keepdims=True)
    sc = qk * wts
    qn = jnp.sum(qf * n_prev, axis=1, keepdims=True)
    den = scale * qn + sc
    rden = 1.0 / jnp.maximum(jnp.abs(den), jnp.exp(-mt))

    ktb = kt_s[...].astype(BF16)
    va = cols(va_ref)
    rowi = lax.broadcasted_iota(jnp.int32, (DEC_BATCH, ML_DV), 0)
    for j in range(bb):
        cj = c_ref[j]
        qc = _dot(qb, cj.astype(BF16))[j:j + 1, :]
        vsel = jnp.where(rowi == i * bb + j, va, 0.0).astype(BF16)
        outer = _dot(ktb, vsel)
        co_ref[j] = scale[j:j + 1, :] * cj + wts[j:j + 1, :] * outer
        num_s[j:j + 1, :] = scale[j:j + 1, :] * qc + sc[j:j + 1, :] * vf[j:j + 1, :]

    hh = num_s[...] * rden
    mu = jnp.mean(hh, axis=1, keepdims=True)
    hc = hh - mu
    var = jnp.mean(hc * hc, axis=1, keepdims=True)
    hn = hc * lax.rsqrt(var + LN_EPS) * ng_ref[...]
    hn_ref[...] = (jax.nn.sigmoid(cols(o_ref)) * hn).astype(BF16)
    no_ref[...] = scale * n_prev + wts * kf
    mo_ref[...] = jnp.broadcast_to(mt, (bb, LANE))


def _mlstm_sample(proj, gates, bias_row, norm_g, c_in, n_in, m_in, hn_all, c_acc, lj):
    bb = ML_DEC_BB
    rb = ROWS_PROMPT // bb
    r128 = ROWS_PROMPT // DEC_BATCH
    nh = ML_HEADS
    ks = ML_DK // LANE
    vs = ML_DV // LANE
    in_specs = [
        pl.BlockSpec((ks, bb, LANE), lambda h, i: (h, rb + i, 0)),
        pl.BlockSpec((ks, bb, LANE), lambda h, i: (nh + h, rb + i, 0)),
        pl.BlockSpec((vs, bb, LANE), lambda h, i: (nh + h, rb + i, 0)),
        pl.BlockSpec((vs, bb, LANE), lambda h, i: (2 * nh + h, rb + i, 0)),
        pl.BlockSpec((ks, DEC_BATCH, LANE), lambda h, i: (nh + h, r128, 0)),
        pl.BlockSpec((vs, DEC_BATCH, LANE), lambda h, i: (nh + h, r128, 0)),
        pl.BlockSpec((bb, LANE), lambda h, i: (rb + i, 0)),
        pl.BlockSpec((1, LANE), lambda h, i: (0, 0)),
        pl.BlockSpec((None, 1, ML_DV), lambda h, i: (lj, 0, h)),
        pl.BlockSpec((None, bb, None, ML_DK, ML_DV), lambda h, i: (lj, i, h, 0, 0)),
        pl.BlockSpec((None, bb, ML_DK), lambda h, i: (lj, i, h)),
        pl.BlockSpec((None, bb, nh), lambda h, i: (lj, i, 0)),
        pl.BlockSpec(memory_space=pl.ANY),
    ]
    args = [proj, proj, proj, proj, proj, proj, gates, bias_row, norm_g, c_in, n_in, m_in, hn_all]
    n_blocked = 12
    aliases = {n_blocked: 0}
    if c_acc is not None:
        in_specs.append(pl.BlockSpec(memory_space=pl.ANY))
        args.append(c_acc)
        aliases[n_blocked + 1] = 1

    def body(*refs):
        _mlstm_step_kernel(*refs[:n_blocked], *refs[len(args):])

    return pl.pallas_call(
        body,
        grid=(nh, DEC_BATCH // bb),
        in_specs=in_specs,
        out_specs=[
            pl.BlockSpec((bb, ML_DV), lambda h, i: (rb + i, h)),
            pl.BlockSpec((None, bb, None, ML_DK, ML_DV), lambda h, i: (lj, i, h, 0, 0)),
            pl.BlockSpec((bb, ML_DK), lambda h, i: (i, h)),
            pl.BlockSpec((None, bb, LANE), lambda h, i: (h, i, 0)),
        ],
        out_shape=[
            jax.ShapeDtypeStruct((ROWS_ALL, nh * ML_DV), BF16),
            jax.ShapeDtypeStruct(c_in.shape, F32),
            jax.ShapeDtypeStruct((DEC_BATCH, nh * ML_DK), F32),
            jax.ShapeDtypeStruct((nh, DEC_BATCH, LANE), F32),
        ],
        scratch_shapes=[pltpu.VMEM((ML_DK, DEC_BATCH), F32), pltpu.VMEM((bb, ML_DV), F32)],
        input_output_aliases=aliases,
        compiler_params=pltpu.CompilerParams(
            dimension_semantics=("arbitrary", "arbitrary"), vmem_limit_bytes=VMEM_LIMIT),
        name="mlstm_step_sample",
    )(*args)


def kernel(x_prompt, x_sample, state_s5_re, state_s5_im, state_mlstm_C, state_mlstm_n, state_mlstm_m,
           ln_g, ln_b, ffn_w_gate, ffn_w_up, ffn_w_down,
           s5_a_re, s5_a_im, s5_log_dt, s5_b_re, s5_b_im, s5_c_re, s5_c_im, s5_d, s5_w_a, s5_w_b,
           ml_w_in, ml_b_i, ml_b_f, ml_norm_g, ml_w_out):
    n_s5 = s5_a_re.shape[0]
    n_ml = ml_w_in.shape[0]

    wg = ffn_w_gate.astype(BF16)
    wu = ffn_w_up.astype(BF16)
    wd = ffn_w_down.astype(BF16)
    glu_a = s5_w_a.astype(BF16)
    glu_b = s5_w_b.astype(BF16)
    w_in = ml_w_in.astype(BF16)
    w_in_gate = jnp.pad(ml_w_in[:, :, ML_QKVO:].astype(BF16), ((0, 0), (0, 0), (0, LANE - 2 * ML_HEADS)))
    w_out = ml_w_out.astype(BF16)
    ln_g4 = ln_g.reshape(DEPTH, 3, 1, D_MODEL)
    ln_b4 = ln_b.reshape(DEPTH, 3, 1, D_MODEL)
    ml_bias = jnp.pad(jnp.concatenate([ml_b_i, ml_b_f], axis=1), ((0, 0), (0, LANE - 2 * ML_HEADS)))
    ml_ng = ml_norm_g.reshape(n_ml, 1, ML_HEADS * ML_DV)
    st_r = state_s5_re.reshape(n_s5, DEC_BATCH, S5_LANES)
    st_i = state_s5_im.reshape(n_s5, DEC_BATCH, S5_LANES)
    st_n = state_mlstm_n.reshape(n_ml, DEC_BATCH, ML_HEADS * ML_DK)

    x = jnp.concatenate([jnp.swapaxes(x_prompt, 0, 1).reshape(ROWS_PROMPT, D_MODEL),
                         x_sample.reshape(DEC_BATCH, D_MODEL)], axis=0)

    p_s5r, p_s5i, s_s5r, s_s5i = [], [], [], []
    p_c, p_n, p_m, s_n, s_m = [], [], [], [], []
    s_c = None
    zero_state = jnp.zeros((8, S5_LANES), F32)
    for li in range(DEPTH):
        x = _ffn_ln(x, wg, wu, wd, ln_g4, ln_b4, li, 0, 0)
        j = li // 2
        if li % 2 == 0:
            abr, abi, bbr, bbi = _s5_discretise(s5_a_re[j], s5_a_im[j], s5_log_dt[j], s5_b_re[j], s5_b_im[j])
            wbr, wbi = _block_diag_in(bbr), _block_diag_in(bbi)
            wcr, wci = _block_diag_out(s5_c_re[j]), _block_diag_out(s5_c_im[j])
            d_row = s5_d[j].reshape(1, D_MODEL)
            z, hpr, hpi = _s5_scan_prompt(x, zero_state, abr, abi, wbr, wbi, wcr, wci, d_row)
            z, hsr, hsi = _s5_step_sample(x, st_r, st_i, j, abr, abi, wbr, wbi, wcr, wci, d_row, z)
            p_s5r.append(hpr[BATCH:].reshape(BATCH, S5_GROUPS, S5_STATE))
            p_s5i.append(hpi[BATCH:].reshape(BATCH, S5_GROUPS, S5_STATE))
            s_s5r.append(hsr.reshape(DEC_BATCH, S5_GROUPS, S5_STATE))
            s_s5i.append(hsi.reshape(DEC_BATCH, S5_GROUPS, S5_STATE))
            x = _proj_ln(x, z, (glu_a, glu_b), ln_g4, ln_b4, j, li)
        else:
            proj, gates = _in_proj(x, w_in, w_in_gate, j)
            bias_row = ml_bias[j].reshape(1, LANE)
            z, cp, npr, mp = _mlstm_prompt(proj, gates, bias_row, ml_ng, j)
            z, s_c, ns, ms = _mlstm_sample(proj, gates, bias_row, ml_ng, state_mlstm_C, st_n,
                                           state_mlstm_m, z, s_c, j)
            p_c.append(cp)
            p_n.append(npr)
            p_m.append(mp[:, :ML_HEADS, 0])
            s_n.append(ns.reshape(DEC_BATCH, ML_HEADS, ML_DK))
            s_m.append(jnp.swapaxes(ms[:, :, 0], 0, 1))
            x = _proj_ln(x, z, (w_out,), ln_g4, ln_b4, j, li)
        x = _ffn_ln(x, wg, wu, wd, ln_g4, ln_b4, li, 1, 2)

    y_prompt = jnp.swapaxes(x[:ROWS_PROMPT].reshape(SEQ, BATCH, D_MODEL), 0, 1)
    y_sample = x[ROWS_PROMPT:].reshape(DEC_BATCH, 1, D_MODEL)
    return (y_prompt, y_sample,
            jnp.stack(p_s5r), jnp.stack(p_s5i), jnp.stack(p_c), jnp.stack(p_n), jnp.stack(p_m),
            jnp.stack(s_s5r), jnp.stack(s_s5i), s_c, jnp.stack(s_n), jnp.stack(s_m))
```

```python
import functools

import jax
import jax.numpy as jnp
from jax import lax
from jax.experimental import pallas as pl
from jax.experimental.pallas import tpu as pltpu

F32 = jnp.float32
BF16 = jnp.bfloat16

D_MODEL = 2048
BATCH = 4
SEQ = 2048
DEPTH = 4
DEC_BATCH = 128
D_FF = 5504
S5_GROUPS = 128
S5_GROUP = 16
S5_STATE = 64
S5_LANES = S5_GROUPS * S5_STATE
ML_HEADS = 4
ML_DK = 256
ML_DV = 512
ML_CHUNK = 128
ML_QKVO = 2 * ML_HEADS * ML_DK + 2 * ML_HEADS * ML_DV
DN_ALPHA = (2.0 * DEPTH) ** 0.25
LN_EPS = 1e-5
K_SCALE = ML_DK ** -0.5

LANE = 128
ROWS_PROMPT = SEQ * BATCH
ROWS_ALL = ROWS_PROMPT + DEC_BATCH
TM = 640
TF = 512
TN = 512
TM_IN = 1040
TN_IN = 1024
S5_OCT = 8
S5_NOCT = S5_GROUPS // S5_OCT
S5_OCT_CH = S5_OCT * S5_GROUP
S5_OCT_ST = S5_OCT * S5_STATE
S5_STEPS = 256
S5_PAIR = 2
RL_STEPS = 1024
ML_DEC_BB = 16
VMEM_LIMIT = 56 * 1024 * 1024


def _dot(a, b):
    return jnp.dot(a, b, preferred_element_type=F32)


def _layer_norm_rows(y, g, b):
    mu = jnp.mean(y, axis=-1, keepdims=True)
    yc = y - mu
    var = jnp.mean(yc * yc, axis=-1, keepdims=True)
    return yc * lax.rsqrt(var + LN_EPS) * g + b


def _log_sigmoid(x):
    return jnp.minimum(x, 0.0) - jnp.log1p(jnp.exp(-jnp.abs(x)))


def _to_rows_kernel(xp_ref, xs_ref, o_ref, *, nt):
    r = pl.program_id(0)

    @pl.when(r < nt)
    def _():
        for b in range(BATCH):
            o_ref[pl.ds(b, RL_STEPS, stride=BATCH), :] = xp_ref[b]

    @pl.when(r == nt)
    def _():
        o_ref[0:DEC_BATCH, :] = xs_ref[...]


def _to_rows(x_prompt, x_sample):
    nt = SEQ // RL_STEPS
    ns = D_MODEL // LANE
    return pl.pallas_call(
        functools.partial(_to_rows_kernel, nt=nt),
        grid=(nt + 1, ns),
        in_specs=[
            pl.BlockSpec((BATCH, RL_STEPS, LANE),
                         lambda r, s: (0, jnp.minimum(r, nt - 1), jnp.where(r == nt, ns - 1, s))),
            pl.BlockSpec((DEC_BATCH, LANE), lambda r, s: (0, s)),
        ],
        out_specs=pl.BlockSpec((RL_STEPS * BATCH, LANE), lambda r, s: (r, s)),
        out_shape=jax.ShapeDtypeStruct((ROWS_ALL, D_MODEL), F32),
        compiler_params=pltpu.CompilerParams(
            dimension_semantics=("arbitrary", "arbitrary"), vmem_limit_bytes=VMEM_LIMIT),
        name="to_rows",
    )(x_prompt, x_sample)


def _from_rows_kernel(x_ref, yp_ref, ys_ref, *, nt):
    r = pl.program_id(0)

    @pl.when(r < nt)
    def _():
        for b in range(BATCH):
            yp_ref[b] = x_ref[pl.ds(b, RL_STEPS, stride=BATCH), :]

    @pl.when(r == nt)
    def _():
        ys_ref[...] = x_ref[0:DEC_BATCH, :]


def _from_rows(x):
    nt = SEQ // RL_STEPS
    ns = D_MODEL // LANE
    return pl.pallas_call(
        functools.partial(_from_rows_kernel, nt=nt),
        grid=(nt + 1, ns),
        in_specs=[pl.BlockSpec((RL_STEPS * BATCH, LANE), lambda r, s: (r, s))],
        out_specs=[
            pl.BlockSpec((BATCH, RL_STEPS, LANE),
                         lambda r, s: (0, jnp.minimum(r, nt - 1), jnp.where(r == nt, ns - 1, s))),
            pl.BlockSpec((DEC_BATCH, LANE), lambda r, s: (0, jnp.where(r == nt, s, 0))),
        ],
        out_shape=[jax.ShapeDtypeStruct((BATCH, SEQ, D_MODEL), F32),
                   jax.ShapeDtypeStruct((DEC_BATCH, D_MODEL), F32)],
        compiler_params=pltpu.CompilerParams(
            dimension_semantics=("arbitrary", "arbitrary"), vmem_limit_bytes=VMEM_LIMIT),
        name="from_rows",
    )(x)


def _ffn_kernel(x_ref, wg_ref, wu_ref, wd_ref, g_ref, b_ref, o_ref, xb_ref, *, nf):
    f = pl.program_id(1)

    @pl.when(f == 0)
    def _():
        xb_ref[...] = x_ref[...].astype(BF16)
        o_ref[...] = jnp.zeros_like(o_ref)

    def hidden_tile(width):
        xb = xb_ref[...]
        hg = _dot(xb, wg_ref[:, :width])
        hu = _dot(xb, wu_ref[:, :width])
        act = (hg * jax.nn.sigmoid(hg) * hu).astype(BF16)
        o_ref[...] += _dot(act, wd_ref[:width, :])

    @pl.when(f < nf - 1)
    def _():
        hidden_tile(TF)

    @pl.when(f == nf - 1)
    def _():
        hidden_tile(D_FF - (nf - 1) * TF)
        y = DN_ALPHA * x_ref[...] + 0.5 * o_ref[...]
        o_ref[...] = _layer_norm_rows(y, g_ref[...], b_ref[...])


def _ffn_ln(x, wg, wu, wd, ln_g, ln_b, li, k, kn):
    m = x.shape[0]
    nf = pl.cdiv(D_FF, TF)
    return pl.pallas_call(
        functools.partial(_ffn_kernel, nf=nf),
        grid=(m // TM, nf),
        in_specs=[
            pl.BlockSpec((TM, D_MODEL), lambda i, f: (i, 0)),
            pl.BlockSpec((None, None, D_MODEL, TF), lambda i, f: (li, k, 0, f)),
            pl.BlockSpec((None, None, D_MODEL, TF), lambda i, f: (li, k, 0, f)),
            pl.BlockSpec((None, None, TF, D_MODEL), lambda i, f: (li, k, f, 0)),
            pl.BlockSpec((None, None, 1, D_MODEL), lambda i, f: (li, kn, 0, 0)),
            pl.BlockSpec((None, None, 1, D_MODEL), lambda i, f: (li, kn, 0, 0)),
        ],
        out_specs=pl.BlockSpec((TM, D_MODEL), lambda i, f: (i, 0)),
        out_shape=jax.ShapeDtypeStruct((m, D_MODEL), F32),
        scratch_shapes=[pltpu.VMEM((TM, D_MODEL), BF16)],
        compiler_params=pltpu.CompilerParams(
            dimension_semantics=("parallel", "arbitrary"), vmem_limit_bytes=VMEM_LIMIT),
        name="ffn_ln",
    )(x, wg, wu, wd, ln_g, ln_b)


def _proj_ln_kernel(x_ref, z_ref, *refs, gated, nn):
    w_refs, (g_ref, b_ref, o_ref, mix_ref) = refs[:-4], refs[-4:]
    n = pl.program_id(1)
    z = z_ref[...]
    p = _dot(z, w_refs[0][...])
    if gated:
        p = p * jax.nn.sigmoid(_dot(z, w_refs[1][...]))
    mix_ref[n] = p

    @pl.when(n == nn - 1)
    def _():
        ys = [DN_ALPHA * x_ref[:, j * TN:(j + 1) * TN] + mix_ref[j] for j in range(nn)]
        mu = sum(jnp.sum(y, axis=-1, keepdims=True) for y in ys) * (1.0 / D_MODEL)
        var = sum(jnp.sum((y - mu) * (y - mu), axis=-1, keepdims=True) for y in ys) * (1.0 / D_MODEL)
        r = lax.rsqrt(var + LN_EPS)
        for j in range(nn):
            sl = slice(j * TN, (j + 1) * TN)
            o_ref[:, sl] = (ys[j] - mu) * r * g_ref[:, sl] + b_ref[:, sl]


def _proj_ln(x, z, ws, ln_g, ln_b, lj, li):
    m = x.shape[0]
    nn = D_MODEL // TN
    gated = len(ws) == 2
    row_spec = pl.BlockSpec((TM, D_MODEL), lambda i, n: (i, 0))
    w_spec = pl.BlockSpec((None, D_MODEL, TN), lambda i, n: (lj, 0, n))
    ln_spec = pl.BlockSpec((None, None, 1, D_MODEL), lambda i, n: (li, 1, 0, 0))
    return pl.pallas_call(
        functools.partial(_proj_ln_kernel, gated=gated, nn=nn),
        grid=(m // TM, nn),
        in_specs=[row_spec, row_spec] + [w_spec] * len(ws) + [ln_spec, ln_spec],
        out_specs=row_spec,
        out_shape=jax.ShapeDtypeStruct((m, D_MODEL), F32),
        scratch_shapes=[pltpu.VMEM((nn, TM, TN), F32)],
        compiler_params=pltpu.CompilerParams(
            dimension_semantics=("parallel", "arbitrary"), vmem_limit_bytes=VMEM_LIMIT),
        name="glu_ln" if gated else "out_ln",
    )(x, z, *ws, ln_g, ln_b)


def _in_proj_kernel(x_ref, w_ref, wg_ref, p_ref, gt_ref, xb_ref):
    n = pl.program_id(1)

    @pl.when(n == 0)
    def _():
        xb = x_ref[...].astype(BF16)
        xb_ref[...] = xb
        gt_ref[...] = _dot(xb, wg_ref[...])

    p = _dot(xb_ref[...], w_ref[...])
    for s in range(TN_IN // LANE):
        p_ref[s] = p[:, s * LANE:(s + 1) * LANE]


def _in_proj(x, w_in, w_gate, lj):
    m = x.shape[0]
    slabs = TN_IN // LANE
    return pl.pallas_call(
        _in_proj_kernel,
        grid=(m // TM_IN, ML_QKVO // TN_IN),
        in_specs=[
            pl.BlockSpec((TM_IN, D_MODEL), lambda i, n: (i, 0)),
            pl.BlockSpec((None, D_MODEL, TN_IN), lambda i, n: (lj, 0, n)),
            pl.BlockSpec((None, D_MODEL, LANE), lambda i, n: (lj, 0, 0)),
        ],
        out_specs=[
            pl.BlockSpec((slabs, TM_IN, LANE), lambda i, n: (n, i, 0)),
            pl.BlockSpec((TM_IN, LANE), lambda i, n: (i, 0)),
        ],
        out_shape=[jax.ShapeDtypeStruct((ML_QKVO // LANE, m, LANE), F32), jax.ShapeDtypeStruct((m, LANE), F32)],
        scratch_shapes=[pltpu.VMEM((TM_IN, D_MODEL), BF16)],
        compiler_params=pltpu.CompilerParams(
            dimension_semantics=("parallel", "arbitrary"), vmem_limit_bytes=VMEM_LIMIT),
        name="ml_in_proj",
    )(x, w_in, w_gate)


def _s5_disc_kernel(ar_ref, ai_ref, ldt_ref, br_ref, bi_ref, abr_ref, abi_ref, bbr_ref, bbi_ref):
    ar = ar_ref[...]
    ai = ai_ref[...]
    dt = jnp.exp(ldt_ref[...])
    mag = jnp.exp(dt * ar)
    abr = mag * jnp.cos(dt * ai)
    abi = mag * jnp.sin(dt * ai)
    zr = abr - 1.0
    zi = abi
    den = ar * ar + ai * ai
    gr = (zr * ar + zi * ai) / den
    gi = (zi * ar - zr * ai) / den
    br = br_ref[...]
    bi = bi_ref[...]
    abr_ref[...] = abr
    abi_ref[...] = abi
    bbr_ref[...] = gr * br - gi * bi
    bbi_ref[...] = gr * bi + gi * br


def _s5_discretise(a_re, a_im, log_dt, b_re, b_im):
    g, p, c = S5_GROUPS, S5_STATE, S5_GROUP
    outs = pl.pallas_call(
        _s5_disc_kernel,
        out_shape=[jax.ShapeDtypeStruct((g, 1, p), F32), jax.ShapeDtypeStruct((g, 1, p), F32),
                   jax.ShapeDtypeStruct((g, c, p), F32), jax.ShapeDtypeStruct((g, c, p), F32)],
        name="s5_discretise",
    )(a_re.reshape(g, 1, p), a_im.reshape(g, 1, p), log_dt.reshape(g, 1, 1),
      jnp.swapaxes(b_re, 1, 2), jnp.swapaxes(b_im, 1, 2))
    abr, abi, bbr, bbi = outs
    return abr.reshape(1, g * p), abi.reshape(1, g * p), bbr, bbi


def _block_diag_in(bb):
    eye = jnp.eye(S5_OCT, dtype=F32)
    w = bb.reshape(S5_NOCT, S5_OCT, S5_GROUP, 1, S5_STATE) * eye[None, :, None, :, None]
    return w.reshape(S5_NOCT, S5_OCT_CH, S5_OCT_ST).astype(BF16)


def _block_diag_out(c):
    eye = jnp.eye(S5_OCT, dtype=F32)
    cx = jnp.swapaxes(c.reshape(S5_NOCT, S5_OCT, S5_GROUP, S5_STATE), 2, 3)
    w = cx[:, :, :, None, :] * eye[None, :, None, :, None]
    return w.reshape(S5_NOCT, S5_OCT_ST, S5_OCT_CH).astype(BF16)


def _s5_seq_kernel(u_ref, h0_ref, a1_ref, a2_ref, wb_ref, wc_ref, d_ref, z_ref, hf_ref,
                   lhs_s, x8_s, zz_s, st_s, *, nsteps, nchunks):
    c = pl.program_id(1)
    rows = nsteps * BATCH
    ch, st = S5_OCT_CH, S5_OCT_ST
    top = lax.broadcasted_iota(jnp.int32, (8, LANE), 0) < BATCH

    @pl.when(c == 0)
    def _():
        st_s[...] = h0_ref[...]

    for o in range(S5_PAIR):
        def spread(j, carry):
            r_in = pl.multiple_of(j * 8, 8)
            r_out = pl.multiple_of(j * 16, 16)
            src = u_ref[pl.ds(r_in, 8), o * ch:(o + 1) * ch]
            rot = pltpu.roll(src, BATCH, 0)
            zero = jnp.zeros_like(src)
            lo, hi = 2 * o * ch, (2 * o + 1) * ch
            lhs_s[pl.ds(r_out, 8), lo:lo + ch] = jnp.where(top, src, zero)
            lhs_s[pl.ds(r_out, 8), hi:hi + ch] = jnp.where(top, zero, rot)
            lhs_s[pl.ds(r_out + 8, 8), lo:lo + ch] = jnp.where(top, rot, zero)
            lhs_s[pl.ds(r_out + 8, 8), hi:hi + ch] = jnp.where(top, zero, src)
            return carry

        lax.fori_loop(0, rows // 8, spread, 0, unroll=4)
        x8_s[:, o * st:(o + 1) * st] = _dot(lhs_s[:, 2 * o * ch:2 * (o + 1) * ch].astype(BF16), wb_ref[o])

    a1 = a1_ref[...]
    a2 = a2_ref[...]

    def step(t, h):
        r = pl.multiple_of(t * 8, 8)
        hn = (a1 * h + x8_s[pl.ds(r, 8), :]) + a2 * pltpu.roll(h, BATCH, 0)
        x8_s[pl.ds(r, 8), :] = hn
        return hn

    h_last = lax.fori_loop(0, nsteps, step, st_s[...], unroll=4)
    st_s[...] = h_last

    for o in range(S5_PAIR):
        zz_s[:, 2 * o * ch:2 * (o + 1) * ch] = _dot(x8_s[:, o * st:(o + 1) * st].astype(BF16), wc_ref[o])

    def emit(jj, carry):
        r_out = pl.multiple_of(jj * 16, 16)
        for o in range(S5_PAIR):
            lo, hi = 2 * o * ch, (2 * o + 1) * ch
            tiles = []
            for half in range(2):
                r_even = pl.multiple_of(jj * 32 + half * 16, 8)
                r_odd = pl.multiple_of(jj * 32 + half * 16 + 8, 8)
                y_even = zz_s[pl.ds(r_even, 8), lo:lo + ch] + pltpu.roll(zz_s[pl.ds(r_even, 8), hi:hi + ch], BATCH, 0)
                y_odd = zz_s[pl.ds(r_odd, 8), lo:lo + ch] + pltpu.roll(zz_s[pl.ds(r_odd, 8), hi:hi + ch], BATCH, 0)
                tiles.append(jnp.where(top, y_even, pltpu.roll(y_odd, BATCH, 0)))
            y = jnp.concatenate(tiles, axis=0)
            y = y + d_ref[:, o * ch:(o + 1) * ch] * u_ref[pl.ds(r_out, 16), o * ch:(o + 1) * ch]
            z_ref[pl.ds(r_out, 16), o * ch:(o + 1) * ch] = jax.nn.gelu(y).astype(BF16)
        return carry

    lax.fori_loop(0, rows // 16, emit, 0, unroll=2)

    @pl.when(c == nchunks - 1)
    def _():
        hf_ref[...] = h_last


def _s5_scan_prompt(x_all, h0, abr, abi, wbr, wbi, wcr, wci, d_row):
    rows = S5_STEPS * BATCH
    nchunks = SEQ // S5_STEPS
    pc, ps = S5_PAIR * S5_OCT_CH, S5_PAIR * S5_OCT_ST
    a1 = jnp.broadcast_to(abr, (8, S5_LANES))
    a2 = jnp.concatenate([jnp.broadcast_to(-abi, (BATCH, S5_LANES)), jnp.broadcast_to(abi, (BATCH, S5_LANES))])
    wb = jnp.concatenate([wbr, wbi], axis=1)
    wc = jnp.concatenate([wcr, -wci], axis=2)
    st_spec = pl.BlockSpec((8, ps), lambda p, c: (0, p))
    return pl.pallas_call(
        functools.partial(_s5_seq_kernel, nsteps=S5_STEPS, nchunks=nchunks),
        grid=(S5_NOCT // S5_PAIR, nchunks),
        in_specs=[
            pl.BlockSpec((rows, pc), lambda p, c: (c, p)),
            st_spec, st_spec, st_spec,
            pl.BlockSpec((S5_PAIR, 2 * S5_OCT_CH, S5_OCT_ST), lambda p, c: (p, 0, 0)),
            pl.BlockSpec((S5_PAIR, S5_OCT_ST, 2 * S5_OCT_CH), lambda p, c: (p, 0, 0)),
            pl.BlockSpec((1, pc), lambda p, c: (0, p)),
        ],
        out_specs=[pl.BlockSpec((rows, pc), lambda p, c: (c, p)), st_spec],
        out_shape=[jax.ShapeDtypeStruct((ROWS_ALL, D_MODEL), BF16),
                   jax.ShapeDtypeStruct((8, S5_LANES), F32)],
        scratch_shapes=[pltpu.VMEM((2 * rows, 2 * pc), F32), pltpu.VMEM((2 * rows, ps), F32),
                        pltpu.VMEM((2 * rows, 2 * pc), F32), pltpu.VMEM((8, ps), F32)],
        compiler_params=pltpu.CompilerParams(
            dimension_semantics=("parallel", "arbitrary"), vmem_limit_bytes=VMEM_LIMIT),
        name="s5_scan_prompt",
    )(x_all, h0, a1, a2, wb, wc, d_row)


def _s5_step_kernel(u_ref, h0r_ref, h0i_ref, ar_ref, ai_ref, wbr_ref, wbi_ref, wcr_ref, wci_ref, d_ref,
                    z_all_ref, z_ref, hr_ref, hi_ref):
    del z_all_ref
    u = u_ref[...]
    ub = u.astype(BF16)
    xr = _dot(ub, wbr_ref[...])
    xi = _dot(ub, wbi_ref[...])
    ar = ar_ref[...]
    ai = ai_ref[...]
    h0r = h0r_ref[...]
    h0i = h0i_ref[...]
    hr = xr + (ar * h0r - ai * h0i)
    hi = xi + (ar * h0i + ai * h0r)
    hr_ref[...] = hr
    hi_ref[...] = hi
    y = _dot(hr.astype(BF16), wcr_ref[...]) - _dot(hi.astype(BF16), wci_ref[...])
    y = y + d_ref[...] * u
    z_ref[...] = jax.nn.gelu(y).astype(BF16)


def _s5_step_sample(x_all, h0r, h0i, lj, abr, abi, wbr, wbi, wcr, wci, d_row, z_all):
    row_blk = ROWS_PROMPT // DEC_BATCH
    oct_w = pl.BlockSpec((None, S5_OCT_CH, S5_OCT_ST), lambda o: (o, 0, 0))
    oct_c = pl.BlockSpec((None, S5_OCT_ST, S5_OCT_CH), lambda o: (o, 0, 0))
    st_in = pl.BlockSpec((None, DEC_BATCH, S5_OCT_ST), lambda o: (lj, 0, o))
    st_out = pl.BlockSpec((DEC_BATCH, S5_OCT_ST), lambda o: (0, o))
    return pl.pallas_call(
        _s5_step_kernel,
        grid=(S5_NOCT,),
        in_specs=[
            pl.BlockSpec((DEC_BATCH, S5_OCT_CH), lambda o: (row_blk, o)),
            st_in, st_in,
            pl.BlockSpec((1, S5_OCT_ST), lambda o: (0, o)),
            pl.BlockSpec((1, S5_OCT_ST), lambda o: (0, o)),
            oct_w, oct_w, oct_c, oct_c,
            pl.BlockSpec((1, S5_OCT_CH), lambda o: (0, o)),
            pl.BlockSpec(memory_space=pl.ANY),
        ],
        out_specs=[pl.BlockSpec((DEC_BATCH, S5_OCT_CH), lambda o: (row_blk, o)), st_out, st_out],
        out_shape=[jax.ShapeDtypeStruct((ROWS_ALL, D_MODEL), BF16),
                   jax.ShapeDtypeStruct((DEC_BATCH, S5_LANES), F32),
                   jax.ShapeDtypeStruct((DEC_BATCH, S5_LANES), F32)],
        input_output_aliases={10: 0},
        compiler_params=pltpu.CompilerParams(
            dimension_semantics=("parallel",), vmem_limit_bytes=VMEM_LIMIT),
        name="s5_step_sample",
    )(x_all, h0r, h0i, abr, abi, wbr, wbi, wcr, wci, d_row, z_all)


def _mlstm_seq_kernel(q_ref, k_ref, v_ref, o_ref, gt_ref, bias_ref, ng_ref,
                      hn_ref, cf_ref, nf_ref, mf_ref, c_s, n_s, m_s, hn_s, *, nchunks):
    ci = pl.program_id(0)
    b = pl.program_id(1)
    L = ML_CHUNK
    rows = pl.ds(b, L, stride=BATCH)

    def seq_rows(ref, h, width):
        per_head = width // LANE
        return jnp.concatenate([ref[h * per_head + j, rows, :] for j in range(per_head)], axis=1)

    @pl.when(ci == 0)
    def _():
        c_s[b] = jnp.zeros(c_s.shape[1:], F32)
        n_s[b] = jnp.zeros(n_s.shape[1:], F32)
        m_s[b] = jnp.zeros(m_s.shape[1:], F32)

    gb = gt_ref[rows, :] + bias_ref[...]
    lfa = _log_sigmoid(gb)
    gbt = gb.T
    lft = lfa.T
    row = lax.broadcasted_iota(jnp.int32, (L, L), 0)
    col = lax.broadcasted_iota(jnp.int32, (L, L), 1)
    causal = col <= row
    tril = causal.astype(F32)
    triu = (row <= col).astype(F32)
    bc_cols = jnp.dot(tril, lfa, precision=lax.Precision.HIGHEST, preferred_element_type=F32)
    bc_rows = jnp.dot(lft[0:8, :], triu, precision=lax.Precision.HIGHEST, preferred_element_type=F32)

    for h in range(ML_HEADS):
        ig_row = gbt[h:h + 1, :]
        bc_row = bc_rows[ML_HEADS + h:ML_HEADS + h + 1, :]
        ig_col = gb[:, h:h + 1]
        bc_col = bc_cols[:, ML_HEADS + h:ML_HEADS + h + 1]
        m_prev = m_s[b, h:h + 1, 0:1]
        inter = bc_col + m_prev
        dmat = jnp.where(causal, bc_col - bc_row + ig_row, -jnp.inf)
        mt = jnp.maximum(inter, jnp.max(dmat, axis=1, keepdims=True))
        wts = jnp.exp(dmat - mt)

        qf = seq_rows(q_ref, h, ML_DK)
        kf = seq_rows(k_ref, h, ML_DK) * K_SCALE
        qb = qf.astype(BF16)
        vb = seq_rows(v_ref, h, ML_DV).astype(BF16)
        qk = lax.dot_general(qb, kf.astype(BF16), (((1,), (1,)), ((), ())), preferred_element_type=F32)
        sc = qk * wts
        scale = jnp.exp(inter - mt)
        cmat = c_s[b, h]
        n_row = n_s[b, h:h + 1, :]
        num = scale * _dot(qb, cmat.astype(BF16)) + _dot(sc.astype(BF16), vb)
        qn = jnp.sum(qb.astype(F32) * n_row, axis=1, keepdims=True)
        den = scale * qn + jnp.sum(sc, axis=1, keepdims=True)
        hh = num * (1.0 / jnp.maximum(jnp.abs(den), jnp.exp(-mt)))

        mu = jnp.mean(hh, axis=1, keepdims=True)
        hc = hh - mu
        var = jnp.mean(hc * hc, axis=1, keepdims=True)
        hn = hc * lax.rsqrt(var + LN_EPS) * ng_ref[:, h * ML_DV:(h + 1) * ML_DV]
        gated = jax.nn.sigmoid(seq_rows(o_ref, h, ML_DV)) * hn
        for j in range(ML_DV // LANE):
            hn_s[h * (ML_DV // LANE) + j, rows, :] = gated[:, j * LANE:(j + 1) * LANE]

        m_last = mt[L - 1:L, :]
        b_last = bc_col[L - 1:L, :]
        dec = jnp.exp(b_last - bc_col + ig_col - m_last)
        cscale = jnp.exp(b_last + m_prev - m_last)
        kd = kf * dec
        c_new = cscale * cmat + _dot(kd.T.astype(BF16), vb)
        n_new = cscale * n_row + jnp.sum(kd, axis=0, keepdims=True)
        c_s[b, h] = c_new
        n_s[b, h:h + 1, :] = n_new
        m_s[b, h:h + 1, :] = jnp.broadcast_to(m_last, (1, LANE))

    @pl.when(ci == nchunks - 1)
    def _():
        cf_ref[...] = c_s[b]
        nf_ref[...] = n_s[b, 0:ML_HEADS, :]
        mf_ref[...] = m_s[b]

    @pl.when(b == BATCH - 1)
    def _():
        for s in range(ML_HEADS * ML_DV // LANE):
            hn_ref[:, s * LANE:(s + 1) * LANE] = hn_s[s].astype(BF16)


def _mlstm_prompt(proj, gates, bias_row, norm_g, lj):
    nchunks = SEQ // ML_CHUNK
    rows = ML_CHUNK * BATCH
    qk_s = ML_HEADS * ML_DK // LANE
    v_s = ML_HEADS * ML_DV // LANE
    v_w = ML_HEADS * ML_DV
    last = nchunks - 1

    def state_idx(c, b):
        return jnp.where(c == last, b, 0)

    return pl.pallas_call(
        functools.partial(_mlstm_seq_kernel, nchunks=nchunks),
        grid=(nchunks, BATCH),
        in_specs=[
            pl.BlockSpec((qk_s, rows, LANE), lambda c, b: (0, c, 0)),
            pl.BlockSpec((qk_s, rows, LANE), lambda c, b: (1, c, 0)),
            pl.BlockSpec((v_s, rows, LANE), lambda c, b: (1, c, 0)),
            pl.BlockSpec((v_s, rows, LANE), lambda c, b: (2, c, 0)),
            pl.BlockSpec((rows, LANE), lambda c, b: (c, 0)),
            pl.BlockSpec((1, LANE), lambda c, b: (0, 0)),
            pl.BlockSpec((None, 1, v_w), lambda c, b: (lj, 0, 0)),
        ],
        out_specs=[
            pl.BlockSpec((rows, v_w), lambda c, b: (c, 0)),
            pl.BlockSpec((None, ML_HEADS, ML_DK, ML_DV), lambda c, b: (state_idx(c, b), 0, 0, 0)),
            pl.BlockSpec((None, ML_HEADS, ML_DK), lambda c, b: (state_idx(c, b), 0, 0)),
            pl.BlockSpec((None, 8, LANE), lambda c, b: (state_idx(c, b), 0, 0)),
        ],
        out_shape=[
            jax.ShapeDtypeStruct((ROWS_ALL, v_w), BF16),
            jax.ShapeDtypeStruct((BATCH, ML_HEADS, ML_DK, ML_DV), F32),
            jax.ShapeDtypeStruct((BATCH, ML_HEADS, ML_DK), F32),
            jax.ShapeDtypeStruct((BATCH, 8, LANE), F32),
        ],
        scratch_shapes=[pltpu.VMEM((BATCH, ML_HEADS, ML_DK, ML_DV), F32), pltpu.VMEM((BATCH, 8, ML_DK), F32),
                        pltpu.VMEM((BATCH, 8, LANE), F32), pltpu.VMEM((v_s, rows, LANE), F32)],
        compiler_params=pltpu.CompilerParams(
            dimension_semantics=("arbitrary", "arbitrary"), vmem_limit_bytes=VMEM_LIMIT),
        name="mlstm_prompt",
    )(proj, proj, proj, proj, gates, bias_row, norm_g)


def _mlstm_step_kernel(q_ref, k_ref, v_ref, o_ref, ka_ref, va_ref, gt_ref, bias_ref, ng_ref,
                       c_ref, n_ref, m_ref, hn_ref, co_ref, no_ref, mo_ref, kt_s, num_s):
    h = pl.program_id(0)
    i = pl.program_id(1)
    bb = ML_DEC_BB

    def cols(ref):
        return jnp.concatenate([ref[s] for s in range(ref.shape[0])], axis=1)

    @pl.when(i == 0)
    def _():
        kt_s[...] = (cols(ka_ref) * K_SCALE).T

    gb = gt_ref[...] + bias_ref[...]
    lane = lax.broadcasted_iota(jnp.int32, (bb, LANE), 1)
    ig = jnp.sum(jnp.where(lane == h, gb, 0.0), axis=1, keepdims=True)
    fp = jnp.sum(jnp.where(lane == h + ML_HEADS, gb, 0.0), axis=1, keepdims=True)
    lf = _log_sigmoid(fp)
    m4 = m_ref[...]
    hl = lax.broadcasted_iota(jnp.int32, (bb, ML_HEADS), 1)
    m_prev = jnp.sum(jnp.where(hl == h, m4, 0.0), axis=1, keepdims=True)
    inter = lf + m_prev
    mt = jnp.maximum(inter, ig)
    wts = jnp.exp(ig - mt)
    scale = jnp.exp(inter - mt)

    qb = cols(q_ref).astype(BF16)
    qf = qb.astype(F32)
    kf = cols(k_ref) * K_SCALE
    vf = cols(v_ref).astype(BF16).astype(F32)
    n_prev = n_ref[...]
    qk = jnp.sum(qf * kf.astype(BF16).astype(F32), axis=1, keepdims=True)
    sc = qk * wts
    qn = jnp.sum(qf * n_prev, axis=1, keepdims=True)
    den = scale * qn + sc
    rden = 1.0 / jnp.maximum(jnp.abs(den), jnp.exp(-mt))

    ktb = kt_s[...].astype(BF16)
    va = cols(va_ref)
    rowi = lax.broadcasted_iota(jnp.int32, (DEC_BATCH, ML_DV), 0)
    for j in range(bb):
        cj = c_ref[j]
        qc = _dot(qb, cj.astype(BF16))[j:j + 1, :]
        vsel = jnp.where(rowi == i * bb + j, va, 0.0).astype(BF16)
        outer = _dot(ktb, vsel)
        co_ref[j] = scale[j:j + 1, :] * cj + wts[j:j + 1, :] * outer
        num_s[j:j + 1, :] = scale[j:j + 1, :] * qc + sc[j:j + 1, :] * vf[j:j + 1, :]

    hh = num_s[...] * rden
    mu = jnp.mean(hh, axis=1, keepdims=True)
    hc = hh - mu
    var = jnp.mean(hc * hc, axis=1, keepdims=True)
    hn = hc * lax.rsqrt(var + LN_EPS) * ng_ref[...]
    hn_ref[...] = (jax.nn.sigmoid(cols(o_ref)) * hn).astype(BF16)
    no_ref[...] = scale * n_prev + wts * kf
    mo_ref[...] = jnp.broadcast_to(mt, (bb, LANE))


def _mlstm_sample(proj, gates, bias_row, norm_g, c_in, n_in, m_in, hn_all, c_acc, lj):
    bb = ML_DEC_BB
    rb = ROWS_PROMPT // bb
    r128 = ROWS_PROMPT // DEC_BATCH
    nh = ML_HEADS
    ks = ML_DK // LANE
    vs = ML_DV // LANE
    in_specs = [
        pl.BlockSpec((ks, bb, LANE), lambda h, i: (h, rb + i, 0)),
        pl.BlockSpec((ks, bb, LANE), lambda h, i: (nh + h, rb + i, 0)),
        pl.BlockSpec((vs, bb, LANE), lambda h, i: (nh + h, rb + i, 0)),
        pl.BlockSpec((vs, bb, LANE), lambda h, i: (2 * nh + h, rb + i, 0)),
        pl.BlockSpec((ks, DEC_BATCH, LANE), lambda h, i: (nh + h, r128, 0)),
        pl.BlockSpec((vs, DEC_BATCH, LANE), lambda h, i: (nh + h, r128, 0)),
        pl.BlockSpec((bb, LANE), lambda h, i: (rb + i, 0)),
        pl.BlockSpec((1, LANE), lambda h, i: (0, 0)),
        pl.BlockSpec((None, 1, ML_DV), lambda h, i: (lj, 0, h)),
        pl.BlockSpec((None, bb, None, ML_DK, ML_DV), lambda h, i: (lj, i, h, 0, 0)),
        pl.BlockSpec((None, bb, ML_DK), lambda h, i: (lj, i, h)),
        pl.BlockSpec((None, bb, nh), lambda h, i: (lj, i, 0)),
        pl.BlockSpec(memory_space=pl.ANY),
    ]
    args = [proj, proj, proj, proj, proj, proj, gates, bias_row, norm_g, c_in, n_in, m_in, hn_all]
    n_blocked = 12
    aliases = {n_blocked: 0}
    if c_acc is not None:
        in_specs.append(pl.BlockSpec(memory_space=pl.ANY))
        args.append(c_acc)
        aliases[n_blocked + 1] = 1

    def body(*refs):
        _mlstm_step_kernel(*refs[:n_blocked], *refs[len(args):])

    return pl.pallas_call(
        body,
        grid=(nh, DEC_BATCH // bb),
        in_specs=in_specs,
        out_specs=[
            pl.BlockSpec((bb, ML_DV), lambda h, i: (rb + i, h)),
            pl.BlockSpec((None, bb, None, ML_DK, ML_DV), lambda h, i: (lj, i, h, 0, 0)),
            pl.BlockSpec((bb, ML_DK), lambda h, i: (i, h)),
            pl.BlockSpec((None, bb, LANE), lambda h, i: (h, i, 0)),
        ],
        out_shape=[
            jax.ShapeDtypeStruct((ROWS_ALL, nh * ML_DV), BF16),
            jax.ShapeDtypeStruct(c_in.shape, F32),
            jax.ShapeDtypeStruct((DEC_BATCH, nh * ML_DK), F32),
            jax.ShapeDtypeStruct((nh, DEC_BATCH, LANE), F32),
        ],
        scratch_shapes=[pltpu.VMEM((ML_DK, DEC_BATCH), F32), pltpu.VMEM((bb, ML_DV), F32)],
        input_output_aliases=aliases,
        compiler_params=pltpu.CompilerParams(
            dimension_semantics=("arbitrary", "arbitrary"), vmem_limit_bytes=VMEM_LIMIT),
        name="mlstm_step_sample",
    )(*args)


def kernel(x_prompt, x_sample, state_s5_re, state_s5_im, state_mlstm_C, state_mlstm_n, state_mlstm_m,
           ln_g, ln_b, ffn_w_gate, ffn_w_up, ffn_w_down,
           s5_a_re, s5_a_im, s5_log_dt, s5_b_re, s5_b_im, s5_c_re, s5_c_im, s5_d, s5_w_a, s5_w_b,
           ml_w_in, ml_b_i, ml_b_f, ml_norm_g, ml_w_out):
    n_s5 = s5_a_re.shape[0]
    n_ml = ml_w_in.shape[0]

    wg = ffn_w_gate.astype(BF16)
    wu = ffn_w_up.astype(BF16)
    wd = ffn_w_down.astype(BF16)
    glu_a = s5_w_a.astype(BF16)
    glu_b = s5_w_b.astype(BF16)
    w_in = ml_w_in.astype(BF16)
    w_in_gate = jnp.pad(ml_w_in[:, :, ML_QKVO:].astype(BF16), ((0, 0), (0, 0), (0, LANE - 2 * ML_HEADS)))
    w_out = ml_w_out.astype(BF16)
    ln_g4 = ln_g.reshape(DEPTH, 3, 1, D_MODEL)
    ln_b4 = ln_b.reshape(DEPTH, 3, 1, D_MODEL)
    ml_bias = jnp.pad(jnp.concatenate([ml_b_i, ml_b_f], axis=1), ((0, 0), (0, LANE - 2 * ML_HEADS)))
    ml_ng = ml_norm_g.reshape(n_ml, 1, ML_HEADS * ML_DV)
    st_r = state_s5_re.reshape(n_s5, DEC_BATCH, S5_LANES)
    st_i = state_s5_im.reshape(n_s5, DEC_BATCH, S5_LANES)
    st_n = state_mlstm_n.reshape(n_ml, DEC_BATCH, ML_HEADS * ML_DK)

    x = _to_rows(x_prompt, x_sample.reshape(DEC_BATCH, D_MODEL))

    p_s5r, p_s5i, s_s5r, s_s5i = [], [], [], []
    p_c, p_n, p_m, s_n, s_m = [], [], [], [], []
    s_c = None
    zero_state = jnp.zeros((8, S5_LANES), F32)
    for li in range(DEPTH):
        x = _ffn_ln(x, wg, wu, wd, ln_g4, ln_b4, li, 0, 0)
        j = li // 2
        if li % 2 == 0:
            abr, abi, bbr, bbi = _s5_discretise(s5_a_re[j], s5_a_im[j], s5_log_dt[j], s5_b_re[j], s5_b_im[j])
            wbr, wbi = _block_diag_in(bbr), _block_diag_in(bbi)
            wcr, wci = _block_diag_out(s5_c_re[j]), _block_diag_out(s5_c_im[j])
            d_row = s5_d[j].reshape(1, D_MODEL)
            z, hp = _s5_scan_prompt(x, zero_state, abr, abi, wbr, wbi, wcr, wci, d_row)
            z, hsr, hsi = _s5_step_sample(x, st_r, st_i, j, abr, abi, wbr, wbi, wcr, wci, d_row, z)
            p_s5r.append(hp[:BATCH].reshape(BATCH, S5_GROUPS, S5_STATE))
            p_s5i.append(hp[BATCH:].reshape(BATCH, S5_GROUPS, S5_STATE))
            s_s5r.append(hsr.reshape(DEC_BATCH, S5_GROUPS, S5_STATE))
            s_s5i.append(hsi.reshape(DEC_BATCH, S5_GROUPS, S5_STATE))
            x = _proj_ln(x, z, (glu_a, glu_b), ln_g4, ln_b4, j, li)
        else:
            proj, gates = _in_proj(x, w_in, w_in_gate, j)
            bias_row = ml_bias[j].reshape(1, LANE)
            z, cp, npr, mp = _mlstm_prompt(proj, gates, bias_row, ml_ng, j)
            z, s_c, ns, ms = _mlstm_sample(proj, gates, bias_row, ml_ng, state_mlstm_C, st_n,
                                           state_mlstm_m, z, s_c, j)
            p_c.append(cp)
            p_n.append(npr)
            p_m.append(mp[:, :ML_HEADS, 0])
            s_n.append(ns.reshape(DEC_BATCH, ML_HEADS, ML_DK))
            s_m.append(jnp.swapaxes(ms[:, :, 0], 0, 1))
            x = _proj_ln(x, z, (w_out,), ln_g4, ln_b4, j, li)
        x = _ffn_ln(x, wg, wu, wd, ln_g4, ln_b4, li, 1, 2)

    y_prompt, y_sample = _from_rows(x)
    y_sample = y_sample.reshape(DEC_BATCH, 1, D_MODEL)
    return (y_prompt, y_sample,
            jnp.stack(p_s5r), jnp.stack(p_s5i), jnp.stack(p_c), jnp.stack(p_n), jnp.stack(p_m),
            jnp.stack(s_s5r), jnp.stack(s_s5i), s_c, jnp.stack(s_n), jnp.stack(s_m))
```

```python
import functools

import jax
import jax.numpy as jnp
from jax import lax
from jax.experimental import pallas as pl
from jax.experimental.pallas import tpu as pltpu

F32 = jnp.float32
BF16 = jnp.bfloat16

D_MODEL = 2048
BATCH = 4
SEQ = 2048
DEPTH = 4
DEC_BATCH = 128
D_FF = 5504
S5_GROUPS = 128
S5_GROUP = 16
S5_STATE = 64
S5_LANES = S5_GROUPS * S5_STATE
ML_HEADS = 4
ML_DK = 256
ML_DV = 512
ML_CHUNK = 128
ML_QKVO = 2 * ML_HEADS * ML_DK + 2 * ML_HEADS * ML_DV
DN_ALPHA = (2.0 * DEPTH) ** 0.25
LN_EPS = 1e-5
K_SCALE = ML_DK ** -0.5

LANE = 128
ROWS_PROMPT = SEQ * BATCH
ROWS_ALL = ROWS_PROMPT + DEC_BATCH
TM = 640
TM_FF = 1040
TF = 256
TN = 512
TM_IN = 1040
TN_IN = 1024
S5_OCT = 8
S5_NOCT = S5_GROUPS // S5_OCT
S5_OCT_CH = S5_OCT * S5_GROUP
S5_OCT_ST = S5_OCT * S5_STATE
S5_STEPS = 256
S5_PAIR = 2
RL_STEPS = 1024
ML_DEC_BB = 16
VMEM_LIMIT = 56 * 1024 * 1024


def _dot(a, b):
    return jnp.dot(a, b, preferred_element_type=F32)


def _layer_norm_rows(y, g, b):
    mu = jnp.mean(y, axis=-1, keepdims=True)
    yc = y - mu
    var = jnp.mean(yc * yc, axis=-1, keepdims=True)
    return yc * lax.rsqrt(var + LN_EPS) * g + b


def _log_sigmoid(x):
    return jnp.minimum(x, 0.0) - jnp.log1p(jnp.exp(-jnp.abs(x)))


def _to_rows_kernel(xp_ref, xs_ref, o_ref, *, nt):
    r = pl.program_id(0)

    @pl.when(r < nt)
    def _():
        for b in range(BATCH):
            o_ref[pl.ds(b, RL_STEPS, stride=BATCH), :] = xp_ref[b]

    @pl.when(r == nt)
    def _():
        o_ref[0:DEC_BATCH, :] = xs_ref[...]


def _to_rows(x_prompt, x_sample):
    nt = SEQ // RL_STEPS
    ns = D_MODEL // LANE
    return pl.pallas_call(
        functools.partial(_to_rows_kernel, nt=nt),
        grid=(nt + 1, ns),
        in_specs=[
            pl.BlockSpec((BATCH, RL_STEPS, LANE),
                         lambda r, s: (0, jnp.minimum(r, nt - 1), jnp.where(r == nt, ns - 1, s))),
            pl.BlockSpec((DEC_BATCH, LANE), lambda r, s: (0, s)),
        ],
        out_specs=pl.BlockSpec((RL_STEPS * BATCH, LANE), lambda r, s: (r, s)),
        out_shape=jax.ShapeDtypeStruct((ROWS_ALL, D_MODEL), F32),
        compiler_params=pltpu.CompilerParams(
            dimension_semantics=("arbitrary", "arbitrary"), vmem_limit_bytes=VMEM_LIMIT),
        name="to_rows",
    )(x_prompt, x_sample)


def _from_rows_kernel(x_ref, yp_ref, ys_ref, *, nt):
    r = pl.program_id(0)

    @pl.when(r < nt)
    def _():
        for b in range(BATCH):
            yp_ref[b] = x_ref[pl.ds(b, RL_STEPS, stride=BATCH), :]

    @pl.when(r == nt)
    def _():
        ys_ref[...] = x_ref[0:DEC_BATCH, :]


def _from_rows(x):
    nt = SEQ // RL_STEPS
    ns = D_MODEL // LANE
    return pl.pallas_call(
        functools.partial(_from_rows_kernel, nt=nt),
        grid=(nt + 1, ns),
        in_specs=[pl.BlockSpec((RL_STEPS * BATCH, LANE), lambda r, s: (r, s))],
        out_specs=[
            pl.BlockSpec((BATCH, RL_STEPS, LANE),
                         lambda r, s: (0, jnp.minimum(r, nt - 1), jnp.where(r == nt, ns - 1, s))),
            pl.BlockSpec((DEC_BATCH, LANE), lambda r, s: (0, jnp.where(r == nt, s, 0))),
        ],
        out_shape=[jax.ShapeDtypeStruct((BATCH, SEQ, D_MODEL), F32),
                   jax.ShapeDtypeStruct((DEC_BATCH, D_MODEL), F32)],
        compiler_params=pltpu.CompilerParams(
            dimension_semantics=("arbitrary", "arbitrary"), vmem_limit_bytes=VMEM_LIMIT),
        name="from_rows",
    )(x)


def _ffn_kernel(x_ref, wg_ref, wu_ref, wd_ref, g_ref, b_ref, o_ref, xb_ref, *, nf):
    f = pl.program_id(1)

    @pl.when(f == 0)
    def _():
        x = x_ref[...]
        xb_ref[...] = x.astype(BF16)
        o_ref[...] = (2.0 * DN_ALPHA) * x

    def hidden_tile(width):
        xb = xb_ref[...]
        hg = _dot(xb, wg_ref[:, :width].astype(BF16))
        hu = _dot(xb, wu_ref[:, :width].astype(BF16))
        act = (hg * jax.nn.sigmoid(hg) * hu).astype(BF16)
        o_ref[...] += _dot(act, wd_ref[:width, :].astype(BF16))

    @pl.when(f < nf - 1)
    def _():
        hidden_tile(TF)

    @pl.when(f == nf - 1)
    def _():
        hidden_tile(D_FF - (nf - 1) * TF)
        o_ref[...] = _layer_norm_rows(0.5 * o_ref[...], g_ref[...], b_ref[...])


def _ffn_ln(x, wg, wu, wd, ln_g, ln_b, li, k, kn):
    m = x.shape[0]
    nf = pl.cdiv(D_FF, TF)
    return pl.pallas_call(
        functools.partial(_ffn_kernel, nf=nf),
        grid=(m // TM_FF, nf),
        in_specs=[
            pl.BlockSpec((TM_FF, D_MODEL), lambda i, f: (i, 0), pipeline_mode=pl.Buffered(1)),
            pl.BlockSpec((None, None, D_MODEL, TF), lambda i, f: (li, k, 0, f)),
            pl.BlockSpec((None, None, D_MODEL, TF), lambda i, f: (li, k, 0, f)),
            pl.BlockSpec((None, None, TF, D_MODEL), lambda i, f: (li, k, f, 0)),
            pl.BlockSpec((None, None, 1, D_MODEL), lambda i, f: (li, kn, 0, 0)),
            pl.BlockSpec((None, None, 1, D_MODEL), lambda i, f: (li, kn, 0, 0)),
        ],
        out_specs=pl.BlockSpec((TM_FF, D_MODEL), lambda i, f: (i, 0)),
        out_shape=jax.ShapeDtypeStruct((m, D_MODEL), F32),
        scratch_shapes=[pltpu.VMEM((TM_FF, D_MODEL), BF16)],
        compiler_params=pltpu.CompilerParams(
            dimension_semantics=("parallel", "arbitrary"), vmem_limit_bytes=VMEM_LIMIT),
        name="ffn_ln",
    )(x, wg, wu, wd, ln_g, ln_b)


def _proj_ln_kernel(x_ref, z_ref, *refs, gated, nn):
    w_refs, (g_ref, b_ref, o_ref, mix_ref) = refs[:-4], refs[-4:]
    n = pl.program_id(1)
    z = z_ref[...]
    p = _dot(z, w_refs[0][...])
    if gated:
        p = p * jax.nn.sigmoid(_dot(z, w_refs[1][...]))
    mix_ref[n] = p

    @pl.when(n == nn - 1)
    def _():
        ys = [DN_ALPHA * x_ref[:, j * TN:(j + 1) * TN] + mix_ref[j] for j in range(nn)]
        mu = sum(jnp.sum(y, axis=-1, keepdims=True) for y in ys) * (1.0 / D_MODEL)
        var = sum(jnp.sum((y - mu) * (y - mu), axis=-1, keepdims=True) for y in ys) * (1.0 / D_MODEL)
        r = lax.rsqrt(var + LN_EPS)
        for j in range(nn):
            sl = slice(j * TN, (j + 1) * TN)
            o_ref[:, sl] = (ys[j] - mu) * r * g_ref[:, sl] + b_ref[:, sl]


def _proj_ln(x, z, ws, ln_g, ln_b, lj, li):
    m = x.shape[0]
    nn = D_MODEL // TN
    gated = len(ws) == 2
    row_spec = pl.BlockSpec((TM, D_MODEL), lambda i, n: (i, 0))
    w_spec = pl.BlockSpec((None, D_MODEL, TN), lambda i, n: (lj, 0, n))
    ln_spec = pl.BlockSpec((None, None, 1, D_MODEL), lambda i, n: (li, 1, 0, 0))
    return pl.pallas_call(
        functools.partial(_proj_ln_kernel, gated=gated, nn=nn),
        grid=(m // TM, nn),
        in_specs=[row_spec, row_spec] + [w_spec] * len(ws) + [ln_spec, ln_spec],
        out_specs=row_spec,
        out_shape=jax.ShapeDtypeStruct((m, D_MODEL), F32),
        scratch_shapes=[pltpu.VMEM((nn, TM, TN), F32)],
        compiler_params=pltpu.CompilerParams(
            dimension_semantics=("parallel", "arbitrary"), vmem_limit_bytes=VMEM_LIMIT),
        name="glu_ln" if gated else "out_ln",
    )(x, z, *ws, ln_g, ln_b)


def _in_proj_kernel(x_ref, w_ref, wg_ref, p_ref, gt_ref, xb_ref):
    n = pl.program_id(1)

    @pl.when(n == 0)
    def _():
        xb = x_ref[...].astype(BF16)
        xb_ref[...] = xb
        gt_ref[...] = _dot(xb, wg_ref[...])

    p = _dot(xb_ref[...], w_ref[...])
    for s in range(TN_IN // LANE):
        p_ref[s] = p[:, s * LANE:(s + 1) * LANE]


def _in_proj(x, w_in, w_gate, lj):
    m = x.shape[0]
    slabs = TN_IN // LANE
    return pl.pallas_call(
        _in_proj_kernel,
        grid=(m // TM_IN, ML_QKVO // TN_IN),
        in_specs=[
            pl.BlockSpec((TM_IN, D_MODEL), lambda i, n: (i, 0)),
            pl.BlockSpec((None, D_MODEL, TN_IN), lambda i, n: (lj, 0, n)),
            pl.BlockSpec((None, D_MODEL, LANE), lambda i, n: (lj, 0, 0)),
        ],
        out_specs=[
            pl.BlockSpec((slabs, TM_IN, LANE), lambda i, n: (n, i, 0)),
            pl.BlockSpec((TM_IN, LANE), lambda i, n: (i, 0)),
        ],
        out_shape=[jax.ShapeDtypeStruct((ML_QKVO // LANE, m, LANE), F32), jax.ShapeDtypeStruct((m, LANE), F32)],
        scratch_shapes=[pltpu.VMEM((TM_IN, D_MODEL), BF16)],
        compiler_params=pltpu.CompilerParams(
            dimension_semantics=("parallel", "arbitrary"), vmem_limit_bytes=VMEM_LIMIT),
        name="ml_in_proj",
    )(x, w_in, w_gate)


def _s5_disc_kernel(ar_ref, ai_ref, ldt_ref, br_ref, bi_ref, abr_ref, abi_ref, bbr_ref, bbi_ref):
    ar = ar_ref[...]
    ai = ai_ref[...]
    dt = jnp.exp(ldt_ref[...])
    mag = jnp.exp(dt * ar)
    abr = mag * jnp.cos(dt * ai)
    abi = mag * jnp.sin(dt * ai)
    zr = abr - 1.0
    zi = abi
    den = ar * ar + ai * ai
    gr = (zr * ar + zi * ai) / den
    gi = (zi * ar - zr * ai) / den
    br = br_ref[...]
    bi = bi_ref[...]
    abr_ref[...] = abr
    abi_ref[...] = abi
    bbr_ref[...] = gr * br - gi * bi
    bbi_ref[...] = gr * bi + gi * br


def _s5_discretise(a_re, a_im, log_dt, b_re, b_im):
    g, p, c = S5_GROUPS, S5_STATE, S5_GROUP
    outs = pl.pallas_call(
        _s5_disc_kernel,
        out_shape=[jax.ShapeDtypeStruct((g, 1, p), F32), jax.ShapeDtypeStruct((g, 1, p), F32),
                   jax.ShapeDtypeStruct((g, c, p), F32), jax.ShapeDtypeStruct((g, c, p), F32)],
        name="s5_discretise",
    )(a_re.reshape(g, 1, p), a_im.reshape(g, 1, p), log_dt.reshape(g, 1, 1),
      jnp.swapaxes(b_re, 1, 2), jnp.swapaxes(b_im, 1, 2))
    abr, abi, bbr, bbi = outs
    return abr.reshape(1, g * p), abi.reshape(1, g * p), bbr, bbi


def _block_diag_in(bb):
    eye = jnp.eye(S5_OCT, dtype=F32)
    w = bb.reshape(S5_NOCT, S5_OCT, S5_GROUP, 1, S5_STATE) * eye[None, :, None, :, None]
    return w.reshape(S5_NOCT, S5_OCT_CH, S5_OCT_ST).astype(BF16)


def _block_diag_out(c):
    eye = jnp.eye(S5_OCT, dtype=F32)
    cx = jnp.swapaxes(c.reshape(S5_NOCT, S5_OCT, S5_GROUP, S5_STATE), 2, 3)
    w = cx[:, :, :, None, :] * eye[None, :, None, :, None]
    return w.reshape(S5_NOCT, S5_OCT_ST, S5_OCT_CH).astype(BF16)


def _s5_seq_kernel(u_ref, h0_ref, a1_ref, a2_ref, wb_ref, wc_ref, d_ref, z_ref, hf_ref,
                   lhs_s, x8_s, zz_s, st_s, *, nsteps, nchunks):
    c = pl.program_id(1)
    rows = nsteps * BATCH
    ch, st = S5_OCT_CH, S5_OCT_ST
    top = lax.broadcasted_iota(jnp.int32, (8, LANE), 0) < BATCH

    @pl.when(c == 0)
    def _():
        st_s[...] = h0_ref[...]

    for o in range(S5_PAIR):
        def spread(j, carry):
            r_in = pl.multiple_of(j * 8, 8)
            r_out = pl.multiple_of(j * 16, 16)
            src = u_ref[pl.ds(r_in, 8), o * ch:(o + 1) * ch]
            rot = pltpu.roll(src, BATCH, 0)
            zero = jnp.zeros_like(src)
            lo, hi = 2 * o * ch, (2 * o + 1) * ch
            lhs_s[pl.ds(r_out, 8), lo:lo + ch] = jnp.where(top, src, zero)
            lhs_s[pl.ds(r_out, 8), hi:hi + ch] = jnp.where(top, zero, rot)
            lhs_s[pl.ds(r_out + 8, 8), lo:lo + ch] = jnp.where(top, rot, zero)
            lhs_s[pl.ds(r_out + 8, 8), hi:hi + ch] = jnp.where(top, zero, src)
            return carry

        lax.fori_loop(0, rows // 8, spread, 0, unroll=4)
        x8_s[:, o * st:(o + 1) * st] = _dot(lhs_s[:, 2 * o * ch:2 * (o + 1) * ch].astype(BF16), wb_ref[o])

    a1 = a1_ref[...]
    a2 = a2_ref[...]

    def step(t, h):
        r = pl.multiple_of(t * 8, 8)
        hn = (a1 * h + x8_s[pl.ds(r, 8), :]) + a2 * pltpu.roll(h, BATCH, 0)
        x8_s[pl.ds(r, 8), :] = hn
        return hn

    h_last = lax.fori_loop(0, nsteps, step, st_s[...], unroll=4)
    st_s[...] = h_last

    for o in range(S5_PAIR):
        zz_s[:, 2 * o * ch:2 * (o + 1) * ch] = _dot(x8_s[:, o * st:(o + 1) * st].astype(BF16), wc_ref[o])

    def emit(jj, carry):
        r_out = pl.multiple_of(jj * 16, 16)
        for o in range(S5_PAIR):
            lo, hi = 2 * o * ch, (2 * o + 1) * ch
            tiles = []
            for half in range(2):
                r_even = pl.multiple_of(jj * 32 + half * 16, 8)
                r_odd = pl.multiple_of(jj * 32 + half * 16 + 8, 8)
                y_even = zz_s[pl.ds(r_even, 8), lo:lo + ch] + pltpu.roll(zz_s[pl.ds(r_even, 8), hi:hi + ch], BATCH, 0)
                y_odd = zz_s[pl.ds(r_odd, 8), lo:lo + ch] + pltpu.roll(zz_s[pl.ds(r_odd, 8), hi:hi + ch], BATCH, 0)
                tiles.append(jnp.where(top, y_even, pltpu.roll(y_odd, BATCH, 0)))
            y = jnp.concatenate(tiles, axis=0)
            y = y + d_ref[:, o * ch:(o + 1) * ch] * u_ref[pl.ds(r_out, 16), o * ch:(o + 1) * ch]
            z_ref[pl.ds(r_out, 16), o * ch:(o + 1) * ch] = jax.nn.gelu(y).astype(BF16)
        return carry

    lax.fori_loop(0, rows // 16, emit, 0, unroll=2)

    @pl.when(c == nchunks - 1)
    def _():
        hf_ref[...] = h_last


def _s5_scan_prompt(x_all, h0, abr, abi, wbr, wbi, wcr, wci, d_row):
    rows = S5_STEPS * BATCH
    nchunks = SEQ // S5_STEPS
    pc, ps = S5_PAIR * S5_OCT_CH, S5_PAIR * S5_OCT_ST
    a1 = jnp.broadcast_to(abr, (8, S5_LANES))
    a2 = jnp.concatenate([jnp.broadcast_to(-abi, (BATCH, S5_LANES)), jnp.broadcast_to(abi, (BATCH, S5_LANES))])
    wb = jnp.concatenate([wbr, wbi], axis=1)
    wc = jnp.concatenate([wcr, -wci], axis=2)
    st_spec = pl.BlockSpec((8, ps), lambda p, c: (0, p))
    return pl.pallas_call(
        functools.partial(_s5_seq_kernel, nsteps=S5_STEPS, nchunks=nchunks),
        grid=(S5_NOCT // S5_PAIR, nchunks),
        in_specs=[
            pl.BlockSpec((rows, pc), lambda p, c: (c, p)),
            st_spec, st_spec, st_spec,
            pl.BlockSpec((S5_PAIR, 2 * S5_OCT_CH, S5_OCT_ST), lambda p, c: (p, 0, 0)),
            pl.BlockSpec((S5_PAIR, S5_OCT_ST, 2 * S5_OCT_CH), lambda p, c: (p, 0, 0)),
            pl.BlockSpec((1, pc), lambda p, c: (0, p)),
        ],
        out_specs=[pl.BlockSpec((rows, pc), lambda p, c: (c, p)), st_spec],
        out_shape=[jax.ShapeDtypeStruct((ROWS_ALL, D_MODEL), BF16),
                   jax.ShapeDtypeStruct((8, S5_LANES), F32)],
        scratch_shapes=[pltpu.VMEM((2 * rows, 2 * pc), F32), pltpu.VMEM((2 * rows, ps), F32),
                        pltpu.VMEM((2 * rows, 2 * pc), F32), pltpu.VMEM((8, ps), F32)],
        compiler_params=pltpu.CompilerParams(
            dimension_semantics=("parallel", "arbitrary"), vmem_limit_bytes=VMEM_LIMIT),
        name="s5_scan_prompt",
    )(x_all, h0, a1, a2, wb, wc, d_row)


def _s5_step_kernel(u_ref, h0r_ref, h0i_ref, ar_ref, ai_ref, wbr_ref, wbi_ref, wcr_ref, wci_ref, d_ref,
                    z_all_ref, z_ref, hr_ref, hi_ref):
    del z_all_ref
    u = u_ref[...]
    ub = u.astype(BF16)
    xr = _dot(ub, wbr_ref[...])
    xi = _dot(ub, wbi_ref[...])
    ar = ar_ref[...]
    ai = ai_ref[...]
    h0r = h0r_ref[...]
    h0i = h0i_ref[...]
    hr = xr + (ar * h0r - ai * h0i)
    hi = xi + (ar * h0i + ai * h0r)
    hr_ref[...] = hr
    hi_ref[...] = hi
    y = _dot(hr.astype(BF16), wcr_ref[...]) - _dot(hi.astype(BF16), wci_ref[...])
    y = y + d_ref[...] * u
    z_ref[...] = jax.nn.gelu(y).astype(BF16)


def _s5_step_sample(x_all, h0r, h0i, lj, abr, abi, wbr, wbi, wcr, wci, d_row, z_all):
    row_blk = ROWS_PROMPT // DEC_BATCH
    oct_w = pl.BlockSpec((None, S5_OCT_CH, S5_OCT_ST), lambda o: (o, 0, 0))
    oct_c = pl.BlockSpec((None, S5_OCT_ST, S5_OCT_CH), lambda o: (o, 0, 0))
    st_in = pl.BlockSpec((None, DEC_BATCH, S5_OCT_ST), lambda o: (lj, 0, o))
    st_out = pl.BlockSpec((DEC_BATCH, S5_OCT_ST), lambda o: (0, o))
    return pl.pallas_call(
        _s5_step_kernel,
        grid=(S5_NOCT,),
        in_specs=[
            pl.BlockSpec((DEC_BATCH, S5_OCT_CH), lambda o: (row_blk, o)),
            st_in, st_in,
            pl.BlockSpec((1, S5_OCT_ST), lambda o: (0, o)),
            pl.BlockSpec((1, S5_OCT_ST), lambda o: (0, o)),
            oct_w, oct_w, oct_c, oct_c,
            pl.BlockSpec((1, S5_OCT_CH), lambda o: (0, o)),
            pl.BlockSpec(memory_space=pl.ANY),
        ],
        out_specs=[pl.BlockSpec((DEC_BATCH, S5_OCT_CH), lambda o: (row_blk, o)), st_out, st_out],
        out_shape=[jax.ShapeDtypeStruct((ROWS_ALL, D_MODEL), BF16),
                   jax.ShapeDtypeStruct((DEC_BATCH, S5_LANES), F32),
                   jax.ShapeDtypeStruct((DEC_BATCH, S5_LANES), F32)],
        input_output_aliases={10: 0},
        compiler_params=pltpu.CompilerParams(
            dimension_semantics=("parallel",), vmem_limit_bytes=VMEM_LIMIT),
        name="s5_step_sample",
    )(x_all, h0r, h0i, abr, abi, wbr, wbi, wcr, wci, d_row, z_all)


def _mlstm_seq_kernel(q_ref, k_ref, v_ref, o_ref, gt_ref, bias_ref, ng_ref,
                      hn_ref, cf_ref, nf_ref, mf_ref, c_s, n_s, m_s, hn_s, *, nchunks):
    ci = pl.program_id(0)
    b = pl.program_id(1)
    L = ML_CHUNK
    rows = pl.ds(b, L, stride=BATCH)

    def seq_rows(ref, h, width):
        per_head = width // LANE
        return jnp.concatenate([ref[h * per_head + j, rows, :] for j in range(per_head)], axis=1)

    @pl.when(ci == 0)
    def _():
        c_s[b] = jnp.zeros(c_s.shape[1:], F32)
        n_s[b] = jnp.zeros(n_s.shape[1:], F32)
        m_s[b] = jnp.zeros(m_s.shape[1:], F32)

    gb = gt_ref[rows, :] + bias_ref[...]
    lfa = _log_sigmoid(gb)
    gbt = gb.T
    lft = lfa.T
    row = lax.broadcasted_iota(jnp.int32, (L, L), 0)
    col = lax.broadcasted_iota(jnp.int32, (L, L), 1)
    causal = col <= row
    tril = causal.astype(F32)
    triu = (row <= col).astype(F32)
    bc_cols = jnp.dot(tril, lfa, precision=lax.Precision.HIGHEST, preferred_element_type=F32)
    bc_rows = jnp.dot(lft[0:8, :], triu, precision=lax.Precision.HIGHEST, preferred_element_type=F32)

    for h in range(ML_HEADS):
        ig_row = gbt[h:h + 1, :]
        bc_row = bc_rows[ML_HEADS + h:ML_HEADS + h + 1, :]
        ig_col = gb[:, h:h + 1]
        bc_col = bc_cols[:, ML_HEADS + h:ML_HEADS + h + 1]
        m_prev = m_s[b, h:h + 1, 0:1]
        inter = bc_col + m_prev
        dmat = jnp.where(causal, bc_col - bc_row + ig_row, -jnp.inf)
        mt = jnp.maximum(inter, jnp.max(dmat, axis=1, keepdims=True))
        wts = jnp.exp(dmat - mt)

        qf = seq_rows(q_ref, h, ML_DK)
        kf = seq_rows(k_ref, h, ML_DK) * K_SCALE
        qb = qf.astype(BF16)
        vb = seq_rows(v_ref, h, ML_DV).astype(BF16)
        qk = lax.dot_general(qb, kf.astype(BF16), (((1,), (1,)), ((), ())), preferred_element_type=F32)
        sc = qk * wts
        scale = jnp.exp(inter - mt)
        cmat = c_s[b, h]
        n_row = n_s[b, h:h + 1, :]
        num = scale * _dot(qb, cmat.astype(BF16)) + _dot(sc.astype(BF16), vb)
        qn = jnp.sum(qb.astype(F32) * n_row, axis=1, keepdims=True)
        den = scale * qn + jnp.sum(sc, axis=1, keepdims=True)
        hh = num * (1.0 / jnp.maximum(jnp.abs(den), jnp.exp(-mt)))

        mu = jnp.mean(hh, axis=1, keepdims=True)
        hc = hh - mu
        var = jnp.mean(hc * hc, axis=1, keepdims=True)
        hn = hc * lax.rsqrt(var + LN_EPS) * ng_ref[:, h * ML_DV:(h + 1) * ML_DV]
        gated = jax.nn.sigmoid(seq_rows(o_ref, h, ML_DV)) * hn
        for j in range(ML_DV // LANE):
            hn_s[h * (ML_DV // LANE) + j, rows, :] = gated[:, j * LANE:(j + 1) * LANE]

        m_last = mt[L - 1:L, :]
        b_last = bc_col[L - 1:L, :]
        dec = jnp.exp(b_last - bc_col + ig_col - m_last)
        cscale = jnp.exp(b_last + m_prev - m_last)
        kd = kf * dec
        c_new = cscale * cmat + _dot(kd.T.astype(BF16), vb)
        n_new = cscale * n_row + jnp.sum(kd, axis=0, keepdims=True)
        c_s[b, h] = c_new
        n_s[b, h:h + 1, :] = n_new
        m_s[b, h:h + 1, :] = jnp.broadcast_to(m_last, (1, LANE))

    @pl.when(ci == nchunks - 1)
    def _():
        cf_ref[...] = c_s[b]
        nf_ref[...] = n_s[b, 0:ML_HEADS, :]
        mf_ref[...] = m_s[b]

    @pl.when(b == BATCH - 1)
    def _():
        for s in range(ML_HEADS * ML_DV // LANE):
            hn_ref[:, s * LANE:(s + 1) * LANE] = hn_s[s].astype(BF16)


def _mlstm_prompt(proj, gates, bias_row, norm_g, lj):
    nchunks = SEQ // ML_CHUNK
    rows = ML_CHUNK * BATCH
    qk_s = ML_HEADS * ML_DK // LANE
    v_s = ML_HEADS * ML_DV // LANE
    v_w = ML_HEADS * ML_DV
    last = nchunks - 1

    def state_idx(c, b):
        return jnp.where(c == last, b, 0)

    return pl.pallas_call(
        functools.partial(_mlstm_seq_kernel, nchunks=nchunks),
        grid=(nchunks, BATCH),
        in_specs=[
            pl.BlockSpec((qk_s, rows, LANE), lambda c, b: (0, c, 0)),
            pl.BlockSpec((qk_s, rows, LANE), lambda c, b: (1, c, 0)),
            pl.BlockSpec((v_s, rows, LANE), lambda c, b: (1, c, 0)),
            pl.BlockSpec((v_s, rows, LANE), lambda c, b: (2, c, 0)),
            pl.BlockSpec((rows, LANE), lambda c, b: (c, 0)),
            pl.BlockSpec((1, LANE), lambda c, b: (0, 0)),
            pl.BlockSpec((None, 1, v_w), lambda c, b: (lj, 0, 0)),
        ],
        out_specs=[
            pl.BlockSpec((rows, v_w), lambda c, b: (c, 0)),
            pl.BlockSpec((None, ML_HEADS, ML_DK, ML_DV), lambda c, b: (state_idx(c, b), 0, 0, 0)),
            pl.BlockSpec((None, ML_HEADS, ML_DK), lambda c, b: (state_idx(c, b), 0, 0)),
            pl.BlockSpec((None, 8, LANE), lambda c, b: (state_idx(c, b), 0, 0)),
        ],
        out_shape=[
            jax.ShapeDtypeStruct((ROWS_ALL, v_w), BF16),
            jax.ShapeDtypeStruct((BATCH, ML_HEADS, ML_DK, ML_DV), F32),
            jax.ShapeDtypeStruct((BATCH, ML_HEADS, ML_DK), F32),
            jax.ShapeDtypeStruct((BATCH, 8, LANE), F32),
        ],
        scratch_shapes=[pltpu.VMEM((BATCH, ML_HEADS, ML_DK, ML_DV), F32), pltpu.VMEM((BATCH, 8, ML_DK), F32),
                        pltpu.VMEM((BATCH, 8, LANE), F32), pltpu.VMEM((v_s, rows, LANE), F32)],
        compiler_params=pltpu.CompilerParams(
            dimension_semantics=("arbitrary", "arbitrary"), vmem_limit_bytes=VMEM_LIMIT),
        name="mlstm_prompt",
    )(proj, proj, proj, proj, gates, bias_row, norm_g)


def _mlstm_step_kernel(q_ref, k_ref, v_ref, o_ref, ka_ref, va_ref, gt_ref, bias_ref, ng_ref,
                       c_ref, n_ref, m_ref, hn_ref, co_ref, no_ref, mo_ref, kt_s, num_s):
    h = pl.program_id(0)
    i = pl.program_id(1)
    bb = ML_DEC_BB

    def cols(ref):
        return jnp.concatenate([ref[s] for s in range(ref.shape[0])], axis=1)

    @pl.when(i == 0)
    def _():
        kt_s[...] = (cols(ka_ref) * K_SCALE).T

    gb = gt_ref[...] + bias_ref[...]
    lane = lax.broadcasted_iota(jnp.int32, (bb, LANE), 1)
    ig = jnp.sum(jnp.where(lane == h, gb, 0.0), axis=1, keepdims=True)
    fp = jnp.sum(jnp.where(lane == h + ML_HEADS, gb, 0.0), axis=1, keepdims=True)
    lf = _log_sigmoid(fp)
    m4 = m_ref[...]
    hl = lax.broadcasted_iota(jnp.int32, (bb, ML_HEADS), 1)
    m_prev = jnp.sum(jnp.where(hl == h, m4, 0.0), axis=1, keepdims=True)
    inter = lf + m_prev
    mt = jnp.maximum(inter, ig)
    wts = jnp.exp(ig - mt)
    scale = jnp.exp(inter - mt)

    qb = cols(q_ref).astype(BF16)
    qf = qb.astype(F32)
    kf = cols(k_ref) * K_SCALE
    vf = cols(v_ref).astype(BF16).astype(F32)
    n_prev = n_ref[...]
    qk = jnp.sum(qf * kf.astype(BF16).astype(F32), axis=1, keepdims=True)
    sc = qk * wts
    qn = jnp.sum(qf * n_prev, axis=1, keepdims=True)
    den = scale * qn + sc
    rden = 1.0 / jnp.maximum(jnp.abs(den), jnp.exp(-mt))

    ktb = kt_s[...].astype(BF16)
    va = cols(va_ref)
    rowi = lax.broadcasted_iota(jnp.int32, (DEC_BATCH, ML_DV), 0)
    for j in range(bb):
        cj = c_ref[j]
        qc = _dot(qb, cj.astype(BF16))[j:j + 1, :]
        vsel = jnp.where(rowi == i * bb + j, va, 0.0).astype(BF16)
        outer = _dot(ktb, vsel)
        co_ref[j] = scale[j:j + 1, :] * cj + wts[j:j + 1, :] * outer
        num_s[j:j + 1, :] = scale[j:j + 1, :] * qc + sc[j:j + 1, :] * vf[j:j + 1, :]

    hh = num_s[...] * rden
    mu = jnp.mean(hh, axis=1, keepdims=True)
    hc = hh - mu
    var = jnp.mean(hc * hc, axis=1, keepdims=True)
    hn = hc * lax.rsqrt(var + LN_EPS) * ng_ref[...]
    hn_ref[...] = (jax.nn.sigmoid(cols(o_ref)) * hn).astype(BF16)
    no_ref[...] = scale * n_prev + wts * kf
    mo_ref[...] = jnp.broadcast_to(mt, (bb, LANE))


def _mlstm_sample(proj, gates, bias_row, norm_g, c_in, n_in, m_in, hn_all, c_acc, lj):
    bb = ML_DEC_BB
    rb = ROWS_PROMPT // bb
    r128 = ROWS_PROMPT // DEC_BATCH
    nh = ML_HEADS
    ks = ML_DK // LANE
    vs = ML_DV // LANE
    in_specs = [
        pl.BlockSpec((ks, bb, LANE), lambda h, i: (h, rb + i, 0)),
        pl.BlockSpec((ks, bb, LANE), lambda h, i: (nh + h, rb + i, 0)),
        pl.BlockSpec((vs, bb, LANE), lambda h, i: (nh + h, rb + i, 0)),
        pl.BlockSpec((vs, bb, LANE), lambda h, i: (2 * nh + h, rb + i, 0)),
        pl.BlockSpec((ks, DEC_BATCH, LANE), lambda h, i: (nh + h, r128, 0)),
        pl.BlockSpec((vs, DEC_BATCH, LANE), lambda h, i: (nh + h, r128, 0)),
        pl.BlockSpec((bb, LANE), lambda h, i: (rb + i, 0)),
        pl.BlockSpec((1, LANE), lambda h, i: (0, 0)),
        pl.BlockSpec((None, 1, ML_DV), lambda h, i: (lj, 0, h)),
        pl.BlockSpec((None, bb, None, ML_DK, ML_DV), lambda h, i: (lj, i, h, 0, 0)),
        pl.BlockSpec((None, bb, ML_DK), lambda h, i: (lj, i, h)),
        pl.BlockSpec((None, bb, nh), lambda h, i: (lj, i, 0)),
        pl.BlockSpec(memory_space=pl.ANY),
    ]
    args = [proj, proj, proj, proj, proj, proj, gates, bias_row, norm_g, c_in, n_in, m_in, hn_all]
    n_blocked = 12
    aliases = {n_blocked: 0}
    if c_acc is not None:
        in_specs.append(pl.BlockSpec(memory_space=pl.ANY))
        args.append(c_acc)
        aliases[n_blocked + 1] = 1

    def body(*refs):
        _mlstm_step_kernel(*refs[:n_blocked], *refs[len(args):])

    return pl.pallas_call(
        body,
        grid=(nh, DEC_BATCH // bb),
        in_specs=in_specs,
        out_specs=[
            pl.BlockSpec((bb, ML_DV), lambda h, i: (rb + i, h)),
            pl.BlockSpec((None, bb, None, ML_DK, ML_DV), lambda h, i: (lj, i, h, 0, 0)),
            pl.BlockSpec((bb, ML_DK), lambda h, i: (i, h)),
            pl.BlockSpec((None, bb, LANE), lambda h, i: (h, i, 0)),
        ],
        out_shape=[
            jax.ShapeDtypeStruct((ROWS_ALL, nh * ML_DV), BF16),
            jax.ShapeDtypeStruct(c_in.shape, F32),
            jax.ShapeDtypeStruct((DEC_BATCH, nh * ML_DK), F32),
            jax.ShapeDtypeStruct((nh, DEC_BATCH, LANE), F32),
        ],
        scratch_shapes=[pltpu.VMEM((ML_DK, DEC_BATCH), F32), pltpu.VMEM((bb, ML_DV), F32)],
        input_output_aliases=aliases,
        compiler_params=pltpu.CompilerParams(
            dimension_semantics=("arbitrary", "arbitrary"), vmem_limit_bytes=VMEM_LIMIT),
        name="mlstm_step_sample",
    )(*args)


def kernel(x_prompt, x_sample, state_s5_re, state_s5_im, state_mlstm_C, state_mlstm_n, state_mlstm_m,
           ln_g, ln_b, ffn_w_gate, ffn_w_up, ffn_w_down,
           s5_a_re, s5_a_im, s5_log_dt, s5_b_re, s5_b_im, s5_c_re, s5_c_im, s5_d, s5_w_a, s5_w_b,
           ml_w_in, ml_b_i, ml_b_f, ml_norm_g, ml_w_out):
    n_s5 = s5_a_re.shape[0]
    n_ml = ml_w_in.shape[0]

    wg, wu, wd = ffn_w_gate, ffn_w_up, ffn_w_down
    glu_a = s5_w_a.astype(BF16)
    glu_b = s5_w_b.astype(BF16)
    w_in = ml_w_in.astype(BF16)
    w_in_gate = jnp.pad(ml_w_in[:, :, ML_QKVO:].astype(BF16), ((0, 0), (0, 0), (0, LANE - 2 * ML_HEADS)))
    w_out = ml_w_out.astype(BF16)
    ln_g4 = ln_g.reshape(DEPTH, 3, 1, D_MODEL)
    ln_b4 = ln_b.reshape(DEPTH, 3, 1, D_MODEL)
    ml_bias = jnp.pad(jnp.concatenate([ml_b_i, ml_b_f], axis=1), ((0, 0), (0, LANE - 2 * ML_HEADS)))
    ml_ng = ml_norm_g.reshape(n_ml, 1, ML_HEADS * ML_DV)
    st_r = state_s5_re.reshape(n_s5, DEC_BATCH, S5_LANES)
    st_i = state_s5_im.reshape(n_s5, DEC_BATCH, S5_LANES)
    st_n = state_mlstm_n.reshape(n_ml, DEC_BATCH, ML_HEADS * ML_DK)

    x = _to_rows(x_prompt, x_sample.reshape(DEC_BATCH, D_MODEL))

    p_s5r, p_s5i, s_s5r, s_s5i = [], [], [], []
    p_c, p_n, p_m, s_n, s_m = [], [], [], [], []
    s_c = None
    zero_state = jnp.zeros((8, S5_LANES), F32)
    for li in range(DEPTH):
        x = _ffn_ln(x, wg, wu, wd, ln_g4, ln_b4, li, 0, 0)
        j = li // 2
        if li % 2 == 0:
            abr, abi, bbr, bbi = _s5_discretise(s5_a_re[j], s5_a_im[j], s5_log_dt[j], s5_b_re[j], s5_b_im[j])
            wbr, wbi = _block_diag_in(bbr), _block_diag_in(bbi)
            wcr, wci = _block_diag_out(s5_c_re[j]), _block_diag_out(s5_c_im[j])
            d_row = s5_d[j].reshape(1, D_MODEL)
            z, hp = _s5_scan_prompt(x, zero_state, abr, abi, wbr, wbi, wcr, wci, d_row)
            z, hsr, hsi = _s5_step_sample(x, st_r, st_i, j, abr, abi, wbr, wbi, wcr, wci, d_row, z)
            p_s5r.append(hp[:BATCH].reshape(BATCH, S5_GROUPS, S5_STATE))
            p_s5i.append(hp[BATCH:].reshape(BATCH, S5_GROUPS, S5_STATE))
            s_s5r.append(hsr.reshape(DEC_BATCH, S5_GROUPS, S5_STATE))
            s_s5i.append(hsi.reshape(DEC_BATCH, S5_GROUPS, S5_STATE))
            x = _proj_ln(x, z, (glu_a, glu_b), ln_g4, ln_b4, j, li)
        else:
            proj, gates = _in_proj(x, w_in, w_in_gate, j)
            bias_row = ml_bias[j].reshape(1, LANE)
            z, cp, npr, mp = _mlstm_prompt(proj, gates, bias_row, ml_ng, j)
            z, s_c, ns, ms = _mlstm_sample(proj, gates, bias_row, ml_ng, state_mlstm_C, st_n,
                                           state_mlstm_m, z, s_c, j)
            p_c.append(cp)
            p_n.append(npr)
            p_m.append(mp[:, :ML_HEADS, 0])
            s_n.append(ns.reshape(DEC_BATCH, ML_HEADS, ML_DK))
            s_m.append(jnp.swapaxes(ms[:, :, 0], 0, 1))
            x = _proj_ln(x, z, (w_out,), ln_g4, ln_b4, j, li)
        x = _ffn_ln(x, wg, wu, wd, ln_g4, ln_b4, li, 1, 2)

    y_prompt, y_sample = _from_rows(x)
    y_sample = y_sample.reshape(DEC_BATCH, 1, D_MODEL)
    return (y_prompt, y_sample,
            jnp.stack(p_s5r), jnp.stack(p_s5i), jnp.stack(p_c), jnp.stack(p_n), jnp.stack(p_m),
            jnp.stack(s_s5r), jnp.stack(s_s5i), s_c, jnp.stack(s_n), jnp.stack(s_m))
```

```python
import functools

import jax
import jax.numpy as jnp
from jax import lax
from jax.experimental import pallas as pl
from jax.experimental.pallas import tpu as pltpu

F32 = jnp.float32
BF16 = jnp.bfloat16

D_MODEL = 2048
BATCH = 4
SEQ = 2048
DEPTH = 4
DEC_BATCH = 128
D_FF = 5504
S5_GROUPS = 128
S5_GROUP = 16
S5_STATE = 64
S5_LANES = S5_GROUPS * S5_STATE
ML_HEADS = 4
ML_DK = 256
ML_DV = 512
ML_CHUNK = 128
ML_QKVO = 2 * ML_HEADS * ML_DK + 2 * ML_HEADS * ML_DV
DN_ALPHA = (2.0 * DEPTH) ** 0.25
LN_EPS = 1e-5
K_SCALE = ML_DK ** -0.5

LANE = 128
ROWS_PROMPT = SEQ * BATCH
ROWS_ALL = ROWS_PROMPT + DEC_BATCH
TM = 640
TM_FF = 1040
TF = 512
TN = 512
PROJ_LN_GROUPS = 2
TM_IN = 1040
TN_IN = 1024
S5_OCT = 8
S5_NOCT = S5_GROUPS // S5_OCT
S5_OCT_CH = S5_OCT * S5_GROUP
S5_OCT_ST = S5_OCT * S5_STATE
S5_STEPS = 256
S5_PAIR = 2
RL_STEPS = 1024
ML_DEC_BB = 16
VMEM_LIMIT = 56 * 1024 * 1024


def _dot(a, b):
    return jnp.dot(a, b, preferred_element_type=F32)


def _layer_norm_rows(y, g, b):
    mu = jnp.mean(y, axis=-1, keepdims=True)
    yc = y - mu
    var = jnp.mean(yc * yc, axis=-1, keepdims=True)
    return yc * lax.rsqrt(var + LN_EPS) * g + b


def _row_groups(n_rows, n_groups):
    step = -(-n_rows // n_groups)
    step = -(-step // 16) * 16
    return [slice(lo, min(lo + step, n_rows)) for lo in range(0, n_rows, step)]


def _log_sigmoid(x):
    return jnp.minimum(x, 0.0) - jnp.log1p(jnp.exp(-jnp.abs(x)))


def _to_rows_kernel(xp_ref, xs_ref, o_ref, *, nt):
    r = pl.program_id(0)

    @pl.when(r < nt)
    def _():
        for b in range(BATCH):
            o_ref[pl.ds(b, RL_STEPS, stride=BATCH), :] = xp_ref[b]

    @pl.when(r == nt)
    def _():
        o_ref[0:DEC_BATCH, :] = xs_ref[...]


def _to_rows(x_prompt, x_sample):
    nt = SEQ // RL_STEPS
    ns = D_MODEL // LANE
    return pl.pallas_call(
        functools.partial(_to_rows_kernel, nt=nt),
        grid=(nt + 1, ns),
        in_specs=[
            pl.BlockSpec((BATCH, RL_STEPS, LANE),
                         lambda r, s: (0, jnp.minimum(r, nt - 1), jnp.where(r == nt, ns - 1, s))),
            pl.BlockSpec((DEC_BATCH, LANE), lambda r, s: (0, s)),
        ],
        out_specs=pl.BlockSpec((RL_STEPS * BATCH, LANE), lambda r, s: (r, s)),
        out_shape=jax.ShapeDtypeStruct((ROWS_ALL, D_MODEL), F32),
        compiler_params=pltpu.CompilerParams(
            dimension_semantics=("arbitrary", "arbitrary"), vmem_limit_bytes=VMEM_LIMIT),
        name="to_rows",
    )(x_prompt, x_sample)


def _from_rows_kernel(x_ref, yp_ref, ys_ref, *, nt):
    r = pl.program_id(0)

    @pl.when(r < nt)
    def _():
        for b in range(BATCH):
            yp_ref[b] = x_ref[pl.ds(b, RL_STEPS, stride=BATCH), :]

    @pl.when(r == nt)
    def _():
        ys_ref[...] = x_ref[0:DEC_BATCH, :]


def _from_rows(x):
    nt = SEQ // RL_STEPS
    ns = D_MODEL // LANE
    return pl.pallas_call(
        functools.partial(_from_rows_kernel, nt=nt),
        grid=(nt + 1, ns),
        in_specs=[pl.BlockSpec((RL_STEPS * BATCH, LANE), lambda r, s: (r, s))],
        out_specs=[
            pl.BlockSpec((BATCH, RL_STEPS, LANE),
                         lambda r, s: (0, jnp.minimum(r, nt - 1), jnp.where(r == nt, ns - 1, s))),
            pl.BlockSpec((DEC_BATCH, LANE), lambda r, s: (0, jnp.where(r == nt, s, 0))),
        ],
        out_shape=[jax.ShapeDtypeStruct((BATCH, SEQ, D_MODEL), F32),
                   jax.ShapeDtypeStruct((DEC_BATCH, D_MODEL), F32)],
        compiler_params=pltpu.CompilerParams(
            dimension_semantics=("arbitrary", "arbitrary"), vmem_limit_bytes=VMEM_LIMIT),
        name="from_rows",
    )(x)


def _ffn_kernel(x_ref, wg_ref, wu_ref, wd_ref, g_ref, b_ref, o_ref, xb_ref, *, nf):
    f = pl.program_id(1)

    @pl.when(f == 0)
    def _():
        x = x_ref[...]
        xb_ref[...] = x.astype(BF16)
        o_ref[...] = (2.0 * DN_ALPHA) * x

    def hidden_tile(width):
        xb = xb_ref[...]
        hg = _dot(xb, wg_ref[:, :width])
        hu = _dot(xb, wu_ref[:, :width])
        act = (hg * jax.nn.sigmoid(hg) * hu).astype(BF16)
        o_ref[...] += _dot(act, wd_ref[:width, :])

    @pl.when(f < nf - 1)
    def _():
        hidden_tile(TF)

    @pl.when(f == nf - 1)
    def _():
        hidden_tile(D_FF - (nf - 1) * TF)
        o_ref[...] = _layer_norm_rows(0.5 * o_ref[...], g_ref[...], b_ref[...])


def _ffn_ln(x, wg, wu, wd, ln_g, ln_b, li, k, kn):
    m = x.shape[0]
    nf = pl.cdiv(D_FF, TF)
    return pl.pallas_call(
        functools.partial(_ffn_kernel, nf=nf),
        grid=(m // TM_FF, nf),
        in_specs=[
            pl.BlockSpec((TM_FF, D_MODEL), lambda i, f: (i, 0), pipeline_mode=pl.Buffered(1)),
            pl.BlockSpec((None, None, D_MODEL, TF), lambda i, f: (li, k, 0, f)),
            pl.BlockSpec((None, None, D_MODEL, TF), lambda i, f: (li, k, 0, f)),
            pl.BlockSpec((None, None, TF, D_MODEL), lambda i, f: (li, k, f, 0)),
            pl.BlockSpec((None, None, 1, D_MODEL), lambda i, f: (li, kn, 0, 0)),
            pl.BlockSpec((None, None, 1, D_MODEL), lambda i, f: (li, kn, 0, 0)),
        ],
        out_specs=pl.BlockSpec((TM_FF, D_MODEL), lambda i, f: (i, 0)),
        out_shape=jax.ShapeDtypeStruct((m, D_MODEL), F32),
        scratch_shapes=[pltpu.VMEM((TM_FF, D_MODEL), BF16)],
        compiler_params=pltpu.CompilerParams(
            dimension_semantics=("parallel", "arbitrary"), vmem_limit_bytes=VMEM_LIMIT),
        name="ffn_ln",
    )(x, wg, wu, wd, ln_g, ln_b)


def _proj_ln_kernel(x_ref, z_ref, *refs, gated, nn):
    w_refs, (g_ref, b_ref, o_ref, mix_ref) = refs[:-4], refs[-4:]
    n = pl.program_id(1)

    def mix_tile(rows):
        z = z_ref[rows, :]
        p = _dot(z, w_refs[0][...])
        if gated:
            p = p * jax.nn.sigmoid(_dot(z, w_refs[1][...]))
        return p

    @pl.when(n < nn - 1)
    def _():
        mix_ref[n] = mix_tile(slice(None))

    @pl.when(n == nn - 1)
    def _():
        def finish(rows, last_tile):
            tiles = [mix_ref[j, rows, :] for j in range(nn - 1)] + [last_tile]
            ys = [DN_ALPHA * x_ref[rows, j * TN:(j + 1) * TN] + tiles[j] for j in range(nn)]
            mu = sum(jnp.sum(y, axis=-1, keepdims=True) for y in ys) * (1.0 / D_MODEL)
            var = sum(jnp.sum((y - mu) * (y - mu), axis=-1, keepdims=True) for y in ys) * (1.0 / D_MODEL)
            r = lax.rsqrt(var + LN_EPS)
            for j in range(nn):
                sl = slice(j * TN, (j + 1) * TN)
                o_ref[rows, sl] = (ys[j] - mu) * r * g_ref[:, sl] + b_ref[:, sl]

        groups = _row_groups(o_ref.shape[0], PROJ_LN_GROUPS)
        pending = None
        for rows in groups:
            tile = mix_tile(rows)
            if pending is not None:
                finish(*pending)
            pending = (rows, tile)
        finish(*pending)


def _proj_ln(x, z, ws, ln_g, ln_b, lj, li):
    m = x.shape[0]
    nn = D_MODEL // TN
    gated = len(ws) == 2
    row_spec = pl.BlockSpec((TM, D_MODEL), lambda i, n: (i, 0))
    w_spec = pl.BlockSpec((None, D_MODEL, TN), lambda i, n: (lj, 0, n))
    ln_spec = pl.BlockSpec((None, None, 1, D_MODEL), lambda i, n: (li, 1, 0, 0))
    return pl.pallas_call(
        functools.partial(_proj_ln_kernel, gated=gated, nn=nn),
        grid=(m // TM, nn),
        in_specs=[row_spec, row_spec] + [w_spec] * len(ws) + [ln_spec, ln_spec],
        out_specs=row_spec,
        out_shape=jax.ShapeDtypeStruct((m, D_MODEL), F32),
        scratch_shapes=[pltpu.VMEM((nn - 1, TM, TN), F32)],
        compiler_params=pltpu.CompilerParams(
            dimension_semantics=("parallel", "arbitrary"), vmem_limit_bytes=VMEM_LIMIT),
        name="glu_ln" if gated else "out_ln",
    )(x, z, *ws, ln_g, ln_b)


def _in_proj_kernel(x_ref, w_ref, wg_ref, p_ref, gt_ref, xb_ref):
    n = pl.program_id(1)

    @pl.when(n == 0)
    def _():
        xb = x_ref[...].astype(BF16)
        xb_ref[...] = xb
        gt_ref[...] = _dot(xb, wg_ref[...])

    p = _dot(xb_ref[...], w_ref[...])
    for s in range(TN_IN // LANE):
        p_ref[s] = p[:, s * LANE:(s + 1) * LANE]


def _in_proj(x, w_in, w_gate, lj):
    m = x.shape[0]
    slabs = TN_IN // LANE
    return pl.pallas_call(
        _in_proj_kernel,
        grid=(m // TM_IN, ML_QKVO // TN_IN),
        in_specs=[
            pl.BlockSpec((TM_IN, D_MODEL), lambda i, n: (i, 0)),
            pl.BlockSpec((None, D_MODEL, TN_IN), lambda i, n: (lj, 0, n)),
            pl.BlockSpec((None, D_MODEL, LANE), lambda i, n: (lj, 0, 0)),
        ],
        out_specs=[
            pl.BlockSpec((slabs, TM_IN, LANE), lambda i, n: (n, i, 0)),
            pl.BlockSpec((TM_IN, LANE), lambda i, n: (i, 0)),
        ],
        out_shape=[jax.ShapeDtypeStruct((ML_QKVO // LANE, m, LANE), F32), jax.ShapeDtypeStruct((m, LANE), F32)],
        scratch_shapes=[pltpu.VMEM((TM_IN, D_MODEL), BF16)],
        compiler_params=pltpu.CompilerParams(
            dimension_semantics=("parallel", "arbitrary"), vmem_limit_bytes=VMEM_LIMIT),
        name="ml_in_proj",
    )(x, w_in, w_gate)


def _s5_disc_kernel(ar_ref, ai_ref, ldt_ref, br_ref, bi_ref, abr_ref, abi_ref, bbr_ref, bbi_ref):
    ar = ar_ref[...]
    ai = ai_ref[...]
    dt = jnp.exp(ldt_ref[...])
    mag = jnp.exp(dt * ar)
    abr = mag * jnp.cos(dt * ai)
    abi = mag * jnp.sin(dt * ai)
    zr = abr - 1.0
    zi = abi
    den = ar * ar + ai * ai
    gr = (zr * ar + zi * ai) / den
    gi = (zi * ar - zr * ai) / den
    br = br_ref[...]
    bi = bi_ref[...]
    abr_ref[...] = abr
    abi_ref[...] = abi
    bbr_ref[...] = gr * br - gi * bi
    bbi_ref[...] = gr * bi + gi * br


def _s5_discretise(a_re, a_im, log_dt, b_re, b_im):
    g, p, c = S5_GROUPS, S5_STATE, S5_GROUP
    outs = pl.pallas_call(
        _s5_disc_kernel,
        out_shape=[jax.ShapeDtypeStruct((g, 1, p), F32), jax.ShapeDtypeStruct((g, 1, p), F32),
                   jax.ShapeDtypeStruct((g, c, p), F32), jax.ShapeDtypeStruct((g, c, p), F32)],
        name="s5_discretise",
    )(a_re.reshape(g, 1, p), a_im.reshape(g, 1, p), log_dt.reshape(g, 1, 1),
      jnp.swapaxes(b_re, 1, 2), jnp.swapaxes(b_im, 1, 2))
    abr, abi, bbr, bbi = outs
    return abr.reshape(1, g * p), abi.reshape(1, g * p), bbr, bbi


def _block_diag_in(bb):
    eye = jnp.eye(S5_OCT, dtype=F32)
    w = bb.reshape(S5_NOCT, S5_OCT, S5_GROUP, 1, S5_STATE) * eye[None, :, None, :, None]
    return w.reshape(S5_NOCT, S5_OCT_CH, S5_OCT_ST).astype(BF16)


def _block_diag_out(c):
    eye = jnp.eye(S5_OCT, dtype=F32)
    cx = jnp.swapaxes(c.reshape(S5_NOCT, S5_OCT, S5_GROUP, S5_STATE), 2, 3)
    w = cx[:, :, :, None, :] * eye[None, :, None, :, None]
    return w.reshape(S5_NOCT, S5_OCT_ST, S5_OCT_CH).astype(BF16)


def _s5_seq_kernel(*refs, nsteps, nchunks):
    u_refs = refs[:S5_PAIR]
    (h0_ref, a1_ref, a2_ref, wb_ref, wc_ref, d_ref, z_ref, hf_ref,
     lhs_l, lhs_r, x8_s, zl_s, zr_s, zo_s, st_s) = refs[S5_PAIR:]
    c = pl.program_id(1)
    ch, st = S5_OCT_CH, S5_OCT_ST

    def seq_steps(b, stride, half=0):
        return pl.ds(half * BATCH + b, nsteps, stride=stride)

    @pl.when(c == 0)
    def _():
        st_s[...] = h0_ref[...]
        lhs_l[...] = jnp.zeros_like(lhs_l)
        lhs_r[...] = jnp.zeros_like(lhs_r)

    for o in range(S5_PAIR):
        for b in range(BATCH):
            ub = u_refs[o][seq_steps(b, BATCH), :]
            lhs_l[o, seq_steps(b, 8, 0), :] = ub
            lhs_r[o, seq_steps(b, 8, 1), :] = ub
        lhs = jnp.concatenate([lhs_l[o], lhs_r[o]], axis=1).astype(BF16)
        x8_s[:, o * st:(o + 1) * st] = _dot(lhs, wb_ref[o])

    a1 = a1_ref[...]
    a2 = a2_ref[...]

    def step(t, h):
        r = pl.multiple_of(t * 8, 8)
        hn = (a1 * h + x8_s[pl.ds(r, 8), :]) + a2 * pltpu.roll(h, BATCH, 0)
        x8_s[pl.ds(r, 8), :] = hn
        return hn

    h_last = lax.fori_loop(0, nsteps, step, st_s[...], unroll=4)
    st_s[...] = h_last

    for o in range(S5_PAIR):
        zz = _dot(x8_s[:, o * st:(o + 1) * st].astype(BF16), wc_ref[o])
        zl_s[...] = zz[:, :ch]
        zr_s[...] = zz[:, ch:]
        for b in range(BATCH):
            y = zl_s[seq_steps(b, 8, 0), :] + zr_s[seq_steps(b, 8, 1), :]
            y = y + d_ref[:, o * ch:(o + 1) * ch] * u_refs[o][seq_steps(b, BATCH), :]
            zo_s[seq_steps(b, BATCH), :] = jax.nn.gelu(y)
        z_ref[:, o * ch:(o + 1) * ch] = zo_s[...].astype(BF16)

    @pl.when(c == nchunks - 1)
    def _():
        hf_ref[...] = h_last


def _s5_scan_prompt(x_all, h0, abr, abi, wbr, wbi, wcr, wci, d_row):
    rows = S5_STEPS * BATCH
    nchunks = SEQ // S5_STEPS
    pc, ps = S5_PAIR * S5_OCT_CH, S5_PAIR * S5_OCT_ST
    a1 = jnp.broadcast_to(abr, (8, S5_LANES))
    a2 = jnp.concatenate([jnp.broadcast_to(-abi, (BATCH, S5_LANES)), jnp.broadcast_to(abi, (BATCH, S5_LANES))])
    wb = jnp.concatenate([wbr, wbi], axis=1)
    wc = jnp.concatenate([wcr, -wci], axis=2)
    st_spec = pl.BlockSpec((8, ps), lambda p, c: (0, p))
    return pl.pallas_call(
        functools.partial(_s5_seq_kernel, nsteps=S5_STEPS, nchunks=nchunks),
        grid=(S5_NOCT // S5_PAIR, nchunks),
        in_specs=[pl.BlockSpec((rows, S5_OCT_CH), lambda p, c, o=o: (c, S5_PAIR * p + o)) for o in range(S5_PAIR)] + [
            st_spec, st_spec, st_spec,
            pl.BlockSpec((S5_PAIR, 2 * S5_OCT_CH, S5_OCT_ST), lambda p, c: (p, 0, 0)),
            pl.BlockSpec((S5_PAIR, S5_OCT_ST, 2 * S5_OCT_CH), lambda p, c: (p, 0, 0)),
            pl.BlockSpec((1, pc), lambda p, c: (0, p)),
        ],
        out_specs=[pl.BlockSpec((rows, pc), lambda p, c: (c, p)), st_spec],
        out_shape=[jax.ShapeDtypeStruct((ROWS_ALL, D_MODEL), BF16),
                   jax.ShapeDtypeStruct((8, S5_LANES), F32)],
        scratch_shapes=[pltpu.VMEM((S5_PAIR, 2 * rows, S5_OCT_CH), F32), pltpu.VMEM((S5_PAIR, 2 * rows, S5_OCT_CH), F32),
                        pltpu.VMEM((2 * rows, ps), F32),
                        pltpu.VMEM((2 * rows, S5_OCT_CH), F32), pltpu.VMEM((2 * rows, S5_OCT_CH), F32),
                        pltpu.VMEM((rows, S5_OCT_CH), F32), pltpu.VMEM((8, ps), F32)],
        compiler_params=pltpu.CompilerParams(
            dimension_semantics=("parallel", "arbitrary"), vmem_limit_bytes=VMEM_LIMIT),
        name="s5_scan_prompt",
    )(*([x_all] * S5_PAIR), h0, a1, a2, wb, wc, d_row)


def _s5_step_kernel(u_ref, h0r_ref, h0i_ref, ar_ref, ai_ref, wbr_ref, wbi_ref, wcr_ref, wci_ref, d_ref,
                    z_all_ref, z_ref, hr_ref, hi_ref):
    del z_all_ref
    u = u_ref[...]
    ub = u.astype(BF16)
    xr = _dot(ub, wbr_ref[...])
    xi = _dot(ub, wbi_ref[...])
    ar = ar_ref[...]
    ai = ai_ref[...]
    h0r = h0r_ref[...]
    h0i = h0i_ref[...]
    hr = xr + (ar * h0r - ai * h0i)
    hi = xi + (ar * h0i + ai * h0r)
    hr_ref[...] = hr
    hi_ref[...] = hi
    y = _dot(hr.astype(BF16), wcr_ref[...]) - _dot(hi.astype(BF16), wci_ref[...])
    y = y + d_ref[...] * u
    z_ref[...] = jax.nn.gelu(y).astype(BF16)


def _s5_step_sample(x_all, h0r, h0i, lj, abr, abi, wbr, wbi, wcr, wci, d_row, z_all):
    row_blk = ROWS_PROMPT // DEC_BATCH
    oct_w = pl.BlockSpec((None, S5_OCT_CH, S5_OCT_ST), lambda o: (o, 0, 0))
    oct_c = pl.BlockSpec((None, S5_OCT_ST, S5_OCT_CH), lambda o: (o, 0, 0))
    st_in = pl.BlockSpec((None, DEC_BATCH, S5_OCT_ST), lambda o: (lj, 0, o))
    st_out = pl.BlockSpec((DEC_BATCH, S5_OCT_ST), lambda o: (0, o))
    return pl.pallas_call(
        _s5_step_kernel,
        grid=(S5_NOCT,),
        in_specs=[
            pl.BlockSpec((DEC_BATCH, S5_OCT_CH), lambda o: (row_blk, o)),
            st_in, st_in,
            pl.BlockSpec((1, S5_OCT_ST), lambda o: (0, o)),
            pl.BlockSpec((1, S5_OCT_ST), lambda o: (0, o)),
            oct_w, oct_w, oct_c, oct_c,
            pl.BlockSpec((1, S5_OCT_CH), lambda o: (0, o)),
            pl.BlockSpec(memory_space=pl.ANY),
        ],
        out_specs=[pl.BlockSpec((DEC_BATCH, S5_OCT_CH), lambda o: (row_blk, o)), st_out, st_out],
        out_shape=[jax.ShapeDtypeStruct((ROWS_ALL, D_MODEL), BF16),
                   jax.ShapeDtypeStruct((DEC_BATCH, S5_LANES), F32),
                   jax.ShapeDtypeStruct((DEC_BATCH, S5_LANES), F32)],
        input_output_aliases={10: 0},
        compiler_params=pltpu.CompilerParams(
            dimension_semantics=("parallel",), vmem_limit_bytes=VMEM_LIMIT),
        name="s5_step_sample",
    )(x_all, h0r, h0i, abr, abi, wbr, wbi, wcr, wci, d_row, z_all)


def _mlstm_seq_kernel(q_ref, k_ref, v_ref, o_ref, gt_ref, bias_ref, ng_ref,
                      hn_ref, cf_ref, nf_ref, mf_ref, c_s, n_s, m_s, hn_s, *, nchunks):
    ci = pl.program_id(0)
    b = pl.program_id(1)
    L = ML_CHUNK
    rows = pl.ds(b, L, stride=BATCH)

    def seq_rows(ref, h, width):
        per_head = width // LANE
        return jnp.concatenate([ref[h * per_head + j, rows, :] for j in range(per_head)], axis=1)

    @pl.when(ci == 0)
    def _():
        c_s[b] = jnp.zeros(c_s.shape[1:], F32)
        n_s[b] = jnp.zeros(n_s.shape[1:], F32)
        m_s[b] = jnp.zeros(m_s.shape[1:], F32)

    gb = gt_ref[rows, :] + bias_ref[...]
    lfa = _log_sigmoid(gb)
    gbt = gb.T
    lft = lfa.T
    row = lax.broadcasted_iota(jnp.int32, (L, L), 0)
    col = lax.broadcasted_iota(jnp.int32, (L, L), 1)
    causal = col <= row
    tril = causal.astype(F32)
    triu = (row <= col).astype(F32)
    bc_cols = jnp.dot(tril, lfa, precision=lax.Precision.HIGHEST, preferred_element_type=F32)
    bc_rows = jnp.dot(lft[0:8, :], triu, precision=lax.Precision.HIGHEST, preferred_element_type=F32)

    for h in range(ML_HEADS):
        ig_row = gbt[h:h + 1, :]
        bc_row = bc_rows[ML_HEADS + h:ML_HEADS + h + 1, :]
        ig_col = gb[:, h:h + 1]
        bc_col = bc_cols[:, ML_HEADS + h:ML_HEADS + h + 1]
        m_prev = m_s[b, h:h + 1, 0:1]
        inter = bc_col + m_prev
        dmat = jnp.where(causal, bc_col - bc_row + ig_row, -jnp.inf)
        mt = jnp.maximum(inter, jnp.max(dmat, axis=1, keepdims=True))
        wts = jnp.exp(dmat - mt)

        qf = seq_rows(q_ref, h, ML_DK)
        kf = seq_rows(k_ref, h, ML_DK) * K_SCALE
        qb = qf.astype(BF16)
        vb = seq_rows(v_ref, h, ML_DV).astype(BF16)
        qk = lax.dot_general(qb, kf.astype(BF16), (((1,), (1,)), ((), ())), preferred_element_type=F32)
        sc = qk * wts
        scale = jnp.exp(inter - mt)
        cmat = c_s[b, h]
        n_row = n_s[b, h:h + 1, :]
        num = scale * _dot(qb, cmat.astype(BF16)) + _dot(sc.astype(BF16), vb)
        qn = jnp.sum(qb.astype(F32) * n_row, axis=1, keepdims=True)
        den = scale * qn + jnp.sum(sc, axis=1, keepdims=True)
        hh = num * (1.0 / jnp.maximum(jnp.abs(den), jnp.exp(-mt)))

        mu = jnp.mean(hh, axis=1, keepdims=True)
        hc = hh - mu
        var = jnp.mean(hc * hc, axis=1, keepdims=True)
        hn = hc * lax.rsqrt(var + LN_EPS) * ng_ref[:, h * ML_DV:(h + 1) * ML_DV]
        gated = jax.nn.sigmoid(seq_rows(o_ref, h, ML_DV)) * hn
        for j in range(ML_DV // LANE):
            hn_s[h * (ML_DV // LANE) + j, rows, :] = gated[:, j * LANE:(j + 1) * LANE]

        m_last = mt[L - 1:L, :]
        b_last = bc_col[L - 1:L, :]
        dec = jnp.exp(b_last - bc_col + ig_col - m_last)
        cscale = jnp.exp(b_last + m_prev - m_last)
        kd = kf * dec
        c_new = cscale * cmat + _dot(kd.T.astype(BF16), vb)
        n_new = cscale * n_row + jnp.sum(kd, axis=0, keepdims=True)
        c_s[b, h] = c_new
        n_s[b, h:h + 1, :] = n_new
        m_s[b, h:h + 1, :] = jnp.broadcast_to(m_last, (1, LANE))

    @pl.when(ci == nchunks - 1)
    def _():
        cf_ref[...] = c_s[b]
        nf_ref[...] = n_s[b, 0:ML_HEADS, :]
        mf_ref[...] = m_s[b]

    @pl.when(b == BATCH - 1)
    def _():
        for s in range(ML_HEADS * ML_DV // LANE):
            hn_ref[:, s * LANE:(s + 1) * LANE] = hn_s[s].astype(BF16)


def _mlstm_prompt(proj, gates, bias_row, norm_g, lj):
    nchunks = SEQ // ML_CHUNK
    rows = ML_CHUNK * BATCH
    qk_s = ML_HEADS * ML_DK // LANE
    v_s = ML_HEADS * ML_DV // LANE
    v_w = ML_HEADS * ML_DV
    last = nchunks - 1

    def state_idx(c, b):
        return jnp.where(c == last, b, 0)

    return pl.pallas_call(
        functools.partial(_mlstm_seq_kernel, nchunks=nchunks),
        grid=(nchunks, BATCH),
        in_specs=[
            pl.BlockSpec((qk_s, rows, LANE), lambda c, b: (0, c, 0)),
            pl.BlockSpec((qk_s, rows, LANE), lambda c, b: (1, c, 0)),
            pl.BlockSpec((v_s, rows, LANE), lambda c, b: (1, c, 0)),
            pl.BlockSpec((v_s, rows, LANE), lambda c, b: (2, c, 0)),
            pl.BlockSpec((rows, LANE), lambda c, b: (c, 0)),
            pl.BlockSpec((1, LANE), lambda c, b: (0, 0)),
            pl.BlockSpec((None, 1, v_w), lambda c, b: (lj, 0, 0)),
        ],
        out_specs=[
            pl.BlockSpec((rows, v_w), lambda c, b: (c, 0)),
            pl.BlockSpec((None, ML_HEADS, ML_DK, ML_DV), lambda c, b: (state_idx(c, b), 0, 0, 0)),
            pl.BlockSpec((None, ML_HEADS, ML_DK), lambda c, b: (state_idx(c, b), 0, 0)),
            pl.BlockSpec((None, 8, LANE), lambda c, b: (state_idx(c, b), 0, 0)),
        ],
        out_shape=[
            jax.ShapeDtypeStruct((ROWS_ALL, v_w), BF16),
            jax.ShapeDtypeStruct((BATCH, ML_HEADS, ML_DK, ML_DV), F32),
            jax.ShapeDtypeStruct((BATCH, ML_HEADS, ML_DK), F32),
            jax.ShapeDtypeStruct((BATCH, 8, LANE), F32),
        ],
        scratch_shapes=[pltpu.VMEM((BATCH, ML_HEADS, ML_DK, ML_DV), F32), pltpu.VMEM((BATCH, 8, ML_DK), F32),
                        pltpu.VMEM((BATCH, 8, LANE), F32), pltpu.VMEM((v_s, rows, LANE), F32)],
        compiler_params=pltpu.CompilerParams(
            dimension_semantics=("arbitrary", "arbitrary"), vmem_limit_bytes=VMEM_LIMIT),
        name="mlstm_prompt",
    )(proj, proj, proj, proj, gates, bias_row, norm_g)


def _mlstm_step_kernel(q_ref, k_ref, v_ref, o_ref, ka_ref, va_ref, gt_ref, bias_ref, ng_ref,
                       c_ref, n_ref, m_ref, hn_ref, co_ref, no_ref, mo_ref, kt_s, num_s):
    h = pl.program_id(0)
    i = pl.program_id(1)
    bb = ML_DEC_BB

    def cols(ref):
        return jnp.concatenate([ref[s] for s in range(ref.shape[0])], axis=1)

    @pl.when(i == 0)
    def _():
        kt_s[...] = (cols(ka_ref) * K_SCALE).T

    gb = gt_ref[...] + bias_ref[...]
    lane = lax.broadcasted_iota(jnp.int32, (bb, LANE), 1)
    ig = jnp.sum(jnp.where(lane == h, gb, 0.0), axis=1, keepdims=True)
    fp = jnp.sum(jnp.where(lane == h + ML_HEADS, gb, 0.0), axis=1, keepdims=True)
    lf = _log_sigmoid(fp)
    m4 = m_ref[...]
    hl = lax.broadcasted_iota(jnp.int32, (bb, ML_HEADS), 1)
    m_prev = jnp.sum(jnp.where(hl == h, m4, 0.0), axis=1, keepdims=True)
    inter = lf + m_prev
    mt = jnp.maximum(inter, ig)
    wts = jnp.exp(ig - mt)
    scale = jnp.exp(inter - mt)

    qb = cols(q_ref).astype(BF16)
    qf = qb.astype(F32)
    kf = cols(k_ref) * K_SCALE
    vf = cols(v_ref).astype(BF16).astype(F32)
    n_prev = n_ref[...]
    qk = jnp.sum(qf * kf.astype(BF16).astype(F32), axis=1, keepdims=True)
    sc = qk * wts
    qn = jnp.sum(qf * n_prev, axis=1, keepdims=True)
    den = scale * qn + sc
    rden = 1.0 / jnp.maximum(jnp.abs(den), jnp.exp(-mt))

    ktb = kt_s[...].astype(BF16)
    va = cols(va_ref)
    rowi = lax.broadcasted_iota(jnp.int32, (DEC_BATCH, ML_DV), 0)
    for j in range(bb):
        cj = c_ref[j]
        qc = _dot(qb, cj.astype(BF16))[j:j + 1, :]
        vsel = jnp.where(rowi == i * bb + j, va, 0.0).astype(BF16)
        outer = _dot(ktb, vsel)
        co_ref[j] = scale[j:j + 1, :] * cj + wts[j:j + 1, :] * outer
        num_s[j:j + 1, :] = scale[j:j + 1, :] * qc + sc[j:j + 1, :] * vf[j:j + 1, :]

    hh = num_s[...] * rden
    mu = jnp.mean(hh, axis=1, keepdims=True)
    hc = hh - mu
    var = jnp.mean(hc * hc, axis=1, keepdims=True)
    hn = hc * lax.rsqrt(var + LN_EPS) * ng_ref[...]
    hn_ref[...] = (jax.nn.sigmoid(cols(o_ref)) * hn).astype(BF16)
    no_ref[...] = scale * n_prev + wts * kf
    mo_ref[...] = jnp.broadcast_to(mt, (bb, LANE))


def _mlstm_sample(proj, gates, bias_row, norm_g, c_in, n_in, m_in, hn_all, c_acc, lj):
    bb = ML_DEC_BB
    rb = ROWS_PROMPT // bb
    r128 = ROWS_PROMPT // DEC_BATCH
    nh = ML_HEADS
    ks = ML_DK // LANE
    vs = ML_DV // LANE
    in_specs = [
        pl.BlockSpec((ks, bb, LANE), lambda h, i: (h, rb + i, 0)),
        pl.BlockSpec((ks, bb, LANE), lambda h, i: (nh + h, rb + i, 0)),
        pl.BlockSpec((vs, bb, LANE), lambda h, i: (nh + h, rb + i, 0)),
        pl.BlockSpec((vs, bb, LANE), lambda h, i: (2 * nh + h, rb + i, 0)),
        pl.BlockSpec((ks, DEC_BATCH, LANE), lambda h, i: (nh + h, r128, 0)),
        pl.BlockSpec((vs, DEC_BATCH, LANE), lambda h, i: (nh + h, r128, 0)),
        pl.BlockSpec((bb, LANE), lambda h, i: (rb + i, 0)),
        pl.BlockSpec((1, LANE), lambda h, i: (0, 0)),
        pl.BlockSpec((None, 1, ML_DV), lambda h, i: (lj, 0, h)),
        pl.BlockSpec((None, bb, None, ML_DK, ML_DV), lambda h, i: (lj, i, h, 0, 0)),
        pl.BlockSpec((None, bb, ML_DK), lambda h, i: (lj, i, h)),
        pl.BlockSpec((None, bb, nh), lambda h, i: (lj, i, 0)),
        pl.BlockSpec(memory_space=pl.ANY),
    ]
    args = [proj, proj, proj, proj, proj, proj, gates, bias_row, norm_g, c_in, n_in, m_in, hn_all]
    n_blocked = 12
    aliases = {n_blocked: 0}
    if c_acc is not None:
        in_specs.append(pl.BlockSpec(memory_space=pl.ANY))
        args.append(c_acc)
        aliases[n_blocked + 1] = 1

    def body(*refs):
        _mlstm_step_kernel(*refs[:n_blocked], *refs[len(args):])

    return pl.pallas_call(
        body,
        grid=(nh, DEC_BATCH // bb),
        in_specs=in_specs,
        out_specs=[
            pl.BlockSpec((bb, ML_DV), lambda h, i: (rb + i, h)),
            pl.BlockSpec((None, bb, None, ML_DK, ML_DV), lambda h, i: (lj, i, h, 0, 0)),
            pl.BlockSpec((bb, ML_DK), lambda h, i: (i, h)),
            pl.BlockSpec((None, bb, LANE), lambda h, i: (h, i, 0)),
        ],
        out_shape=[
            jax.ShapeDtypeStruct((ROWS_ALL, nh * ML_DV), BF16),
            jax.ShapeDtypeStruct(c_in.shape, F32),
            jax.ShapeDtypeStruct((DEC_BATCH, nh * ML_DK), F32),
            jax.ShapeDtypeStruct((nh, DEC_BATCH, LANE), F32),
        ],
        scratch_shapes=[pltpu.VMEM((ML_DK, DEC_BATCH), F32), pltpu.VMEM((bb, ML_DV), F32)],
        input_output_aliases=aliases,
        compiler_params=pltpu.CompilerParams(
            dimension_semantics=("arbitrary", "arbitrary"), vmem_limit_bytes=VMEM_LIMIT),
        name="mlstm_step_sample",
    )(*args)


def kernel(x_prompt, x_sample, state_s5_re, state_s5_im, state_mlstm_C, state_mlstm_n, state_mlstm_m,
           ln_g, ln_b, ffn_w_gate, ffn_w_up, ffn_w_down,
           s5_a_re, s5_a_im, s5_log_dt, s5_b_re, s5_b_im, s5_c_re, s5_c_im, s5_d, s5_w_a, s5_w_b,
           ml_w_in, ml_b_i, ml_b_f, ml_norm_g, ml_w_out):
    n_s5 = s5_a_re.shape[0]
    n_ml = ml_w_in.shape[0]

    wg = ffn_w_gate.astype(BF16)
    wu = ffn_w_up.astype(BF16)
    wd = ffn_w_down.astype(BF16)
    glu_a = s5_w_a.astype(BF16)
    glu_b = s5_w_b.astype(BF16)
    w_in = ml_w_in.astype(BF16)
    w_in_gate = jnp.pad(ml_w_in[:, :, ML_QKVO:].astype(BF16), ((0, 0), (0, 0), (0, LANE - 2 * ML_HEADS)))
    w_out = ml_w_out.astype(BF16)
    ln_g4 = ln_g.reshape(DEPTH, 3, 1, D_MODEL)
    ln_b4 = ln_b.reshape(DEPTH, 3, 1, D_MODEL)
    ml_bias = jnp.pad(jnp.concatenate([ml_b_i, ml_b_f], axis=1), ((0, 0), (0, LANE - 2 * ML_HEADS)))
    ml_ng = ml_norm_g.reshape(n_ml, 1, ML_HEADS * ML_DV)
    st_r = state_s5_re.reshape(n_s5, DEC_BATCH, S5_LANES)
    st_i = state_s5_im.reshape(n_s5, DEC_BATCH, S5_LANES)
    st_n = state_mlstm_n.reshape(n_ml, DEC_BATCH, ML_HEADS * ML_DK)

    x = _to_rows(x_prompt, x_sample.reshape(DEC_BATCH, D_MODEL))

    p_s5r, p_s5i, s_s5r, s_s5i = [], [], [], []
    p_c, p_n, p_m, s_n, s_m = [], [], [], [], []
    s_c = None
    zero_state = jnp.zeros((8, S5_LANES), F32)
    for li in range(DEPTH):
        x = _ffn_ln(x, wg, wu, wd, ln_g4, ln_b4, li, 0, 0)
        j = li // 2
        if li % 2 == 0:
            abr, abi, bbr, bbi = _s5_discretise(s5_a_re[j], s5_a_im[j], s5_log_dt[j], s5_b_re[j], s5_b_im[j])
            wbr, wbi = _block_diag_in(bbr), _block_diag_in(bbi)
            wcr, wci = _block_diag_out(s5_c_re[j]), _block_diag_out(s5_c_im[j])
            d_row = s5_d[j].reshape(1, D_MODEL)
            z, hp = _s5_scan_prompt(x, zero_state, abr, abi, wbr, wbi, wcr, wci, d_row)
            z, hsr, hsi = _s5_step_sample(x, st_r, st_i, j, abr, abi, wbr, wbi, wcr, wci, d_row, z)
            p_s5r.append(hp[:BATCH].reshape(BATCH, S5_GROUPS, S5_STATE))
            p_s5i.append(hp[BATCH:].reshape(BATCH, S5_GROUPS, S5_STATE))
            s_s5r.append(hsr.reshape(DEC_BATCH, S5_GROUPS, S5_STATE))
            s_s5i.append(hsi.reshape(DEC_BATCH, S5_GROUPS, S5_STATE))
            x = _proj_ln(x, z, (glu_a, glu_b), ln_g4, ln_b4, j, li)
        else:
            proj, gates = _in_proj(x, w_in, w_in_gate, j)
            bias_row = ml_bias[j].reshape(1, LANE)
            z, cp, npr, mp = _mlstm_prompt(proj, gates, bias_row, ml_ng, j)
            z, s_c, ns, ms = _mlstm_sample(proj, gates, bias_row, ml_ng, state_mlstm_C, st_n,
                                           state_mlstm_m, z, s_c, j)
            p_c.append(cp)
            p_n.append(npr)
            p_m.append(mp[:, :ML_HEADS, 0])
            s_n.append(ns.reshape(DEC_BATCH, ML_HEADS, ML_DK))
            s_m.append(jnp.swapaxes(ms[:, :, 0], 0, 1))
            x = _proj_ln(x, z, (w_out,), ln_g4, ln_b4, j, li)
        x = _ffn_ln(x, wg, wu, wd, ln_g4, ln_b4, li, 1, 2)

    y_prompt, y_sample = _from_rows(x)
    y_sample = y_sample.reshape(DEC_BATCH, 1, D_MODEL)
    return (y_prompt, y_sample,
            jnp.stack(p_s5r), jnp.stack(p_s5i), jnp.stack(p_c), jnp.stack(p_n), jnp.stack(p_m),
            jnp.stack(s_s5r), jnp.stack(s_s5i), s_c, jnp.stack(s_n), jnp.stack(s_m))
```

```python
import functools

import jax
import jax.numpy as jnp
from jax import lax
from jax.experimental import pallas as pl
from jax.experimental.pallas import tpu as pltpu

F32 = jnp.float32
BF16 = jnp.bfloat16

D_MODEL = 2048
BATCH = 4
SEQ = 2048
DEPTH = 4
DEC_BATCH = 128
D_FF = 5504
S5_GROUPS = 128
S5_GROUP = 16
S5_STATE = 64
S5_LANES = S5_GROUPS * S5_STATE
ML_HEADS = 4
ML_DK = 256
ML_DV = 512
ML_CHUNK = 128
ML_QKVO = 2 * ML_HEADS * ML_DK + 2 * ML_HEADS * ML_DV
DN_ALPHA = (2.0 * DEPTH) ** 0.25
LN_EPS = 1e-5
K_SCALE = ML_DK ** -0.5

LANE = 128
ROWS_PROMPT = SEQ * BATCH
ROWS_ALL = ROWS_PROMPT + DEC_BATCH
TM = 640
TM_FF = 832
TF = 512
CV_BLOCKS = 8
TN = 512
PROJ_LN_GROUPS = 2
TM_IN = 1040
TN_IN = 1024
S5_OCT = 8
S5_NOCT = S5_GROUPS // S5_OCT
S5_OCT_CH = S5_OCT * S5_GROUP
S5_OCT_ST = S5_OCT * S5_STATE
S5_STEPS = 256
S5_PAIR = 2
RL_STEPS = 1024
ML_DEC_BB = 16
VMEM_LIMIT = 56 * 1024 * 1024


def _dot(a, b):
    return jnp.dot(a, b, preferred_element_type=F32)


def _layer_norm_rows(y, g, b):
    mu = jnp.mean(y, axis=-1, keepdims=True)
    yc = y - mu
    var = jnp.mean(yc * yc, axis=-1, keepdims=True)
    return yc * lax.rsqrt(var + LN_EPS) * g + b


def _row_groups(n_rows, n_groups):
    step = -(-n_rows // n_groups)
    step = -(-step // 16) * 16
    return [slice(lo, min(lo + step, n_rows)) for lo in range(0, n_rows, step)]


def _log_sigmoid(x):
    return jnp.minimum(x, 0.0) - jnp.log1p(jnp.exp(-jnp.abs(x)))


def _to_rows_kernel(xp_ref, xs_ref, o_ref, *, nt):
    r = pl.program_id(0)

    @pl.when(r < nt)
    def _():
        for b in range(BATCH):
            o_ref[pl.ds(b, RL_STEPS, stride=BATCH), :] = xp_ref[b]

    @pl.when(r == nt)
    def _():
        o_ref[0:DEC_BATCH, :] = xs_ref[...]


def _to_rows(x_prompt, x_sample):
    nt = SEQ // RL_STEPS
    ns = D_MODEL // LANE
    return pl.pallas_call(
        functools.partial(_to_rows_kernel, nt=nt),
        grid=(nt + 1, ns),
        in_specs=[
            pl.BlockSpec((BATCH, RL_STEPS, LANE),
                         lambda r, s: (0, jnp.minimum(r, nt - 1), jnp.where(r == nt, ns - 1, s))),
            pl.BlockSpec((DEC_BATCH, LANE), lambda r, s: (0, s)),
        ],
        out_specs=pl.BlockSpec((RL_STEPS * BATCH, LANE), lambda r, s: (r, s)),
        out_shape=jax.ShapeDtypeStruct((ROWS_ALL, D_MODEL), F32),
        compiler_params=pltpu.CompilerParams(
            dimension_semantics=("arbitrary", "arbitrary"), vmem_limit_bytes=VMEM_LIMIT),
        name="to_rows",
    )(x_prompt, x_sample)


def _from_rows_kernel(x_ref, yp_ref, ys_ref, *, nt):
    r = pl.program_id(0)

    @pl.when(r < nt)
    def _():
        for b in range(BATCH):
            yp_ref[b] = x_ref[pl.ds(b, RL_STEPS, stride=BATCH), :]

    @pl.when(r == nt)
    def _():
        ys_ref[...] = x_ref[0:DEC_BATCH, :]


def _from_rows(x):
    nt = SEQ // RL_STEPS
    ns = D_MODEL // LANE
    return pl.pallas_call(
        functools.partial(_from_rows_kernel, nt=nt),
        grid=(nt + 1, ns),
        in_specs=[pl.BlockSpec((RL_STEPS * BATCH, LANE), lambda r, s: (r, s))],
        out_specs=[
            pl.BlockSpec((BATCH, RL_STEPS, LANE),
                         lambda r, s: (0, jnp.minimum(r, nt - 1), jnp.where(r == nt, ns - 1, s))),
            pl.BlockSpec((DEC_BATCH, LANE), lambda r, s: (0, jnp.where(r == nt, s, 0))),
        ],
        out_shape=[jax.ShapeDtypeStruct((BATCH, SEQ, D_MODEL), F32),
                   jax.ShapeDtypeStruct((DEC_BATCH, D_MODEL), F32)],
        compiler_params=pltpu.CompilerParams(
            dimension_semantics=("arbitrary", "arbitrary"), vmem_limit_bytes=VMEM_LIMIT),
        name="from_rows",
    )(x)


def _ffn_kernel(x_ref, wg_ref, wu_ref, wd_ref, g_ref, b_ref, *refs, nf, convert_next):
    if convert_next:
        ng_ref, nu_ref, nd_ref, o_ref, cg_ref, cu_ref, cd_ref, xb_ref = refs
    else:
        o_ref, xb_ref = refs
    f = pl.program_id(1)

    @pl.when(f == 0)
    def _():
        x = x_ref[...]
        xb_ref[...] = x.astype(BF16)
        o_ref[...] = (2.0 * DN_ALPHA) * x

    def hidden_tile(width):
        xb = xb_ref[...]
        hg = _dot(xb, wg_ref[:, :width])
        hu = _dot(xb, wu_ref[:, :width])
        act = (hg * jax.nn.sigmoid(hg) * hu).astype(BF16)
        o_ref[...] += _dot(act, wd_ref[:width, :])
        if convert_next:
            cg_ref[:, :width] = ng_ref[:, :width].astype(BF16)
            cu_ref[:, :width] = nu_ref[:, :width].astype(BF16)
            cd_ref[:width, :] = nd_ref[:width, :].astype(BF16)

    @pl.when(f < nf - 1)
    def _():
        hidden_tile(TF)

    @pl.when(f == nf - 1)
    def _():
        hidden_tile(D_FF - (nf - 1) * TF)
        o_ref[...] = _layer_norm_rows(0.5 * o_ref[...], g_ref[...], b_ref[...])


def _ffn_ln(x, w_cur, ln_g, ln_b, li, kn, w_f32, nxt):
    m = x.shape[0]
    nf = pl.cdiv(D_FF, TF)
    n_row_tiles = m // TM_FF
    cv = D_MODEL // CV_BLOCKS
    assert n_row_tiles >= CV_BLOCKS
    in_specs = [
        pl.BlockSpec((TM_FF, D_MODEL), lambda i, f: (i, 0)),
        pl.BlockSpec((D_MODEL, TF), lambda i, f: (0, f)),
        pl.BlockSpec((D_MODEL, TF), lambda i, f: (0, f)),
        pl.BlockSpec((TF, D_MODEL), lambda i, f: (f, 0)),
        pl.BlockSpec((None, None, 1, D_MODEL), lambda i, f: (li, kn, 0, 0)),
        pl.BlockSpec((None, None, 1, D_MODEL), lambda i, f: (li, kn, 0, 0)),
    ]
    out_specs = [pl.BlockSpec((TM_FF, D_MODEL), lambda i, f: (i, 0))]
    out_shape = [jax.ShapeDtypeStruct((m, D_MODEL), F32)]
    args = [x, *w_cur, ln_g, ln_b]
    if nxt is not None:
        nli, nk = nxt

        def cv_row(i, f):
            return jnp.minimum(i, CV_BLOCKS - 1)

        def cv_col(i, f):
            return jnp.where(i < CV_BLOCKS, f, nf - 1)

        in_specs += [
            pl.BlockSpec((None, None, cv, TF), lambda i, f: (nli, nk, cv_row(i, f), cv_col(i, f))),
            pl.BlockSpec((None, None, cv, TF), lambda i, f: (nli, nk, cv_row(i, f), cv_col(i, f))),
            pl.BlockSpec((None, None, TF, cv), lambda i, f: (nli, nk, cv_col(i, f), cv_row(i, f))),
        ]
        out_specs += [
            pl.BlockSpec((cv, TF), lambda i, f: (cv_row(i, f), cv_col(i, f))),
            pl.BlockSpec((cv, TF), lambda i, f: (cv_row(i, f), cv_col(i, f))),
            pl.BlockSpec((TF, cv), lambda i, f: (cv_col(i, f), cv_row(i, f))),
        ]
        out_shape += [jax.ShapeDtypeStruct((D_MODEL, D_FF), BF16), jax.ShapeDtypeStruct((D_MODEL, D_FF), BF16),
                      jax.ShapeDtypeStruct((D_FF, D_MODEL), BF16)]
        args += list(w_f32)
    outs = pl.pallas_call(
        functools.partial(_ffn_kernel, nf=nf, convert_next=nxt is not None),
        grid=(n_row_tiles, nf),
        in_specs=in_specs,
        out_specs=out_specs,
        out_shape=out_shape,
        scratch_shapes=[pltpu.VMEM((TM_FF, D_MODEL), BF16)],
        compiler_params=pltpu.CompilerParams(
            dimension_semantics=("arbitrary", "arbitrary"), vmem_limit_bytes=VMEM_LIMIT),
        name="ffn_ln",
    )(*args)
    return outs[0], (tuple(outs[1:]) if nxt is not None else None)


def _proj_ln_kernel(x_ref, z_ref, *refs, gated, nn):
    w_refs, (g_ref, b_ref, o_ref, mix_ref) = refs[:-4], refs[-4:]
    n = pl.program_id(1)

    def mix_tile(rows):
        z = z_ref[rows, :]
        p = _dot(z, w_refs[0][...])
        if gated:
            p = p * jax.nn.sigmoid(_dot(z, w_refs[1][...]))
        return p

    @pl.when(n < nn - 1)
    def _():
        mix_ref[n] = mix_tile(slice(None))

    @pl.when(n == nn - 1)
    def _():
        def finish(rows, last_tile):
            tiles = [mix_ref[j, rows, :] for j in range(nn - 1)] + [last_tile]
            ys = [DN_ALPHA * x_ref[rows, j * TN:(j + 1) * TN] + tiles[j] for j in range(nn)]
            mu = sum(jnp.sum(y, axis=-1, keepdims=True) for y in ys) * (1.0 / D_MODEL)
            var = sum(jnp.sum((y - mu) * (y - mu), axis=-1, keepdims=True) for y in ys) * (1.0 / D_MODEL)
            r = lax.rsqrt(var + LN_EPS)
            for j in range(nn):
                sl = slice(j * TN, (j + 1) * TN)
                o_ref[rows, sl] = (ys[j] - mu) * r * g_ref[:, sl] + b_ref[:, sl]

        groups = _row_groups(o_ref.shape[0], PROJ_LN_GROUPS)
        pending = None
        for rows in groups:
            tile = mix_tile(rows)
            if pending is not None:
                finish(*pending)
            pending = (rows, tile)
        finish(*pending)


def _proj_ln(x, z, ws, ln_g, ln_b, lj, li):
    m = x.shape[0]
    nn = D_MODEL // TN
    gated = len(ws) == 2
    row_spec = pl.BlockSpec((TM, D_MODEL), lambda i, n: (i, 0))
    w_spec = pl.BlockSpec((None, D_MODEL, TN), lambda i, n: (lj, 0, n))
    ln_spec = pl.BlockSpec((None, None, 1, D_MODEL), lambda i, n: (li, 1, 0, 0))
    return pl.pallas_call(
        functools.partial(_proj_ln_kernel, gated=gated, nn=nn),
        grid=(m // TM, nn),
        in_specs=[row_spec, row_spec] + [w_spec] * len(ws) + [ln_spec, ln_spec],
        out_specs=row_spec,
        out_shape=jax.ShapeDtypeStruct((m, D_MODEL), F32),
        scratch_shapes=[pltpu.VMEM((nn - 1, TM, TN), F32)],
        compiler_params=pltpu.CompilerParams(
            dimension_semantics=("parallel", "arbitrary"), vmem_limit_bytes=VMEM_LIMIT),
        name="glu_ln" if gated else "out_ln",
    )(x, z, *ws, ln_g, ln_b)


def _in_proj_kernel(x_ref, w_ref, wg_ref, p_ref, gt_ref, xb_ref):
    n = pl.program_id(1)

    @pl.when(n == 0)
    def _():
        xb = x_ref[...].astype(BF16)
        xb_ref[...] = xb
        gt_ref[...] = _dot(xb, wg_ref[...])

    p = _dot(xb_ref[...], w_ref[...])
    for s in range(TN_IN // LANE):
        p_ref[s] = p[:, s * LANE:(s + 1) * LANE]


def _in_proj(x, w_in, w_gate, lj):
    m = x.shape[0]
    slabs = TN_IN // LANE
    return pl.pallas_call(
        _in_proj_kernel,
        grid=(m // TM_IN, ML_QKVO // TN_IN),
        in_specs=[
            pl.BlockSpec((TM_IN, D_MODEL), lambda i, n: (i, 0)),
            pl.BlockSpec((None, D_MODEL, TN_IN), lambda i, n: (lj, 0, n)),
            pl.BlockSpec((None, D_MODEL, LANE), lambda i, n: (lj, 0, 0)),
        ],
        out_specs=[
            pl.BlockSpec((slabs, TM_IN, LANE), lambda i, n: (n, i, 0)),
            pl.BlockSpec((TM_IN, LANE), lambda i, n: (i, 0)),
        ],
        out_shape=[jax.ShapeDtypeStruct((ML_QKVO // LANE, m, LANE), F32), jax.ShapeDtypeStruct((m, LANE), F32)],
        scratch_shapes=[pltpu.VMEM((TM_IN, D_MODEL), BF16)],
        compiler_params=pltpu.CompilerParams(
            dimension_semantics=("parallel", "arbitrary"), vmem_limit_bytes=VMEM_LIMIT),
        name="ml_in_proj",
    )(x, w_in, w_gate)


def _s5_disc_kernel(ar_ref, ai_ref, ldt_ref, br_ref, bi_ref, abr_ref, abi_ref, bbr_ref, bbi_ref):
    ar = ar_ref[...]
    ai = ai_ref[...]
    dt = jnp.exp(ldt_ref[...])
    mag = jnp.exp(dt * ar)
    abr = mag * jnp.cos(dt * ai)
    abi = mag * jnp.sin(dt * ai)
    zr = abr - 1.0
    zi = abi
    den = ar * ar + ai * ai
    gr = (zr * ar + zi * ai) / den
    gi = (zi * ar - zr * ai) / den
    br = br_ref[...]
    bi = bi_ref[...]
    abr_ref[...] = abr
    abi_ref[...] = abi
    bbr_ref[...] = gr * br - gi * bi
    bbi_ref[...] = gr * bi + gi * br


def _s5_discretise(a_re, a_im, log_dt, b_re, b_im):
    g, p, c = S5_GROUPS, S5_STATE, S5_GROUP
    outs = pl.pallas_call(
        _s5_disc_kernel,
        out_shape=[jax.ShapeDtypeStruct((g, 1, p), F32), jax.ShapeDtypeStruct((g, 1, p), F32),
                   jax.ShapeDtypeStruct((g, c, p), F32), jax.ShapeDtypeStruct((g, c, p), F32)],
        name="s5_discretise",
    )(a_re.reshape(g, 1, p), a_im.reshape(g, 1, p), log_dt.reshape(g, 1, 1),
      jnp.swapaxes(b_re, 1, 2), jnp.swapaxes(b_im, 1, 2))
    abr, abi, bbr, bbi = outs
    return abr.reshape(1, g * p), abi.reshape(1, g * p), bbr, bbi


def _block_diag_in(bb):
    eye = jnp.eye(S5_OCT, dtype=F32)
    w = bb.reshape(S5_NOCT, S5_OCT, S5_GROUP, 1, S5_STATE) * eye[None, :, None, :, None]
    return w.reshape(S5_NOCT, S5_OCT_CH, S5_OCT_ST).astype(BF16)


def _block_diag_out(c):
    eye = jnp.eye(S5_OCT, dtype=F32)
    cx = jnp.swapaxes(c.reshape(S5_NOCT, S5_OCT, S5_GROUP, S5_STATE), 2, 3)
    w = cx[:, :, :, None, :] * eye[None, :, None, :, None]
    return w.reshape(S5_NOCT, S5_OCT_ST, S5_OCT_CH).astype(BF16)


def _s5_seq_kernel(*refs, nsteps, nchunks):
    u_refs = refs[:S5_PAIR]
    (h0_ref, a1_ref, a2_ref, wb_ref, wc_ref, d_ref, z_ref, hf_ref,
     lhs_l, lhs_r, x8_s, zl_s, zr_s, zo_s, st_s) = refs[S5_PAIR:]
    c = pl.program_id(1)
    ch, st = S5_OCT_CH, S5_OCT_ST

    def seq_steps(b, stride, half=0):
        return pl.ds(half * BATCH + b, nsteps, stride=stride)

    @pl.when(c == 0)
    def _():
        st_s[...] = h0_ref[...]
        lhs_l[...] = jnp.zeros_like(lhs_l)
        lhs_r[...] = jnp.zeros_like(lhs_r)

    for o in range(S5_PAIR):
        for b in range(BATCH):
            ub = u_refs[o][seq_steps(b, BATCH), :]
            lhs_l[o, seq_steps(b, 8, 0), :] = ub
            lhs_r[o, seq_steps(b, 8, 1), :] = ub
        lhs = jnp.concatenate([lhs_l[o], lhs_r[o]], axis=1).astype(BF16)
        x8_s[:, o * st:(o + 1) * st] = _dot(lhs, wb_ref[o])

    a1 = a1_ref[...]
    a2 = a2_ref[...]

    def step(t, h):
        r = pl.multiple_of(t * 8, 8)
        hn = (a1 * h + x8_s[pl.ds(r, 8), :]) + a2 * pltpu.roll(h, BATCH, 0)
        x8_s[pl.ds(r, 8), :] = hn
        return hn

    h_last = lax.fori_loop(0, nsteps, step, st_s[...], unroll=4)
    st_s[...] = h_last

    for o in range(S5_PAIR):
        zz = _dot(x8_s[:, o * st:(o + 1) * st].astype(BF16), wc_ref[o])
        zl_s[...] = zz[:, :ch]
        zr_s[...] = zz[:, ch:]
        for b in range(BATCH):
            y = zl_s[seq_steps(b, 8, 0), :] + zr_s[seq_steps(b, 8, 1), :]
            y = y + d_ref[:, o * ch:(o + 1) * ch] * u_refs[o][seq_steps(b, BATCH), :]
            zo_s[seq_steps(b, BATCH), :] = jax.nn.gelu(y)
        z_ref[:, o * ch:(o + 1) * ch] = zo_s[...].astype(BF16)

    @pl.when(c == nchunks - 1)
    def _():
        hf_ref[...] = h_last


def _s5_scan_prompt(x_all, h0, abr, abi, wbr, wbi, wcr, wci, d_row):
    rows = S5_STEPS * BATCH
    nchunks = SEQ // S5_STEPS
    pc, ps = S5_PAIR * S5_OCT_CH, S5_PAIR * S5_OCT_ST
    a1 = jnp.broadcast_to(abr, (8, S5_LANES))
    a2 = jnp.concatenate([jnp.broadcast_to(-abi, (BATCH, S5_LANES)), jnp.broadcast_to(abi, (BATCH, S5_LANES))])
    wb = jnp.concatenate([wbr, wbi], axis=1)
    wc = jnp.concatenate([wcr, -wci], axis=2)
    st_spec = pl.BlockSpec((8, ps), lambda p, c: (0, p))
    return pl.pallas_call(
        functools.partial(_s5_seq_kernel, nsteps=S5_STEPS, nchunks=nchunks),
        grid=(S5_NOCT // S5_PAIR, nchunks),
        in_specs=[pl.BlockSpec((rows, S5_OCT_CH), lambda p, c, o=o: (c, S5_PAIR * p + o)) for o in range(S5_PAIR)] + [
            st_spec, st_spec, st_spec,
            pl.BlockSpec((S5_PAIR, 2 * S5_OCT_CH, S5_OCT_ST), lambda p, c: (p, 0, 0)),
            pl.BlockSpec((S5_PAIR, S5_OCT_ST, 2 * S5_OCT_CH), lambda p, c: (p, 0, 0)),
            pl.BlockSpec((1, pc), lambda p, c: (0, p)),
        ],
        out_specs=[pl.BlockSpec((rows, pc), lambda p, c: (c, p)), st_spec],
        out_shape=[jax.ShapeDtypeStruct((ROWS_ALL, D_MODEL), BF16),
                   jax.ShapeDtypeStruct((8, S5_LANES), F32)],
        scratch_shapes=[pltpu.VMEM((S5_PAIR, 2 * rows, S5_OCT_CH), F32), pltpu.VMEM((S5_PAIR, 2 * rows, S5_OCT_CH), F32),
                        pltpu.VMEM((2 * rows, ps), F32),
                        pltpu.VMEM((2 * rows, S5_OCT_CH), F32), pltpu.VMEM((2 * rows, S5_OCT_CH), F32),
                        pltpu.VMEM((rows, S5_OCT_CH), F32), pltpu.VMEM((8, ps), F32)],
        compiler_params=pltpu.CompilerParams(
            dimension_semantics=("parallel", "arbitrary"), vmem_limit_bytes=VMEM_LIMIT),
        name="s5_scan_prompt",
    )(*([x_all] * S5_PAIR), h0, a1, a2, wb, wc, d_row)


def _s5_step_kernel(u_ref, h0r_ref, h0i_ref, ar_ref, ai_ref, wbr_ref, wbi_ref, wcr_ref, wci_ref, d_ref,
                    z_all_ref, z_ref, hr_ref, hi_ref):
    del z_all_ref
    u = u_ref[...]
    ub = u.astype(BF16)
    xr = _dot(ub, wbr_ref[...])
    xi = _dot(ub, wbi_ref[...])
    ar = ar_ref[...]
    ai = ai_ref[...]
    h0r = h0r_ref[...]
    h0i = h0i_ref[...]
    hr = xr + (ar * h0r - ai * h0i)
    hi = xi + (ar * h0i + ai * h0r)
    hr_ref[...] = hr
    hi_ref[...] = hi
    y = _dot(hr.astype(BF16), wcr_ref[...]) - _dot(hi.astype(BF16), wci_ref[...])
    y = y + d_ref[...] * u
    z_ref[...] = jax.nn.gelu(y).astype(BF16)


def _s5_step_sample(x_all, h0r, h0i, lj, abr, abi, wbr, wbi, wcr, wci, d_row, z_all):
    row_blk = ROWS_PROMPT // DEC_BATCH
    oct_w = pl.BlockSpec((None, S5_OCT_CH, S5_OCT_ST), lambda o: (o, 0, 0))
    oct_c = pl.BlockSpec((None, S5_OCT_ST, S5_OCT_CH), lambda o: (o, 0, 0))
    st_in = pl.BlockSpec((None, DEC_BATCH, S5_OCT_ST), lambda o: (lj, 0, o))
    st_out = pl.BlockSpec((DEC_BATCH, S5_OCT_ST), lambda o: (0, o))
    return pl.pallas_call(
        _s5_step_kernel,
        grid=(S5_NOCT,),
        in_specs=[
            pl.BlockSpec((DEC_BATCH, S5_OCT_CH), lambda o: (row_blk, o)),
            st_in, st_in,
            pl.BlockSpec((1, S5_OCT_ST), lambda o: (0, o)),
            pl.BlockSpec((1, S5_OCT_ST), lambda o: (0, o)),
            oct_w, oct_w, oct_c, oct_c,
            pl.BlockSpec((1, S5_OCT_CH), lambda o: (0, o)),
            pl.BlockSpec(memory_space=pl.ANY),
        ],
        out_specs=[pl.BlockSpec((DEC_BATCH, S5_OCT_CH), lambda o: (row_blk, o)), st_out, st_out],
        out_shape=[jax.ShapeDtypeStruct((ROWS_ALL, D_MODEL), BF16),
                   jax.ShapeDtypeStruct((DEC_BATCH, S5_LANES), F32),
                   jax.ShapeDtypeStruct((DEC_BATCH, S5_LANES), F32)],
        input_output_aliases={10: 0},
        compiler_params=pltpu.CompilerParams(
            dimension_semantics=("parallel",), vmem_limit_bytes=VMEM_LIMIT),
        name="s5_step_sample",
    )(x_all, h0r, h0i, abr, abi, wbr, wbi, wcr, wci, d_row, z_all)


def _mlstm_seq_kernel(q_ref, k_ref, v_ref, o_ref, gt_ref, bias_ref, ng_ref,
                      hn_ref, cf_ref, nf_ref, mf_ref, c_s, n_s, m_s, hn_s, *, nchunks):
    ci = pl.program_id(0)
    b = pl.program_id(1)
    L = ML_CHUNK
    rows = pl.ds(b, L, stride=BATCH)

    def seq_rows(ref, h, width):
        per_head = width // LANE
        return jnp.concatenate([ref[h * per_head + j, rows, :] for j in range(per_head)], axis=1)

    @pl.when(ci == 0)
    def _():
        c_s[b] = jnp.zeros(c_s.shape[1:], F32)
        n_s[b] = jnp.zeros(n_s.shape[1:], F32)
        m_s[b] = jnp.zeros(m_s.shape[1:], F32)

    gb = gt_ref[rows, :] + bias_ref[...]
    lfa = _log_sigmoid(gb)
    gbt = gb.T
    lft = lfa.T
    row = lax.broadcasted_iota(jnp.int32, (L, L), 0)
    col = lax.broadcasted_iota(jnp.int32, (L, L), 1)
    causal = col <= row
    tril = causal.astype(F32)
    triu = (row <= col).astype(F32)
    bc_cols = jnp.dot(tril, lfa, precision=lax.Precision.HIGHEST, preferred_element_type=F32)
    bc_rows = jnp.dot(lft[0:8, :], triu, precision=lax.Precision.HIGHEST, preferred_element_type=F32)

    for h in range(ML_HEADS):
        ig_row = gbt[h:h + 1, :]
        bc_row = bc_rows[ML_HEADS + h:ML_HEADS + h + 1, :]
        ig_col = gb[:, h:h + 1]
        bc_col = bc_cols[:, ML_HEADS + h:ML_HEADS + h + 1]
        m_prev = m_s[b, h:h + 1, 0:1]
        inter = bc_col + m_prev
        dmat = jnp.where(causal, bc_col - bc_row + ig_row, -jnp.inf)
        mt = jnp.maximum(inter, jnp.max(dmat, axis=1, keepdims=True))
        wts = jnp.exp(dmat - mt)

        qf = seq_rows(q_ref, h, ML_DK)
        kf = seq_rows(k_ref, h, ML_DK) * K_SCALE
        qb = qf.astype(BF16)
        vb = seq_rows(v_ref, h, ML_DV).astype(BF16)
        qk = lax.dot_general(qb, kf.astype(BF16), (((1,), (1,)), ((), ())), preferred_element_type=F32)
        sc = qk * wts
        scale = jnp.exp(inter - mt)
        cmat = c_s[b, h]
        n_row = n_s[b, h:h + 1, :]
        num = scale * _dot(qb, cmat.astype(BF16)) + _dot(sc.astype(BF16), vb)
        qn = jnp.sum(qb.astype(F32) * n_row, axis=1, keepdims=True)
        den = scale * qn + jnp.sum(sc, axis=1, keepdims=True)
        hh = num * (1.0 / jnp.maximum(jnp.abs(den), jnp.exp(-mt)))

        mu = jnp.mean(hh, axis=1, keepdims=True)
        hc = hh - mu
        var = jnp.mean(hc * hc, axis=1, keepdims=True)
        hn = hc * lax.rsqrt(var + LN_EPS) * ng_ref[:, h * ML_DV:(h + 1) * ML_DV]
        gated = jax.nn.sigmoid(seq_rows(o_ref, h, ML_DV)) * hn
        for j in range(ML_DV // LANE):
            hn_s[h * (ML_DV // LANE) + j, rows, :] = gated[:, j * LANE:(j + 1) * LANE]

        m_last = mt[L - 1:L, :]
        b_last = bc_col[L - 1:L, :]
        dec = jnp.exp(b_last - bc_col + ig_col - m_last)
        cscale = jnp.exp(b_last + m_prev - m_last)
        kd = kf * dec
        c_new = cscale * cmat + _dot(kd.T.astype(BF16), vb)
        n_new = cscale * n_row + jnp.sum(kd, axis=0, keepdims=True)
        c_s[b, h] = c_new
        n_s[b, h:h + 1, :] = n_new
        m_s[b, h:h + 1, :] = jnp.broadcast_to(m_last, (1, LANE))

    @pl.when(ci == nchunks - 1)
    def _():
        cf_ref[...] = c_s[b]
        nf_ref[...] = n_s[b, 0:ML_HEADS, :]
        mf_ref[...] = m_s[b]

    @pl.when(b == BATCH - 1)
    def _():
        for s in range(ML_HEADS * ML_DV // LANE):
            hn_ref[:, s * LANE:(s + 1) * LANE] = hn_s[s].astype(BF16)


def _mlstm_prompt(proj, gates, bias_row, norm_g, lj):
    nchunks = SEQ // ML_CHUNK
    rows = ML_CHUNK * BATCH
    qk_s = ML_HEADS * ML_DK // LANE
    v_s = ML_HEADS * ML_DV // LANE
    v_w = ML_HEADS * ML_DV
    last = nchunks - 1

    def state_idx(c, b):
        return jnp.where(c == last, b, 0)

    return pl.pallas_call(
        functools.partial(_mlstm_seq_kernel, nchunks=nchunks),
        grid=(nchunks, BATCH),
        in_specs=[
            pl.BlockSpec((qk_s, rows, LANE), lambda c, b: (0, c, 0)),
            pl.BlockSpec((qk_s, rows, LANE), lambda c, b: (1, c, 0)),
            pl.BlockSpec((v_s, rows, LANE), lambda c, b: (1, c, 0)),
            pl.BlockSpec((v_s, rows, LANE), lambda c, b: (2, c, 0)),
            pl.BlockSpec((rows, LANE), lambda c, b: (c, 0)),
            pl.BlockSpec((1, LANE), lambda c, b: (0, 0)),
            pl.BlockSpec((None, 1, v_w), lambda c, b: (lj, 0, 0)),
        ],
        out_specs=[
            pl.BlockSpec((rows, v_w), lambda c, b: (c, 0)),
            pl.BlockSpec((None, ML_HEADS, ML_DK, ML_DV), lambda c, b: (state_idx(c, b), 0, 0, 0)),
            pl.BlockSpec((None, ML_HEADS, ML_DK), lambda c, b: (state_idx(c, b), 0, 0)),
            pl.BlockSpec((None, 8, LANE), lambda c, b: (state_idx(c, b), 0, 0)),
        ],
        out_shape=[
            jax.ShapeDtypeStruct((ROWS_ALL, v_w), BF16),
            jax.ShapeDtypeStruct((BATCH, ML_HEADS, ML_DK, ML_DV), F32),
            jax.ShapeDtypeStruct((BATCH, ML_HEADS, ML_DK), F32),
            jax.ShapeDtypeStruct((BATCH, 8, LANE), F32),
        ],
        scratch_shapes=[pltpu.VMEM((BATCH, ML_HEADS, ML_DK, ML_DV), F32), pltpu.VMEM((BATCH, 8, ML_DK), F32),
                        pltpu.VMEM((BATCH, 8, LANE), F32), pltpu.VMEM((v_s, rows, LANE), F32)],
        compiler_params=pltpu.CompilerParams(
            dimension_semantics=("arbitrary", "arbitrary"), vmem_limit_bytes=VMEM_LIMIT),
        name="mlstm_prompt",
    )(proj, proj, proj, proj, gates, bias_row, norm_g)


def _mlstm_step_kernel(q_ref, k_ref, v_ref, o_ref, ka_ref, va_ref, gt_ref, bias_ref, ng_ref,
                       c_ref, n_ref, m_ref, hn_ref, co_ref, no_ref, mo_ref, kt_s, num_s):
    h = pl.program_id(0)
    i = pl.program_id(1)
    bb = ML_DEC_BB

    def cols(ref):
        return jnp.concatenate([ref[s] for s in range(ref.shape[0])], axis=1)

    @pl.when(i == 0)
    def _():
        kt_s[...] = (cols(ka_ref) * K_SCALE).T

    gb = gt_ref[...] + bias_ref[...]
    lane = lax.broadcasted_iota(jnp.int32, (bb, LANE), 1)
    ig = jnp.sum(jnp.where(lane == h, gb, 0.0), axis=1, keepdims=True)
    fp = jnp.sum(jnp.where(lane == h + ML_HEADS, gb, 0.0), axis=1, keepdims=True)
    lf = _log_sigmoid(fp)
    m4 = m_ref[...]
    hl = lax.broadcasted_iota(jnp.int32, (bb, ML_HEADS), 1)
    m_prev = jnp.sum(jnp.where(hl == h, m4, 0.0), axis=1, keepdims=True)
    inter = lf + m_prev
    mt = jnp.maximum(inter, ig)
    wts = jnp.exp(ig - mt)
    scale = jnp.exp(inter - mt)

    qb = cols(q_ref).astype(BF16)
    qf = qb.astype(F32)
    kf = cols(k_ref) * K_SCALE
    vf = cols(v_ref).astype(BF16).astype(F32)
    n_prev = n_ref[...]
    qk = jnp.sum(qf * kf.astype(BF16).astype(F32), axis=1, keepdims=True)
    sc = qk * wts
    qn = jnp.sum(qf * n_prev, axis=1, keepdims=True)
    den = scale * qn + sc
    rden = 1.0 / jnp.maximum(jnp.abs(den), jnp.exp(-mt))

    ktb = kt_s[...].astype(BF16)
    va = cols(va_ref)
    rowi = lax.broadcasted_iota(jnp.int32, (DEC_BATCH, ML_DV), 0)
    for j in range(bb):
        cj = c_ref[j]
        qc = _dot(qb, cj.astype(BF16))[j:j + 1, :]
        vsel = jnp.where(rowi == i * bb + j, va, 0.0).astype(BF16)
        outer = _dot(ktb, vsel)
        co_ref[j] = scale[j:j + 1, :] * cj + wts[j:j + 1, :] * outer
        num_s[j:j + 1, :] = scale[j:j + 1, :] * qc + sc[j:j + 1, :] * vf[j:j + 1, :]

    hh = num_s[...] * rden
    mu = jnp.mean(hh, axis=1, keepdims=True)
    hc = hh - mu
    var = jnp.mean(hc * hc, axis=1, keepdims=True)
    hn = hc * lax.rsqrt(var + LN_EPS) * ng_ref[...]
    hn_ref[...] = (jax.nn.sigmoid(cols(o_ref)) * hn).astype(BF16)
    no_ref[...] = scale * n_prev + wts * kf
    mo_ref[...] = jnp.broadcast_to(mt, (bb, LANE))


def _mlstm_sample(proj, gates, bias_row, norm_g, c_in, n_in, m_in, hn_all, c_acc, lj):
    bb = ML_DEC_BB
    rb = ROWS_PROMPT // bb
    r128 = ROWS_PROMPT // DEC_BATCH
    nh = ML_HEADS
    ks = ML_DK // LANE
    vs = ML_DV // LANE
    in_specs = [
        pl.BlockSpec((ks, bb, LANE), lambda h, i: (h, rb + i, 0)),
        pl.BlockSpec((ks, bb, LANE), lambda h, i: (nh + h, rb + i, 0)),
        pl.BlockSpec((vs, bb, LANE), lambda h, i: (nh + h, rb + i, 0)),
        pl.BlockSpec((vs, bb, LANE), lambda h, i: (2 * nh + h, rb + i, 0)),
        pl.BlockSpec((ks, DEC_BATCH, LANE), lambda h, i: (nh + h, r128, 0)),
        pl.BlockSpec((vs, DEC_BATCH, LANE), lambda h, i: (nh + h, r128, 0)),
        pl.BlockSpec((bb, LANE), lambda h, i: (rb + i, 0)),
        pl.BlockSpec((1, LANE), lambda h, i: (0, 0)),
        pl.BlockSpec((None, 1, ML_DV), lambda h, i: (lj, 0, h)),
        pl.BlockSpec((None, bb, None, ML_DK, ML_DV), lambda h, i: (lj, i, h, 0, 0)),
        pl.BlockSpec((None, bb, ML_DK), lambda h, i: (lj, i, h)),
        pl.BlockSpec((None, bb, nh), lambda h, i: (lj, i, 0)),
        pl.BlockSpec(memory_space=pl.ANY),
    ]
    args = [proj, proj, proj, proj, proj, proj, gates, bias_row, norm_g, c_in, n_in, m_in, hn_all]
    n_blocked = 12
    aliases = {n_blocked: 0}
    if c_acc is not None:
        in_specs.append(pl.BlockSpec(memory_space=pl.ANY))
        args.append(c_acc)
        aliases[n_blocked + 1] = 1

    def body(*refs):
        _mlstm_step_kernel(*refs[:n_blocked], *refs[len(args):])

    return pl.pallas_call(
        body,
        grid=(nh, DEC_BATCH // bb),
        in_specs=in_specs,
        out_specs=[
            pl.BlockSpec((bb, ML_DV), lambda h, i: (rb + i, h)),
            pl.BlockSpec((None, bb, None, ML_DK, ML_DV), lambda h, i: (lj, i, h, 0, 0)),
            pl.BlockSpec((bb, ML_DK), lambda h, i: (i, h)),
            pl.BlockSpec((None, bb, LANE), lambda h, i: (h, i, 0)),
        ],
        out_shape=[
            jax.ShapeDtypeStruct((ROWS_ALL, nh * ML_DV), BF16),
            jax.ShapeDtypeStruct(c_in.shape, F32),
            jax.ShapeDtypeStruct((DEC_BATCH, nh * ML_DK), F32),
            jax.ShapeDtypeStruct((nh, DEC_BATCH, LANE), F32),
        ],
        scratch_shapes=[pltpu.VMEM((ML_DK, DEC_BATCH), F32), pltpu.VMEM((bb, ML_DV), F32)],
        input_output_aliases=aliases,
        compiler_params=pltpu.CompilerParams(
            dimension_semantics=("arbitrary", "arbitrary"), vmem_limit_bytes=VMEM_LIMIT),
        name="mlstm_step_sample",
    )(*args)


def kernel(x_prompt, x_sample, state_s5_re, state_s5_im, state_mlstm_C, state_mlstm_n, state_mlstm_m,
           ln_g, ln_b, ffn_w_gate, ffn_w_up, ffn_w_down,
           s5_a_re, s5_a_im, s5_log_dt, s5_b_re, s5_b_im, s5_c_re, s5_c_im, s5_d, s5_w_a, s5_w_b,
           ml_w_in, ml_b_i, ml_b_f, ml_norm_g, ml_w_out):
    n_s5 = s5_a_re.shape[0]
    n_ml = ml_w_in.shape[0]

    ffn_f32 = (ffn_w_gate, ffn_w_up, ffn_w_down)
    ffn_w = tuple(w[0, 0].astype(BF16) for w in ffn_f32)
    glu_a = s5_w_a.astype(BF16)
    glu_b = s5_w_b.astype(BF16)
    w_in = ml_w_in.astype(BF16)
    w_in_gate = jnp.pad(ml_w_in[:, :, ML_QKVO:].astype(BF16), ((0, 0), (0, 0), (0, LANE - 2 * ML_HEADS)))
    w_out = ml_w_out.astype(BF16)
    ln_g4 = ln_g.reshape(DEPTH, 3, 1, D_MODEL)
    ln_b4 = ln_b.reshape(DEPTH, 3, 1, D_MODEL)
    ml_bias = jnp.pad(jnp.concatenate([ml_b_i, ml_b_f], axis=1), ((0, 0), (0, LANE - 2 * ML_HEADS)))
    ml_ng = ml_norm_g.reshape(n_ml, 1, ML_HEADS * ML_DV)
    st_r = state_s5_re.reshape(n_s5, DEC_BATCH, S5_LANES)
    st_i = state_s5_im.reshape(n_s5, DEC_BATCH, S5_LANES)
    st_n = state_mlstm_n.reshape(n_ml, DEC_BATCH, ML_HEADS * ML_DK)

    x = _to_rows(x_prompt, x_sample.reshape(DEC_BATCH, D_MODEL))

    p_s5r, p_s5i, s_s5r, s_s5i = [], [], [], []
    p_c, p_n, p_m, s_n, s_m = [], [], [], [], []
    s_c = None
    zero_state = jnp.zeros((8, S5_LANES), F32)
    for li in range(DEPTH):
        x, ffn_w = _ffn_ln(x, ffn_w, ln_g4, ln_b4, li, 0, ffn_f32, (li, 1))
        j = li // 2
        if li % 2 == 0:
            abr, abi, bbr, bbi = _s5_discretise(s5_a_re[j], s5_a_im[j], s5_log_dt[j], s5_b_re[j], s5_b_im[j])
            wbr, wbi = _block_diag_in(bbr), _block_diag_in(bbi)
            wcr, wci = _block_diag_out(s5_c_re[j]), _block_diag_out(s5_c_im[j])
            d_row = s5_d[j].reshape(1, D_MODEL)
            z, hp = _s5_scan_prompt(x, zero_state, abr, abi, wbr, wbi, wcr, wci, d_row)
            z, hsr, hsi = _s5_step_sample(x, st_r, st_i, j, abr, abi, wbr, wbi, wcr, wci, d_row, z)
            p_s5r.append(hp[:BATCH].reshape(BATCH, S5_GROUPS, S5_STATE))
            p_s5i.append(hp[BATCH:].reshape(BATCH, S5_GROUPS, S5_STATE))
            s_s5r.append(hsr.reshape(DEC_BATCH, S5_GROUPS, S5_STATE))
            s_s5i.append(hsi.reshape(DEC_BATCH, S5_GROUPS, S5_STATE))
            x = _proj_ln(x, z, (glu_a, glu_b), ln_g4, ln_b4, j, li)
        else:
            proj, gates = _in_proj(x, w_in, w_in_gate, j)
            bias_row = ml_bias[j].reshape(1, LANE)
            z, cp, npr, mp = _mlstm_prompt(proj, gates, bias_row, ml_ng, j)
            z, s_c, ns, ms = _mlstm_sample(proj, gates, bias_row, ml_ng, state_mlstm_C, st_n,
                                           state_mlstm_m, z, s_c, j)
            p_c.append(cp)
            p_n.append(npr)
            p_m.append(mp[:, :ML_HEADS, 0])
            s_n.append(ns.reshape(DEC_BATCH, ML_HEADS, ML_DK))
            s_m.append(jnp.swapaxes(ms[:, :, 0], 0, 1))
            x = _proj_ln(x, z, (w_out,), ln_g4, ln_b4, j, li)
        x, ffn_w = _ffn_ln(x, ffn_w, ln_g4, ln_b4, li, 2, ffn_f32, (li + 1, 0) if li + 1 < DEPTH else None)

    y_prompt, y_sample = _from_rows(x)
    y_sample = y_sample.reshape(DEC_BATCH, 1, D_MODEL)
    return (y_prompt, y_sample,
            jnp.stack(p_s5r), jnp.stack(p_s5i), jnp.stack(p_c), jnp.stack(p_n), jnp.stack(p_m),
            jnp.stack(s_s5r), jnp.stack(s_s5i), s_c, jnp.stack(s_n), jnp.stack(s_m))
```

```python
import functools

import jax
import jax.numpy as jnp
from jax import lax
from jax.experimental import pallas as pl
from jax.experimental.pallas import tpu as pltpu

F32 = jnp.float32
BF16 = jnp.bfloat16

D_MODEL = 2048
BATCH = 4
SEQ = 2048
DEPTH = 4
DEC_BATCH = 128
D_FF = 5504
S5_GROUPS = 128
S5_GROUP = 16
S5_STATE = 64
S5_LANES = S5_GROUPS * S5_STATE
ML_HEADS = 4
ML_DK = 256
ML_DV = 512
ML_CHUNK = 128
ML_QKVO = 2 * ML_HEADS * ML_DK + 2 * ML_HEADS * ML_DV
DN_ALPHA = (2.0 * DEPTH) ** 0.25
LN_EPS = 1e-5
K_SCALE = ML_DK ** -0.5

LANE = 128
ROWS_PROMPT = SEQ * BATCH
ROWS_ALL = ROWS_PROMPT + DEC_BATCH
TM = 640
TM_FF = 832
TF = 512
CV_BLOCKS = 8
TN = 512
PROJ_LN_GROUPS = 2
TM_IN = 1040
TN_IN = 1024
S5_OCT = 8
S5_NOCT = S5_GROUPS // S5_OCT
S5_OCT_CH = S5_OCT * S5_GROUP
S5_OCT_ST = S5_OCT * S5_STATE
S5_STEPS = 256
S5_PAIR = 2
RL_STEPS = 1024
ML_DEC_BB = 16
VMEM_LIMIT = 56 * 1024 * 1024


def _dot(a, b):
    return jnp.dot(a, b, preferred_element_type=F32)


def _layer_norm_rows(y, g, b):
    mu = jnp.mean(y, axis=-1, keepdims=True)
    yc = y - mu
    var = jnp.mean(yc * yc, axis=-1, keepdims=True)
    return yc * lax.rsqrt(var + LN_EPS) * g + b


def _row_groups(n_rows, n_groups):
    step = -(-n_rows // n_groups)
    step = -(-step // 16) * 16
    return [slice(lo, min(lo + step, n_rows)) for lo in range(0, n_rows, step)]


def _log_sigmoid(x):
    return jnp.minimum(x, 0.0) - jnp.log1p(jnp.exp(-jnp.abs(x)))


def _to_rows_kernel(xp_ref, xs_ref, o_ref, *, nt):
    r = pl.program_id(0)

    @pl.when(r < nt)
    def _():
        for b in range(BATCH):
            o_ref[pl.ds(b, RL_STEPS, stride=BATCH), :] = xp_ref[b]

    @pl.when(r == nt)
    def _():
        o_ref[0:DEC_BATCH, :] = xs_ref[...]


def _to_rows(x_prompt, x_sample):
    nt = SEQ // RL_STEPS
    ns = D_MODEL // LANE
    return pl.pallas_call(
        functools.partial(_to_rows_kernel, nt=nt),
        grid=(nt + 1, ns),
        in_specs=[
            pl.BlockSpec((BATCH, RL_STEPS, LANE),
                         lambda r, s: (0, jnp.minimum(r, nt - 1), jnp.where(r == nt, ns - 1, s))),
            pl.BlockSpec((DEC_BATCH, LANE), lambda r, s: (0, s)),
        ],
        out_specs=pl.BlockSpec((RL_STEPS * BATCH, LANE), lambda r, s: (r, s)),
        out_shape=jax.ShapeDtypeStruct((ROWS_ALL, D_MODEL), F32),
        compiler_params=pltpu.CompilerParams(
            dimension_semantics=("arbitrary", "arbitrary"), vmem_limit_bytes=VMEM_LIMIT),
        name="to_rows",
    )(x_prompt, x_sample)


def _from_rows_kernel(x_ref, yp_ref, ys_ref, *, nt):
    r = pl.program_id(0)

    @pl.when(r < nt)
    def _():
        for b in range(BATCH):
            yp_ref[b] = x_ref[pl.ds(b, RL_STEPS, stride=BATCH), :]

    @pl.when(r == nt)
    def _():
        ys_ref[...] = x_ref[0:DEC_BATCH, :]


def _from_rows(x):
    nt = SEQ // RL_STEPS
    ns = D_MODEL // LANE
    return pl.pallas_call(
        functools.partial(_from_rows_kernel, nt=nt),
        grid=(nt + 1, ns),
        in_specs=[pl.BlockSpec((RL_STEPS * BATCH, LANE), lambda r, s: (r, s))],
        out_specs=[
            pl.BlockSpec((BATCH, RL_STEPS, LANE),
                         lambda r, s: (0, jnp.minimum(r, nt - 1), jnp.where(r == nt, ns - 1, s))),
            pl.BlockSpec((DEC_BATCH, LANE), lambda r, s: (0, jnp.where(r == nt, s, 0))),
        ],
        out_shape=[jax.ShapeDtypeStruct((BATCH, SEQ, D_MODEL), F32),
                   jax.ShapeDtypeStruct((DEC_BATCH, D_MODEL), F32)],
        compiler_params=pltpu.CompilerParams(
            dimension_semantics=("arbitrary", "arbitrary"), vmem_limit_bytes=VMEM_LIMIT),
        name="from_rows",
    )(x)


def _ffn_kernel(x_ref, wg_ref, wu_ref, wd_ref, g_ref, b_ref, *refs, nf, convert_next, convert_extra):
    refs = list(refs)
    xb_ref = refs.pop()
    if convert_next:
        ng_ref, nu_ref, nd_ref = refs[:3]
        del refs[:3]
    if convert_extra:
        ne_ref = refs.pop(0)
    o_ref = refs.pop(0)
    if convert_next:
        cg_ref, cu_ref, cd_ref = refs[:3]
        del refs[:3]
    if convert_extra:
        ce_ref = refs.pop(0)
    f = pl.program_id(1)

    @pl.when(f == 0)
    def _():
        x = x_ref[...]
        xb_ref[...] = x.astype(BF16)
        o_ref[...] = (2.0 * DN_ALPHA) * x

    def hidden_tile(width):
        xb = xb_ref[...]
        hg = _dot(xb, wg_ref[:, :width])
        hu = _dot(xb, wu_ref[:, :width])
        act = (hg * jax.nn.sigmoid(hg) * hu).astype(BF16)
        o_ref[...] += _dot(act, wd_ref[:width, :])
        if convert_next:
            cg_ref[:, :width] = ng_ref[:, :width].astype(BF16)
            cu_ref[:, :width] = nu_ref[:, :width].astype(BF16)
            cd_ref[:width, :] = nd_ref[:width, :].astype(BF16)
        if convert_extra:
            ce_ref[...] = ne_ref[...].astype(BF16)

    @pl.when(f < nf - 1)
    def _():
        hidden_tile(TF)

    @pl.when(f == nf - 1)
    def _():
        hidden_tile(D_FF - (nf - 1) * TF)
        o_ref[...] = _layer_norm_rows(0.5 * o_ref[...], g_ref[...], b_ref[...])


def _ffn_ln(x, w_cur, ln_g, ln_b, li, kn, w_f32, nxt, extra=None):
    m = x.shape[0]
    nf = pl.cdiv(D_FF, TF)
    n_row_tiles = m // TM_FF
    cv = D_MODEL // CV_BLOCKS
    assert n_row_tiles >= CV_BLOCKS
    in_specs = [
        pl.BlockSpec((TM_FF, D_MODEL), lambda i, f: (i, 0)),
        pl.BlockSpec((D_MODEL, TF), lambda i, f: (0, f)),
        pl.BlockSpec((D_MODEL, TF), lambda i, f: (0, f)),
        pl.BlockSpec((TF, D_MODEL), lambda i, f: (f, 0)),
        pl.BlockSpec((None, None, 1, D_MODEL), lambda i, f: (li, kn, 0, 0)),
        pl.BlockSpec((None, None, 1, D_MODEL), lambda i, f: (li, kn, 0, 0)),
    ]
    out_specs = [pl.BlockSpec((TM_FF, D_MODEL), lambda i, f: (i, 0))]
    out_shape = [jax.ShapeDtypeStruct((m, D_MODEL), F32)]
    args = [x, *w_cur, ln_g, ln_b]
    extra_in, extra_out, extra_shape = [], [], []
    if extra is not None:
        w_extra, le, cols = extra
        ncb = cols // TF
        n_blocks = CV_BLOCKS * ncb
        assert cols % TF == 0 and n_blocks <= n_row_tiles * nf

        def blk(i, f):
            return jnp.minimum(i * nf + f, n_blocks - 1)

        extra_in = [pl.BlockSpec((None, cv, TF), lambda i, f: (le, blk(i, f) // ncb, blk(i, f) % ncb))]
        extra_out = [pl.BlockSpec((cv, TF), lambda i, f: (blk(i, f) // ncb, blk(i, f) % ncb))]
        extra_shape = [jax.ShapeDtypeStruct((D_MODEL, cols), BF16)]
    if nxt is not None:
        nli, nk = nxt

        def cv_row(i, f):
            return jnp.minimum(i, CV_BLOCKS - 1)

        def cv_col(i, f):
            return jnp.where(i < CV_BLOCKS, f, nf - 1)

        in_specs += [
            pl.BlockSpec((None, None, cv, TF), lambda i, f: (nli, nk, cv_row(i, f), cv_col(i, f))),
            pl.BlockSpec((None, None, cv, TF), lambda i, f: (nli, nk, cv_row(i, f), cv_col(i, f))),
            pl.BlockSpec((None, None, TF, cv), lambda i, f: (nli, nk, cv_col(i, f), cv_row(i, f))),
        ]
        out_specs += [
            pl.BlockSpec((cv, TF), lambda i, f: (cv_row(i, f), cv_col(i, f))),
            pl.BlockSpec((cv, TF), lambda i, f: (cv_row(i, f), cv_col(i, f))),
            pl.BlockSpec((TF, cv), lambda i, f: (cv_col(i, f), cv_row(i, f))),
        ]
        out_shape += [jax.ShapeDtypeStruct((D_MODEL, D_FF), BF16), jax.ShapeDtypeStruct((D_MODEL, D_FF), BF16),
                      jax.ShapeDtypeStruct((D_FF, D_MODEL), BF16)]
        args += list(w_f32)
    if extra is not None:
        args.append(w_extra)
    outs = pl.pallas_call(
        functools.partial(_ffn_kernel, nf=nf, convert_next=nxt is not None, convert_extra=extra is not None),
        grid=(n_row_tiles, nf),
        in_specs=in_specs + extra_in,
        out_specs=out_specs + extra_out,
        out_shape=out_shape + extra_shape,
        scratch_shapes=[pltpu.VMEM((TM_FF, D_MODEL), BF16)],
        compiler_params=pltpu.CompilerParams(
            dimension_semantics=("arbitrary", "arbitrary"), vmem_limit_bytes=VMEM_LIMIT),
        name="ffn_ln",
    )(*args)
    n_next = 3 if nxt is not None else 0
    return outs[0], (tuple(outs[1:4]) if nxt is not None else None), (outs[1 + n_next] if extra is not None else None)


def _proj_ln_kernel(x_ref, z_ref, *refs, gated, nn):
    w_refs, (g_ref, b_ref, o_ref, mix_ref) = refs[:-4], refs[-4:]
    n = pl.program_id(1)

    def mix_tile(rows):
        z = z_ref[rows, :]
        p = _dot(z, w_refs[0][...])
        if gated:
            p = p * jax.nn.sigmoid(_dot(z, w_refs[1][...]))
        return p

    @pl.when(n < nn - 1)
    def _():
        mix_ref[n] = mix_tile(slice(None))

    @pl.when(n == nn - 1)
    def _():
        def finish(rows, last_tile):
            tiles = [mix_ref[j, rows, :] for j in range(nn - 1)] + [last_tile]
            ys = [DN_ALPHA * x_ref[rows, j * TN:(j + 1) * TN] + tiles[j] for j in range(nn)]
            mu = sum(jnp.sum(y, axis=-1, keepdims=True) for y in ys) * (1.0 / D_MODEL)
            var = sum(jnp.sum((y - mu) * (y - mu), axis=-1, keepdims=True) for y in ys) * (1.0 / D_MODEL)
            r = lax.rsqrt(var + LN_EPS)
            for j in range(nn):
                sl = slice(j * TN, (j + 1) * TN)
                o_ref[rows, sl] = (ys[j] - mu) * r * g_ref[:, sl] + b_ref[:, sl]

        groups = _row_groups(o_ref.shape[0], PROJ_LN_GROUPS)
        pending = None
        for rows in groups:
            tile = mix_tile(rows)
            if pending is not None:
                finish(*pending)
            pending = (rows, tile)
        finish(*pending)


def _proj_ln(x, z, ws, ln_g, ln_b, lj, li):
    m = x.shape[0]
    nn = D_MODEL // TN
    gated = len(ws) == 2
    row_spec = pl.BlockSpec((TM, D_MODEL), lambda i, n: (i, 0))
    w_spec = pl.BlockSpec((None, D_MODEL, TN), lambda i, n: (lj, 0, n))
    ln_spec = pl.BlockSpec((None, None, 1, D_MODEL), lambda i, n: (li, 1, 0, 0))
    return pl.pallas_call(
        functools.partial(_proj_ln_kernel, gated=gated, nn=nn),
        grid=(m // TM, nn),
        in_specs=[row_spec, row_spec] + [w_spec] * len(ws) + [ln_spec, ln_spec],
        out_specs=row_spec,
        out_shape=jax.ShapeDtypeStruct((m, D_MODEL), F32),
        scratch_shapes=[pltpu.VMEM((nn - 1, TM, TN), F32)],
        compiler_params=pltpu.CompilerParams(
            dimension_semantics=("parallel", "arbitrary"), vmem_limit_bytes=VMEM_LIMIT),
        name="glu_ln" if gated else "out_ln",
    )(x, z, *ws, ln_g, ln_b)


def _in_proj_kernel(x_ref, w_ref, wg_ref, p_ref, gt_ref, xb_ref):
    n = pl.program_id(1)

    @pl.when(n == 0)
    def _():
        xb = x_ref[...].astype(BF16)
        xb_ref[...] = xb
        gt_ref[...] = _dot(xb, wg_ref[...])

    p = _dot(xb_ref[...], w_ref[...])
    for s in range(TN_IN // LANE):
        p_ref[s] = p[:, s * LANE:(s + 1) * LANE]


def _in_proj(x, w_in, w_gate, lj):
    m = x.shape[0]
    slabs = TN_IN // LANE
    return pl.pallas_call(
        _in_proj_kernel,
        grid=(m // TM_IN, ML_QKVO // TN_IN),
        in_specs=[
            pl.BlockSpec((TM_IN, D_MODEL), lambda i, n: (i, 0)),
            pl.BlockSpec((D_MODEL, TN_IN), lambda i, n: (0, n)),
            pl.BlockSpec((None, D_MODEL, LANE), lambda i, n: (lj, 0, 0)),
        ],
        out_specs=[
            pl.BlockSpec((slabs, TM_IN, LANE), lambda i, n: (n, i, 0)),
            pl.BlockSpec((TM_IN, LANE), lambda i, n: (i, 0)),
        ],
        out_shape=[jax.ShapeDtypeStruct((ML_QKVO // LANE, m, LANE), F32), jax.ShapeDtypeStruct((m, LANE), F32)],
        scratch_shapes=[pltpu.VMEM((TM_IN, D_MODEL), BF16)],
        compiler_params=pltpu.CompilerParams(
            dimension_semantics=("parallel", "arbitrary"), vmem_limit_bytes=VMEM_LIMIT),
        name="ml_in_proj",
    )(x, w_in, w_gate)


def _s5_disc_kernel(ar_ref, ai_ref, ldt_ref, br_ref, bi_ref, abr_ref, abi_ref, bbr_ref, bbi_ref):
    ar = ar_ref[...]
    ai = ai_ref[...]
    dt = jnp.exp(ldt_ref[...])
    mag = jnp.exp(dt * ar)
    abr = mag * jnp.cos(dt * ai)
    abi = mag * jnp.sin(dt * ai)
    zr = abr - 1.0
    zi = abi
    den = ar * ar + ai * ai
    gr = (zr * ar + zi * ai) / den
    gi = (zi * ar - zr * ai) / den
    br = br_ref[...]
    bi = bi_ref[...]
    abr_ref[...] = abr
    abi_ref[...] = abi
    bbr_ref[...] = gr * br - gi * bi
    bbi_ref[...] = gr * bi + gi * br


def _s5_discretise(a_re, a_im, log_dt, b_re, b_im):
    g, p, c = S5_GROUPS, S5_STATE, S5_GROUP
    outs = pl.pallas_call(
        _s5_disc_kernel,
        out_shape=[jax.ShapeDtypeStruct((g, 1, p), F32), jax.ShapeDtypeStruct((g, 1, p), F32),
                   jax.ShapeDtypeStruct((g, c, p), F32), jax.ShapeDtypeStruct((g, c, p), F32)],
        name="s5_discretise",
    )(a_re.reshape(g, 1, p), a_im.reshape(g, 1, p), log_dt.reshape(g, 1, 1),
      jnp.swapaxes(b_re, 1, 2), jnp.swapaxes(b_im, 1, 2))
    return outs


def _s5_operands(abr, abi, bbr, bbi, c_re, c_im):
    eye = jnp.eye(S5_OCT, dtype=F32)
    ab = jnp.stack([abr, abi]).reshape(2, 1, S5_LANES)
    upper = (jnp.arange(8) < BATCH)[:, None]
    a1 = jnp.broadcast_to(ab[0], (8, S5_LANES))
    a2 = jnp.where(upper, -ab[1], ab[1])
    bb = jnp.stack([bbr, bbi]).reshape(2, S5_NOCT, S5_OCT, S5_GROUP, 1, S5_STATE)
    wb = jnp.moveaxis(bb * eye[None, None, :, None, :, None], 0, 1)
    wb = wb.reshape(S5_NOCT, 2 * S5_OCT_CH, S5_OCT_ST).astype(BF16)
    cc = jnp.stack([c_re, -c_im]).reshape(2, S5_NOCT, S5_OCT, S5_GROUP, S5_STATE)
    cc = jnp.transpose(cc, (1, 4, 0, 2, 3))
    wc = cc[:, None] * eye[None, :, None, None, :, None]
    wc = wc.reshape(S5_NOCT, S5_OCT_ST, 2 * S5_OCT_CH).astype(BF16)
    return a1, a2, wb, wc


def _s5_seq_kernel(*refs, nsteps, nchunks):
    u_refs = refs[:S5_PAIR]
    (h0_ref, a1_ref, a2_ref, wb_ref, wc_ref, d_ref, z_ref, hf_ref,
     lhs_l, lhs_r, x8_s, zl_s, zr_s, zo_s, st_s) = refs[S5_PAIR:]
    c = pl.program_id(1)
    ch, st = S5_OCT_CH, S5_OCT_ST

    def seq_steps(b, stride, half=0):
        return pl.ds(half * BATCH + b, nsteps, stride=stride)

    @pl.when(c == 0)
    def _():
        st_s[...] = h0_ref[...]
        lhs_l[...] = jnp.zeros_like(lhs_l)
        lhs_r[...] = jnp.zeros_like(lhs_r)

    for o in range(S5_PAIR):
        for b in range(BATCH):
            ub = u_refs[o][seq_steps(b, BATCH), :]
            lhs_l[o, seq_steps(b, 8, 0), :] = ub
            lhs_r[o, seq_steps(b, 8, 1), :] = ub
        lhs = jnp.concatenate([lhs_l[o], lhs_r[o]], axis=1).astype(BF16)
        x8_s[:, o * st:(o + 1) * st] = _dot(lhs, wb_ref[o])

    a1 = a1_ref[...]
    a2 = a2_ref[...]

    def step(t, h):
        r = pl.multiple_of(t * 8, 8)
        hn = (a1 * h + x8_s[pl.ds(r, 8), :]) + a2 * pltpu.roll(h, BATCH, 0)
        x8_s[pl.ds(r, 8), :] = hn
        return hn

    h_last = lax.fori_loop(0, nsteps, step, st_s[...], unroll=4)
    st_s[...] = h_last

    for o in range(S5_PAIR):
        zz = _dot(x8_s[:, o * st:(o + 1) * st].astype(BF16), wc_ref[o])
        zl_s[...] = zz[:, :ch]
        zr_s[...] = zz[:, ch:]
        for b in range(BATCH):
            y = zl_s[seq_steps(b, 8, 0), :] + zr_s[seq_steps(b, 8, 1), :]
            y = y + d_ref[:, o * ch:(o + 1) * ch] * u_refs[o][seq_steps(b, BATCH), :]
            zo_s[seq_steps(b, BATCH), :] = jax.nn.gelu(y)
        z_ref[:, o * ch:(o + 1) * ch] = zo_s[...].astype(BF16)

    @pl.when(c == nchunks - 1)
    def _():
        hf_ref[...] = h_last


def _s5_scan_prompt(x_all, h0, a1, a2, wb, wc, d_row):
    rows = S5_STEPS * BATCH
    nchunks = SEQ // S5_STEPS
    pc, ps = S5_PAIR * S5_OCT_CH, S5_PAIR * S5_OCT_ST
    st_spec = pl.BlockSpec((8, ps), lambda p, c: (0, p))
    return pl.pallas_call(
        functools.partial(_s5_seq_kernel, nsteps=S5_STEPS, nchunks=nchunks),
        grid=(S5_NOCT // S5_PAIR, nchunks),
        in_specs=[pl.BlockSpec((rows, S5_OCT_CH), lambda p, c, o=o: (c, S5_PAIR * p + o)) for o in range(S5_PAIR)] + [
            st_spec, st_spec, st_spec,
            pl.BlockSpec((S5_PAIR, 2 * S5_OCT_CH, S5_OCT_ST), lambda p, c: (p, 0, 0)),
            pl.BlockSpec((S5_PAIR, S5_OCT_ST, 2 * S5_OCT_CH), lambda p, c: (p, 0, 0)),
            pl.BlockSpec((1, pc), lambda p, c: (0, p)),
        ],
        out_specs=[pl.BlockSpec((rows, pc), lambda p, c: (c, p)), st_spec],
        out_shape=[jax.ShapeDtypeStruct((ROWS_ALL, D_MODEL), BF16),
                   jax.ShapeDtypeStruct((8, S5_LANES), F32)],
        scratch_shapes=[pltpu.VMEM((S5_PAIR, 2 * rows, S5_OCT_CH), F32), pltpu.VMEM((S5_PAIR, 2 * rows, S5_OCT_CH), F32),
                        pltpu.VMEM((2 * rows, ps), F32),
                        pltpu.VMEM((2 * rows, S5_OCT_CH), F32), pltpu.VMEM((2 * rows, S5_OCT_CH), F32),
                        pltpu.VMEM((rows, S5_OCT_CH), F32), pltpu.VMEM((8, ps), F32)],
        compiler_params=pltpu.CompilerParams(
            dimension_semantics=("parallel", "arbitrary"), vmem_limit_bytes=VMEM_LIMIT),
        name="s5_scan_prompt",
    )(*([x_all] * S5_PAIR), h0, a1, a2, wb, wc, d_row)


def _s5_step_kernel(u_ref, h0r_ref, h0i_ref, a1_ref, a2_ref, wb_ref, wc_ref, d_ref,
                    z_all_ref, z_ref, hr_ref, hi_ref):
    del z_all_ref
    ch = S5_OCT_CH
    u = u_ref[...]
    ub = u.astype(BF16)
    xr = _dot(ub, wb_ref[:ch, :])
    xi = _dot(ub, wb_ref[ch:, :])
    ar = a1_ref[0:1, :]
    ai = a2_ref[BATCH:BATCH + 1, :]
    h0r = h0r_ref[...]
    h0i = h0i_ref[...]
    hr = xr + (ar * h0r - ai * h0i)
    hi = xi + (ar * h0i + ai * h0r)
    hr_ref[...] = hr
    hi_ref[...] = hi
    y = _dot(hr.astype(BF16), wc_ref[:, :ch]) + _dot(hi.astype(BF16), wc_ref[:, ch:])
    y = y + d_ref[...] * u
    z_ref[...] = jax.nn.gelu(y).astype(BF16)


def _s5_step_sample(x_all, h0r, h0i, lj, a1, a2, wb, wc, d_row, z_all):
    row_blk = ROWS_PROMPT // DEC_BATCH
    st_in = pl.BlockSpec((None, DEC_BATCH, S5_OCT_ST), lambda o: (lj, 0, o))
    st_out = pl.BlockSpec((DEC_BATCH, S5_OCT_ST), lambda o: (0, o))
    return pl.pallas_call(
        _s5_step_kernel,
        grid=(S5_NOCT,),
        in_specs=[
            pl.BlockSpec((DEC_BATCH, S5_OCT_CH), lambda o: (row_blk, o)),
            st_in, st_in,
            pl.BlockSpec((8, S5_OCT_ST), lambda o: (0, o)),
            pl.BlockSpec((8, S5_OCT_ST), lambda o: (0, o)),
            pl.BlockSpec((None, 2 * S5_OCT_CH, S5_OCT_ST), lambda o: (o, 0, 0)),
            pl.BlockSpec((None, S5_OCT_ST, 2 * S5_OCT_CH), lambda o: (o, 0, 0)),
            pl.BlockSpec((1, S5_OCT_CH), lambda o: (0, o)),
            pl.BlockSpec(memory_space=pl.ANY),
        ],
        out_specs=[pl.BlockSpec((DEC_BATCH, S5_OCT_CH), lambda o: (row_blk, o)), st_out, st_out],
        out_shape=[jax.ShapeDtypeStruct((ROWS_ALL, D_MODEL), BF16),
                   jax.ShapeDtypeStruct((DEC_BATCH, S5_LANES), F32),
                   jax.ShapeDtypeStruct((DEC_BATCH, S5_LANES), F32)],
        input_output_aliases={8: 0},
        compiler_params=pltpu.CompilerParams(
            dimension_semantics=("parallel",), vmem_limit_bytes=VMEM_LIMIT),
        name="s5_step_sample",
    )(x_all, h0r, h0i, a1, a2, wb, wc, d_row, z_all)


def _mlstm_seq_kernel(q_ref, k_ref, v_ref, o_ref, gt_ref, bias_ref, ng_ref,
                      hn_ref, cf_ref, nf_ref, mf_ref, c_s, n_s, m_s, hn_s, *, nchunks):
    ci = pl.program_id(0)
    b = pl.program_id(1)
    L = ML_CHUNK
    rows = pl.ds(b, L, stride=BATCH)

    def seq_rows(ref, h, width):
        per_head = width // LANE
        return jnp.concatenate([ref[h * per_head + j, rows, :] for j in range(per_head)], axis=1)

    @pl.when(ci == 0)
    def _():
        c_s[b] = jnp.zeros(c_s.shape[1:], F32)
        n_s[b] = jnp.zeros(n_s.shape[1:], F32)
        m_s[b] = jnp.zeros(m_s.shape[1:], F32)

    gb = gt_ref[rows, :] + bias_ref[...]
    lfa = _log_sigmoid(gb)
    gbt = gb.T
    lft = lfa.T
    row = lax.broadcasted_iota(jnp.int32, (L, L), 0)
    col = lax.broadcasted_iota(jnp.int32, (L, L), 1)
    causal = col <= row
    tril = causal.astype(F32)
    triu = (row <= col).astype(F32)
    bc_cols = jnp.dot(tril, lfa, precision=lax.Precision.HIGHEST, preferred_element_type=F32)
    bc_rows = jnp.dot(lft[0:8, :], triu, precision=lax.Precision.HIGHEST, preferred_element_type=F32)

    for h in range(ML_HEADS):
        ig_row = gbt[h:h + 1, :]
        bc_row = bc_rows[ML_HEADS + h:ML_HEADS + h + 1, :]
        ig_col = gb[:, h:h + 1]
        bc_col = bc_cols[:, ML_HEADS + h:ML_HEADS + h + 1]
        m_prev = m_s[b, h:h + 1, 0:1]
        inter = bc_col + m_prev
        dmat = jnp.where(causal, bc_col - bc_row + ig_row, -jnp.inf)
        mt = jnp.maximum(inter, jnp.max(dmat, axis=1, keepdims=True))
        wts = jnp.exp(dmat - mt)

        qf = seq_rows(q_ref, h, ML_DK)
        kf = seq_rows(k_ref, h, ML_DK) * K_SCALE
        qb = qf.astype(BF16)
        vb = seq_rows(v_ref, h, ML_DV).astype(BF16)
        qk = lax.dot_general(qb, kf.astype(BF16), (((1,), (1,)), ((), ())), preferred_element_type=F32)
        sc = qk * wts
        scale = jnp.exp(inter - mt)
        cmat = c_s[b, h]
        n_row = n_s[b, h:h + 1, :]
        num = scale * _dot(qb, cmat.astype(BF16)) + _dot(sc.astype(BF16), vb)
        qn = jnp.sum(qb.astype(F32) * n_row, axis=1, keepdims=True)
        den = scale * qn + jnp.sum(sc, axis=1, keepdims=True)
        hh = num * (1.0 / jnp.maximum(jnp.abs(den), jnp.exp(-mt)))

        mu = jnp.mean(hh, axis=1, keepdims=True)
        hc = hh - mu
        var = jnp.mean(hc * hc, axis=1, keepdims=True)
        hn = hc * lax.rsqrt(var + LN_EPS) * ng_ref[:, h * ML_DV:(h + 1) * ML_DV]
        gated = jax.nn.sigmoid(seq_rows(o_ref, h, ML_DV)) * hn
        for j in range(ML_DV // LANE):
            hn_s[h * (ML_DV // LANE) + j, rows, :] = gated[:, j * LANE:(j + 1) * LANE]

        m_last = mt[L - 1:L, :]
        b_last = bc_col[L - 1:L, :]
        dec = jnp.exp(b_last - bc_col + ig_col - m_last)
        cscale = jnp.exp(b_last + m_prev - m_last)
        kd = kf * dec
        c_new = cscale * cmat + _dot(kd.T.astype(BF16), vb)
        n_new = cscale * n_row + jnp.sum(kd, axis=0, keepdims=True)
        c_s[b, h] = c_new
        n_s[b, h:h + 1, :] = n_new
        m_s[b, h:h + 1, :] = jnp.broadcast_to(m_last, (1, LANE))

    @pl.when(ci == nchunks - 1)
    def _():
        cf_ref[...] = c_s[b]
        nf_ref[...] = n_s[b, 0:ML_HEADS, :]
        mf_ref[...] = m_s[b]

    @pl.when(b == BATCH - 1)
    def _():
        for s in range(ML_HEADS * ML_DV // LANE):
            hn_ref[:, s * LANE:(s + 1) * LANE] = hn_s[s].astype(BF16)


def _mlstm_prompt(proj, gates, bias_row, norm_g, lj):
    nchunks = SEQ // ML_CHUNK
    rows = ML_CHUNK * BATCH
    qk_s = ML_HEADS * ML_DK // LANE
    v_s = ML_HEADS * ML_DV // LANE
    v_w = ML_HEADS * ML_DV
    last = nchunks - 1

    def state_idx(c, b):
        return jnp.where(c == last, b, 0)

    return pl.pallas_call(
        functools.partial(_mlstm_seq_kernel, nchunks=nchunks),
        grid=(nchunks, BATCH),
        in_specs=[
            pl.BlockSpec((qk_s, rows, LANE), lambda c, b: (0, c, 0)),
            pl.BlockSpec((qk_s, rows, LANE), lambda c, b: (1, c, 0)),
            pl.BlockSpec((v_s, rows, LANE), lambda c, b: (1, c, 0)),
            pl.BlockSpec((v_s, rows, LANE), lambda c, b: (2, c, 0)),
            pl.BlockSpec((rows, LANE), lambda c, b: (c, 0)),
            pl.BlockSpec((1, LANE), lambda c, b: (0, 0)),
            pl.BlockSpec((None, 1, v_w), lambda c, b: (lj, 0, 0)),
        ],
        out_specs=[
            pl.BlockSpec((rows, v_w), lambda c, b: (c, 0)),
            pl.BlockSpec((None, ML_HEADS, ML_DK, ML_DV), lambda c, b: (state_idx(c, b), 0, 0, 0)),
            pl.BlockSpec((None, ML_HEADS, ML_DK), lambda c, b: (state_idx(c, b), 0, 0)),
            pl.BlockSpec((None, 8, LANE), lambda c, b: (state_idx(c, b), 0, 0)),
        ],
        out_shape=[
            jax.ShapeDtypeStruct((ROWS_ALL, v_w), BF16),
            jax.ShapeDtypeStruct((BATCH, ML_HEADS, ML_DK, ML_DV), F32),
            jax.ShapeDtypeStruct((BATCH, ML_HEADS, ML_DK), F32),
            jax.ShapeDtypeStruct((BATCH, 8, LANE), F32),
        ],
        scratch_shapes=[pltpu.VMEM((BATCH, ML_HEADS, ML_DK, ML_DV), F32), pltpu.VMEM((BATCH, 8, ML_DK), F32),
                        pltpu.VMEM((BATCH, 8, LANE), F32), pltpu.VMEM((v_s, rows, LANE), F32)],
        compiler_params=pltpu.CompilerParams(
            dimension_semantics=("arbitrary", "arbitrary"), vmem_limit_bytes=VMEM_LIMIT),
        name="mlstm_prompt",
    )(proj, proj, proj, proj, gates, bias_row, norm_g)


def _mlstm_step_kernel(q_ref, k_ref, v_ref, o_ref, ka_ref, va_ref, gt_ref, bias_ref, ng_ref,
                       c_ref, n_ref, m_ref, hn_ref, co_ref, no_ref, mo_ref, kt_s, num_s):
    h = pl.program_id(0)
    i = pl.program_id(1)
    bb = ML_DEC_BB

    def cols(ref):
        return jnp.concatenate([ref[s] for s in range(ref.shape[0])], axis=1)

    @pl.when(i == 0)
    def _():
        kt_s[...] = (cols(ka_ref) * K_SCALE).T

    gb = gt_ref[...] + bias_ref[...]
    lane = lax.broadcasted_iota(jnp.int32, (bb, LANE), 1)
    ig = jnp.sum(jnp.where(lane == h, gb, 0.0), axis=1, keepdims=True)
    fp = jnp.sum(jnp.where(lane == h + ML_HEADS, gb, 0.0), axis=1, keepdims=True)
    lf = _log_sigmoid(fp)
    m4 = m_ref[...]
    hl = lax.broadcasted_iota(jnp.int32, (bb, ML_HEADS), 1)
    m_prev = jnp.sum(jnp.where(hl == h, m4, 0.0), axis=1, keepdims=True)
    inter = lf + m_prev
    mt = jnp.maximum(inter, ig)
    wts = jnp.exp(ig - mt)
    scale = jnp.exp(inter - mt)

    qb = cols(q_ref).astype(BF16)
    qf = qb.astype(F32)
    kf = cols(k_ref) * K_SCALE
    vf = cols(v_ref).astype(BF16).astype(F32)
    n_prev = n_ref[...]
    qk = jnp.sum(qf * kf.astype(BF16).astype(F32), axis=1, keepdims=True)
    sc = qk * wts
    qn = jnp.sum(qf * n_prev, axis=1, keepdims=True)
    den = scale * qn + sc
    rden = 1.0 / jnp.maximum(jnp.abs(den), jnp.exp(-mt))

    ktb = kt_s[...].astype(BF16)
    va = cols(va_ref)
    rowi = lax.broadcasted_iota(jnp.int32, (DEC_BATCH, ML_DV), 0)
    for j in range(bb):
        cj = c_ref[j]
        qc = _dot(qb, cj.astype(BF16))[j:j + 1, :]
        vsel = jnp.where(rowi == i * bb + j, va, 0.0).astype(BF16)
        outer = _dot(ktb, vsel)
        co_ref[j] = scale[j:j + 1, :] * cj + wts[j:j + 1, :] * outer
        num_s[j:j + 1, :] = scale[j:j + 1, :] * qc + sc[j:j + 1, :] * vf[j:j + 1, :]

    hh = num_s[...] * rden
    mu = jnp.mean(hh, axis=1, keepdims=True)
    hc = hh - mu
    var = jnp.mean(hc * hc, axis=1, keepdims=True)
    hn = hc * lax.rsqrt(var + LN_EPS) * ng_ref[...]
    hn_ref[...] = (jax.nn.sigmoid(cols(o_ref)) * hn).astype(BF16)
    no_ref[...] = scale * n_prev + wts * kf
    mo_ref[...] = jnp.broadcast_to(mt, (bb, LANE))


def _mlstm_sample(proj, gates, bias_row, norm_g, c_in, n_in, m_in, hn_all, c_acc, lj):
    bb = ML_DEC_BB
    rb = ROWS_PROMPT // bb
    r128 = ROWS_PROMPT // DEC_BATCH
    nh = ML_HEADS
    ks = ML_DK // LANE
    vs = ML_DV // LANE
    in_specs = [
        pl.BlockSpec((ks, bb, LANE), lambda h, i: (h, rb + i, 0)),
        pl.BlockSpec((ks, bb, LANE), lambda h, i: (nh + h, rb + i, 0)),
        pl.BlockSpec((vs, bb, LANE), lambda h, i: (nh + h, rb + i, 0)),
        pl.BlockSpec((vs, bb, LANE), lambda h, i: (2 * nh + h, rb + i, 0)),
        pl.BlockSpec((ks, DEC_BATCH, LANE), lambda h, i: (nh + h, r128, 0)),
        pl.BlockSpec((vs, DEC_BATCH, LANE), lambda h, i: (nh + h, r128, 0)),
        pl.BlockSpec((bb, LANE), lambda h, i: (rb + i, 0)),
        pl.BlockSpec((1, LANE), lambda h, i: (0, 0)),
        pl.BlockSpec((None, 1, ML_DV), lambda h, i: (lj, 0, h)),
        pl.BlockSpec((None, bb, None, ML_DK, ML_DV), lambda h, i: (lj, i, h, 0, 0)),
        pl.BlockSpec((None, bb, ML_DK), lambda h, i: (lj, i, h)),
        pl.BlockSpec((None, bb, nh), lambda h, i: (lj, i, 0)),
        pl.BlockSpec(memory_space=pl.ANY),
    ]
    args = [proj, proj, proj, proj, proj, proj, gates, bias_row, norm_g, c_in, n_in, m_in, hn_all]
    n_blocked = 12
    aliases = {n_blocked: 0}
    if c_acc is not None:
        in_specs.append(pl.BlockSpec(memory_space=pl.ANY))
        args.append(c_acc)
        aliases[n_blocked + 1] = 1

    def body(*refs):
        _mlstm_step_kernel(*refs[:n_blocked], *refs[len(args):])

    return pl.pallas_call(
        body,
        grid=(nh, DEC_BATCH // bb),
        in_specs=in_specs,
        out_specs=[
            pl.BlockSpec((bb, ML_DV), lambda h, i: (rb + i, h)),
            pl.BlockSpec((None, bb, None, ML_DK, ML_DV), lambda h, i: (lj, i, h, 0, 0)),
            pl.BlockSpec((bb, ML_DK), lambda h, i: (i, h)),
            pl.BlockSpec((None, bb, LANE), lambda h, i: (h, i, 0)),
        ],
        out_shape=[
            jax.ShapeDtypeStruct((ROWS_ALL, nh * ML_DV), BF16),
            jax.ShapeDtypeStruct(c_in.shape, F32),
            jax.ShapeDtypeStruct((DEC_BATCH, nh * ML_DK), F32),
            jax.ShapeDtypeStruct((nh, DEC_BATCH, LANE), F32),
        ],
        scratch_shapes=[pltpu.VMEM((ML_DK, DEC_BATCH), F32), pltpu.VMEM((bb, ML_DV), F32)],
        input_output_aliases=aliases,
        compiler_params=pltpu.CompilerParams(
            dimension_semantics=("arbitrary", "arbitrary"), vmem_limit_bytes=VMEM_LIMIT),
        name="mlstm_step_sample",
    )(*args)


def kernel(x_prompt, x_sample, state_s5_re, state_s5_im, state_mlstm_C, state_mlstm_n, state_mlstm_m,
           ln_g, ln_b, ffn_w_gate, ffn_w_up, ffn_w_down,
           s5_a_re, s5_a_im, s5_log_dt, s5_b_re, s5_b_im, s5_c_re, s5_c_im, s5_d, s5_w_a, s5_w_b,
           ml_w_in, ml_b_i, ml_b_f, ml_norm_g, ml_w_out):
    n_s5 = s5_a_re.shape[0]
    n_ml = ml_w_in.shape[0]

    ffn_f32 = (ffn_w_gate, ffn_w_up, ffn_w_down)
    ffn_w = tuple(w[0, 0].astype(BF16) for w in ffn_f32)
    glu_a = s5_w_a.astype(BF16)
    glu_b = s5_w_b.astype(BF16)
    w_in_gate = jnp.pad(ml_w_in[:, :, ML_QKVO:].astype(BF16), ((0, 0), (0, 0), (0, LANE - 2 * ML_HEADS)))
    w_out = ml_w_out.astype(BF16)
    ln_g4 = ln_g.reshape(DEPTH, 3, 1, D_MODEL)
    ln_b4 = ln_b.reshape(DEPTH, 3, 1, D_MODEL)
    ml_bias = jnp.pad(jnp.concatenate([ml_b_i, ml_b_f], axis=1), ((0, 0), (0, LANE - 2 * ML_HEADS)))
    ml_ng = ml_norm_g.reshape(n_ml, 1, ML_HEADS * ML_DV)
    st_r = state_s5_re.reshape(n_s5, DEC_BATCH, S5_LANES)
    st_i = state_s5_im.reshape(n_s5, DEC_BATCH, S5_LANES)
    st_n = state_mlstm_n.reshape(n_ml, DEC_BATCH, ML_HEADS * ML_DK)

    x = _to_rows(x_prompt, x_sample.reshape(DEC_BATCH, D_MODEL))

    p_s5r, p_s5i, s_s5r, s_s5i = [], [], [], []
    p_c, p_n, p_m, s_n, s_m = [], [], [], [], []
    s_c = None
    zero_state = jnp.zeros((8, S5_LANES), F32)
    w_in = None
    for li in range(DEPTH):
        x, ffn_w, _ = _ffn_ln(x, ffn_w, ln_g4, ln_b4, li, 0, ffn_f32, (li, 1))
        j = li // 2
        if li % 2 == 0:
            abr, abi, bbr, bbi = _s5_discretise(s5_a_re[j], s5_a_im[j], s5_log_dt[j], s5_b_re[j], s5_b_im[j])
            a1, a2, wb, wc = _s5_operands(abr, abi, bbr, bbi, s5_c_re[j], s5_c_im[j])
            d_row = s5_d[j].reshape(1, D_MODEL)
            z, hp = _s5_scan_prompt(x, zero_state, a1, a2, wb, wc, d_row)
            z, hsr, hsi = _s5_step_sample(x, st_r, st_i, j, a1, a2, wb, wc, d_row, z)
            p_s5r.append(hp[:BATCH].reshape(BATCH, S5_GROUPS, S5_STATE))
            p_s5i.append(hp[BATCH:].reshape(BATCH, S5_GROUPS, S5_STATE))
            s_s5r.append(hsr.reshape(DEC_BATCH, S5_GROUPS, S5_STATE))
            s_s5i.append(hsi.reshape(DEC_BATCH, S5_GROUPS, S5_STATE))
            x = _proj_ln(x, z, (glu_a, glu_b), ln_g4, ln_b4, j, li)
        else:
            proj, gates = _in_proj(x, w_in, w_in_gate, j)
            bias_row = ml_bias[j].reshape(1, LANE)
            z, cp, npr, mp = _mlstm_prompt(proj, gates, bias_row, ml_ng, j)
            z, s_c, ns, ms = _mlstm_sample(proj, gates, bias_row, ml_ng, state_mlstm_C, st_n,
                                           state_mlstm_m, z, s_c, j)
            p_c.append(cp)
            p_n.append(npr)
            p_m.append(mp[:, :ML_HEADS, 0])
            s_n.append(ns.reshape(DEC_BATCH, ML_HEADS, ML_DK))
            s_m.append(jnp.swapaxes(ms[:, :, 0], 0, 1))
            x = _proj_ln(x, z, (w_out,), ln_g4, ln_b4, j, li)
        extra = (ml_w_in, (li + 1) // 2, ML_QKVO) if li % 2 == 0 and li + 1 < DEPTH else None
        x, ffn_w, w_in = _ffn_ln(x, ffn_w, ln_g4, ln_b4, li, 2, ffn_f32, (li + 1, 0) if li + 1 < DEPTH else None,
                                 extra)

    y_prompt, y_sample = _from_rows(x)
    y_sample = y_sample.reshape(DEC_BATCH, 1, D_MODEL)
    return (y_prompt, y_sample,
            jnp.stack(p_s5r), jnp.stack(p_s5i), jnp.stack(p_c), jnp.stack(p_n), jnp.stack(p_m),
            jnp.stack(s_s5r), jnp.stack(s_s5i), s_c, jnp.stack(s_n), jnp.stack(s_m))
```

```python
import functools

import jax
import jax.numpy as jnp
from jax import lax
from jax.experimental import pallas as pl
from jax.experimental.pallas import tpu as pltpu

F32 = jnp.float32
BF16 = jnp.bfloat16

D_MODEL = 2048
BATCH = 4
SEQ = 2048
DEPTH = 4
DEC_BATCH = 128
D_FF = 5504
S5_GROUPS = 128
S5_GROUP = 16
S5_STATE = 64
S5_LANES = S5_GROUPS * S5_STATE
ML_HEADS = 4
ML_DK = 256
ML_DV = 512
ML_CHUNK = 128
ML_QKVO = 2 * ML_HEADS * ML_DK + 2 * ML_HEADS * ML_DV
DN_ALPHA = (2.0 * DEPTH) ** 0.25
LN_EPS = 1e-5
K_SCALE = ML_DK ** -0.5

LANE = 128
ROWS_PROMPT = SEQ * BATCH
ROWS_ALL = ROWS_PROMPT + DEC_BATCH
TM = 640
TM_FF = 832
TF = 512
CV_BLOCKS = 8
TN = 512
PROJ_LN_GROUPS = 2
TM_IN = 1040
TN_IN = 1024
S5_OCT = 8
S5_NOCT = S5_GROUPS // S5_OCT
S5_OCT_CH = S5_OCT * S5_GROUP
S5_OCT_ST = S5_OCT * S5_STATE
S5_STEPS = 256
S5_PAIR = 4
RL_STEPS = 1024
ML_DEC_BB = 16
VMEM_LIMIT = 56 * 1024 * 1024


def _dot(a, b):
    return jnp.dot(a, b, preferred_element_type=F32)


def _layer_norm_rows(y, g, b):
    mu = jnp.mean(y, axis=-1, keepdims=True)
    yc = y - mu
    var = jnp.mean(yc * yc, axis=-1, keepdims=True)
    return yc * lax.rsqrt(var + LN_EPS) * g + b


def _row_groups(n_rows, n_groups):
    step = -(-n_rows // n_groups)
    step = -(-step // 16) * 16
    return [slice(lo, min(lo + step, n_rows)) for lo in range(0, n_rows, step)]


def _log_sigmoid(x):
    return jnp.minimum(x, 0.0) - jnp.log1p(jnp.exp(-jnp.abs(x)))


def _to_rows_kernel(xp_ref, xs_ref, o_ref, *, nt):
    r = pl.program_id(0)

    @pl.when(r < nt)
    def _():
        for b in range(BATCH):
            o_ref[pl.ds(b, RL_STEPS, stride=BATCH), :] = xp_ref[b]

    @pl.when(r == nt)
    def _():
        o_ref[0:DEC_BATCH, :] = xs_ref[...]


def _to_rows(x_prompt, x_sample):
    nt = SEQ // RL_STEPS
    ns = D_MODEL // LANE
    return pl.pallas_call(
        functools.partial(_to_rows_kernel, nt=nt),
        grid=(nt + 1, ns),
        in_specs=[
            pl.BlockSpec((BATCH, RL_STEPS, LANE),
                         lambda r, s: (0, jnp.minimum(r, nt - 1), jnp.where(r == nt, ns - 1, s))),
            pl.BlockSpec((DEC_BATCH, LANE), lambda r, s: (0, s)),
        ],
        out_specs=pl.BlockSpec((RL_STEPS * BATCH, LANE), lambda r, s: (r, s)),
        out_shape=jax.ShapeDtypeStruct((ROWS_ALL, D_MODEL), F32),
        compiler_params=pltpu.CompilerParams(
            dimension_semantics=("arbitrary", "arbitrary"), vmem_limit_bytes=VMEM_LIMIT),
        name="to_rows",
    )(x_prompt, x_sample)


def _from_rows_kernel(x_ref, yp_ref, ys_ref, *, nt):
    r = pl.program_id(0)

    @pl.when(r < nt)
    def _():
        for b in range(BATCH):
            yp_ref[b] = x_ref[pl.ds(b, RL_STEPS, stride=BATCH), :]

    @pl.when(r == nt)
    def _():
        ys_ref[...] = x_ref[0:DEC_BATCH, :]


def _from_rows(x):
    nt = SEQ // RL_STEPS
    ns = D_MODEL // LANE
    return pl.pallas_call(
        functools.partial(_from_rows_kernel, nt=nt),
        grid=(nt + 1, ns),
        in_specs=[pl.BlockSpec((RL_STEPS * BATCH, LANE), lambda r, s: (r, s))],
        out_specs=[
            pl.BlockSpec((BATCH, RL_STEPS, LANE),
                         lambda r, s: (0, jnp.minimum(r, nt - 1), jnp.where(r == nt, ns - 1, s))),
            pl.BlockSpec((DEC_BATCH, LANE), lambda r, s: (0, jnp.where(r == nt, s, 0))),
        ],
        out_shape=[jax.ShapeDtypeStruct((BATCH, SEQ, D_MODEL), F32),
                   jax.ShapeDtypeStruct((DEC_BATCH, D_MODEL), F32)],
        compiler_params=pltpu.CompilerParams(
            dimension_semantics=("arbitrary", "arbitrary"), vmem_limit_bytes=VMEM_LIMIT),
        name="from_rows",
    )(x)


def _ffn_kernel(x_ref, wg_ref, wu_ref, wd_ref, g_ref, b_ref, *refs, nf, convert_next):
    if convert_next:
        ng_ref, nu_ref, nd_ref, o_ref, cg_ref, cu_ref, cd_ref, xb_ref = refs
    else:
        o_ref, xb_ref = refs
    f = pl.program_id(1)

    @pl.when(f == 0)
    def _():
        x = x_ref[...]
        xb_ref[...] = x.astype(BF16)
        o_ref[...] = (2.0 * DN_ALPHA) * x

    def hidden_tile(width):
        xb = xb_ref[...]
        hg = _dot(xb, wg_ref[:, :width])
        hu = _dot(xb, wu_ref[:, :width])
        act = (hg * jax.nn.sigmoid(hg) * hu).astype(BF16)
        o_ref[...] += _dot(act, wd_ref[:width, :])
        if convert_next:
            cg_ref[:, :width] = ng_ref[:, :width].astype(BF16)
            cu_ref[:, :width] = nu_ref[:, :width].astype(BF16)
            cd_ref[:width, :] = nd_ref[:width, :].astype(BF16)

    @pl.when(f < nf - 1)
    def _():
        hidden_tile(TF)

    @pl.when(f == nf - 1)
    def _():
        hidden_tile(D_FF - (nf - 1) * TF)
        o_ref[...] = _layer_norm_rows(0.5 * o_ref[...], g_ref[...], b_ref[...])


def _ffn_ln(x, w_cur, ln_g, ln_b, li, kn, w_f32, nxt):
    m = x.shape[0]
    nf = pl.cdiv(D_FF, TF)
    n_row_tiles = m // TM_FF
    cv = D_MODEL // CV_BLOCKS
    assert n_row_tiles >= CV_BLOCKS
    in_specs = [
        pl.BlockSpec((TM_FF, D_MODEL), lambda i, f: (i, 0)),
        pl.BlockSpec((D_MODEL, TF), lambda i, f: (0, f)),
        pl.BlockSpec((D_MODEL, TF), lambda i, f: (0, f)),
        pl.BlockSpec((TF, D_MODEL), lambda i, f: (f, 0)),
        pl.BlockSpec((None, None, 1, D_MODEL), lambda i, f: (li, kn, 0, 0)),
        pl.BlockSpec((None, None, 1, D_MODEL), lambda i, f: (li, kn, 0, 0)),
    ]
    out_specs = [pl.BlockSpec((TM_FF, D_MODEL), lambda i, f: (i, 0))]
    out_shape = [jax.ShapeDtypeStruct((m, D_MODEL), F32)]
    args = [x, *w_cur, ln_g, ln_b]
    if nxt is not None:
        nli, nk = nxt

        def cv_row(i, f):
            return jnp.minimum(i, CV_BLOCKS - 1)

        def cv_col(i, f):
            return jnp.where(i < CV_BLOCKS, f, nf - 1)

        in_specs += [
            pl.BlockSpec((None, None, cv, TF), lambda i, f: (nli, nk, cv_row(i, f), cv_col(i, f))),
            pl.BlockSpec((None, None, cv, TF), lambda i, f: (nli, nk, cv_row(i, f), cv_col(i, f))),
            pl.BlockSpec((None, None, TF, cv), lambda i, f: (nli, nk, cv_col(i, f), cv_row(i, f))),
        ]
        out_specs += [
            pl.BlockSpec((cv, TF), lambda i, f: (cv_row(i, f), cv_col(i, f))),
            pl.BlockSpec((cv, TF), lambda i, f: (cv_row(i, f), cv_col(i, f))),
            pl.BlockSpec((TF, cv), lambda i, f: (cv_col(i, f), cv_row(i, f))),
        ]
        out_shape += [jax.ShapeDtypeStruct((D_MODEL, D_FF), BF16), jax.ShapeDtypeStruct((D_MODEL, D_FF), BF16),
                      jax.ShapeDtypeStruct((D_FF, D_MODEL), BF16)]
        args += list(w_f32)
    outs = pl.pallas_call(
        functools.partial(_ffn_kernel, nf=nf, convert_next=nxt is not None),
        grid=(n_row_tiles, nf),
        in_specs=in_specs,
        out_specs=out_specs,
        out_shape=out_shape,
        scratch_shapes=[pltpu.VMEM((TM_FF, D_MODEL), BF16)],
        compiler_params=pltpu.CompilerParams(
            dimension_semantics=("arbitrary", "arbitrary"), vmem_limit_bytes=VMEM_LIMIT),
        name="ffn_ln",
    )(*args)
    return outs[0], (tuple(outs[1:]) if nxt is not None else None)


def _proj_ln_kernel(x_ref, z_ref, *refs, gated, nn):
    w_refs, (g_ref, b_ref, o_ref, mix_ref) = refs[:-4], refs[-4:]
    n = pl.program_id(1)

    def mix_tile(rows):
        z = z_ref[rows, :]
        p = _dot(z, w_refs[0][...])
        if gated:
            p = p * jax.nn.sigmoid(_dot(z, w_refs[1][...]))
        return p

    @pl.when(n < nn - 1)
    def _():
        mix_ref[n] = mix_tile(slice(None))

    @pl.when(n == nn - 1)
    def _():
        def finish(rows, last_tile):
            tiles = [mix_ref[j, rows, :] for j in range(nn - 1)] + [last_tile]
            ys = [DN_ALPHA * x_ref[rows, j * TN:(j + 1) * TN] + tiles[j] for j in range(nn)]
            mu = sum(jnp.sum(y, axis=-1, keepdims=True) for y in ys) * (1.0 / D_MODEL)
            var = sum(jnp.sum((y - mu) * (y - mu), axis=-1, keepdims=True) for y in ys) * (1.0 / D_MODEL)
            r = lax.rsqrt(var + LN_EPS)
            for j in range(nn):
                sl = slice(j * TN, (j + 1) * TN)
                o_ref[rows, sl] = (ys[j] - mu) * r * g_ref[:, sl] + b_ref[:, sl]

        groups = _row_groups(o_ref.shape[0], PROJ_LN_GROUPS)
        pending = None
        for rows in groups:
            tile = mix_tile(rows)
            if pending is not None:
                finish(*pending)
            pending = (rows, tile)
        finish(*pending)


def _proj_ln(x, z, ws, ln_g, ln_b, lj, li):
    m = x.shape[0]
    nn = D_MODEL // TN
    gated = len(ws) == 2
    row_spec = pl.BlockSpec((TM, D_MODEL), lambda i, n: (i, 0))
    w_spec = pl.BlockSpec((None, D_MODEL, TN), lambda i, n: (lj, 0, n))
    ln_spec = pl.BlockSpec((None, None, 1, D_MODEL), lambda i, n: (li, 1, 0, 0))
    return pl.pallas_call(
        functools.partial(_proj_ln_kernel, gated=gated, nn=nn),
        grid=(m // TM, nn),
        in_specs=[row_spec, row_spec] + [w_spec] * len(ws) + [ln_spec, ln_spec],
        out_specs=row_spec,
        out_shape=jax.ShapeDtypeStruct((m, D_MODEL), F32),
        scratch_shapes=[pltpu.VMEM((nn - 1, TM, TN), F32)],
        compiler_params=pltpu.CompilerParams(
            dimension_semantics=("parallel", "arbitrary"), vmem_limit_bytes=VMEM_LIMIT),
        name="glu_ln" if gated else "out_ln",
    )(x, z, *ws, ln_g, ln_b)


def _in_proj_kernel(x_ref, w_ref, wg_ref, p_ref, gt_ref, xb_ref):
    n = pl.program_id(1)

    @pl.when(n == 0)
    def _():
        xb = x_ref[...].astype(BF16)
        xb_ref[...] = xb
        gt_ref[...] = _dot(xb, wg_ref[...])

    p = _dot(xb_ref[...], w_ref[...])
    for s in range(TN_IN // LANE):
        p_ref[s] = p[:, s * LANE:(s + 1) * LANE]


def _in_proj(x, w_in, w_gate, lj):
    m = x.shape[0]
    slabs = TN_IN // LANE
    return pl.pallas_call(
        _in_proj_kernel,
        grid=(m // TM_IN, ML_QKVO // TN_IN),
        in_specs=[
            pl.BlockSpec((TM_IN, D_MODEL), lambda i, n: (i, 0)),
            pl.BlockSpec((None, D_MODEL, TN_IN), lambda i, n: (lj, 0, n)),
            pl.BlockSpec((None, D_MODEL, LANE), lambda i, n: (lj, 0, 0)),
        ],
        out_specs=[
            pl.BlockSpec((slabs, TM_IN, LANE), lambda i, n: (n, i, 0)),
            pl.BlockSpec((TM_IN, LANE), lambda i, n: (i, 0)),
        ],
        out_shape=[jax.ShapeDtypeStruct((ML_QKVO // LANE, m, LANE), F32), jax.ShapeDtypeStruct((m, LANE), F32)],
        scratch_shapes=[pltpu.VMEM((TM_IN, D_MODEL), BF16)],
        compiler_params=pltpu.CompilerParams(
            dimension_semantics=("parallel", "arbitrary"), vmem_limit_bytes=VMEM_LIMIT),
        name="ml_in_proj",
    )(x, w_in, w_gate)


def _s5_disc_kernel(ar_ref, ai_ref, ldt_ref, br_ref, bi_ref, abr_ref, abi_ref, bbr_ref, bbi_ref):
    ar = ar_ref[...]
    ai = ai_ref[...]
    dt = jnp.exp(ldt_ref[...])
    mag = jnp.exp(dt * ar)
    abr = mag * jnp.cos(dt * ai)
    abi = mag * jnp.sin(dt * ai)
    zr = abr - 1.0
    zi = abi
    den = ar * ar + ai * ai
    gr = (zr * ar + zi * ai) / den
    gi = (zi * ar - zr * ai) / den
    br = br_ref[...]
    bi = bi_ref[...]
    abr_ref[...] = abr
    abi_ref[...] = abi
    bbr_ref[...] = gr * br - gi * bi
    bbi_ref[...] = gr * bi + gi * br


def _s5_discretise(a_re, a_im, log_dt, b_re, b_im):
    g, p, c = S5_GROUPS, S5_STATE, S5_GROUP
    outs = pl.pallas_call(
        _s5_disc_kernel,
        out_shape=[jax.ShapeDtypeStruct((g, 1, p), F32), jax.ShapeDtypeStruct((g, 1, p), F32),
                   jax.ShapeDtypeStruct((g, c, p), F32), jax.ShapeDtypeStruct((g, c, p), F32)],
        name="s5_discretise",
    )(a_re.reshape(g, 1, p), a_im.reshape(g, 1, p), log_dt.reshape(g, 1, 1),
      jnp.swapaxes(b_re, 1, 2), jnp.swapaxes(b_im, 1, 2))
    abr, abi, bbr, bbi = outs
    return abr.reshape(1, g * p), abi.reshape(1, g * p), bbr, bbi


def _block_diag_in(bb):
    eye = jnp.eye(S5_OCT, dtype=F32)
    w = bb.reshape(S5_NOCT, S5_OCT, S5_GROUP, 1, S5_STATE) * eye[None, :, None, :, None]
    return w.reshape(S5_NOCT, S5_OCT_CH, S5_OCT_ST).astype(BF16)


def _block_diag_out(c):
    eye = jnp.eye(S5_OCT, dtype=F32)
    cx = jnp.swapaxes(c.reshape(S5_NOCT, S5_OCT, S5_GROUP, S5_STATE), 2, 3)
    w = cx[:, :, :, None, :] * eye[None, :, None, :, None]
    return w.reshape(S5_NOCT, S5_OCT_ST, S5_OCT_CH).astype(BF16)


def _s5_seq_kernel(*refs, nsteps, nchunks):
    u_refs = refs[:S5_PAIR]
    (h0_ref, a1_ref, a2_ref, wb_ref, wc_ref, d_ref, z_ref, hf_ref,
     lhs_l, lhs_r, x8_s, zl_s, zr_s, zo_s, st_s) = refs[S5_PAIR:]
    c = pl.program_id(1)
    ch, st = S5_OCT_CH, S5_OCT_ST

    def seq_steps(b, stride, half=0):
        return pl.ds(half * BATCH + b, nsteps, stride=stride)

    @pl.when(c == 0)
    def _():
        st_s[...] = h0_ref[...]
        lhs_l[...] = jnp.zeros_like(lhs_l)
        lhs_r[...] = jnp.zeros_like(lhs_r)

    for o in range(S5_PAIR):
        for b in range(BATCH):
            ub = u_refs[o][seq_steps(b, BATCH), :]
            lhs_l[o, seq_steps(b, 8, 0), :] = ub
            lhs_r[o, seq_steps(b, 8, 1), :] = ub
        lhs = jnp.concatenate([lhs_l[o], lhs_r[o]], axis=1).astype(BF16)
        x8_s[:, o * st:(o + 1) * st] = _dot(lhs, wb_ref[o])

    a1 = a1_ref[...]
    a2 = a2_ref[...]

    def step(t, h):
        r = pl.multiple_of(t * 8, 8)
        hn = (a1 * h + x8_s[pl.ds(r, 8), :]) + a2 * pltpu.roll(h, BATCH, 0)
        x8_s[pl.ds(r, 8), :] = hn
        return hn

    h_last = lax.fori_loop(0, nsteps, step, st_s[...], unroll=4)
    st_s[...] = h_last

    for o in range(S5_PAIR):
        zz = _dot(x8_s[:, o * st:(o + 1) * st].astype(BF16), wc_ref[o])
        zl_s[...] = zz[:, :ch]
        zr_s[...] = zz[:, ch:]
        for b in range(BATCH):
            y = zl_s[seq_steps(b, 8, 0), :] + zr_s[seq_steps(b, 8, 1), :]
            y = y + d_ref[:, o * ch:(o + 1) * ch] * u_refs[o][seq_steps(b, BATCH), :]
            zo_s[seq_steps(b, BATCH), :] = jax.nn.gelu(y)
        z_ref[:, o * ch:(o + 1) * ch] = zo_s[...].astype(BF16)

    @pl.when(c == nchunks - 1)
    def _():
        hf_ref[...] = h_last


def _s5_scan_prompt(x_all, h0, abr, abi, wbr, wbi, wcr, wci, d_row):
    rows = S5_STEPS * BATCH
    nchunks = SEQ // S5_STEPS
    pc, ps = S5_PAIR * S5_OCT_CH, S5_PAIR * S5_OCT_ST
    a1 = jnp.broadcast_to(abr, (8, S5_LANES))
    a2 = jnp.concatenate([jnp.broadcast_to(-abi, (BATCH, S5_LANES)), jnp.broadcast_to(abi, (BATCH, S5_LANES))])
    wb = jnp.concatenate([wbr, wbi], axis=1)
    wc = jnp.concatenate([wcr, -wci], axis=2)
    st_spec = pl.BlockSpec((8, ps), lambda p, c: (0, p))
    return pl.pallas_call(
        functools.partial(_s5_seq_kernel, nsteps=S5_STEPS, nchunks=nchunks),
        grid=(S5_NOCT // S5_PAIR, nchunks),
        in_specs=[pl.BlockSpec((rows, S5_OCT_CH), lambda p, c, o=o: (c, S5_PAIR * p + o)) for o in range(S5_PAIR)] + [
            st_spec, st_spec, st_spec,
            pl.BlockSpec((S5_PAIR, 2 * S5_OCT_CH, S5_OCT_ST), lambda p, c: (p, 0, 0)),
            pl.BlockSpec((S5_PAIR, S5_OCT_ST, 2 * S5_OCT_CH), lambda p, c: (p, 0, 0)),
            pl.BlockSpec((1, pc), lambda p, c: (0, p)),
        ],
        out_specs=[pl.BlockSpec((rows, pc), lambda p, c: (c, p)), st_spec],
        out_shape=[jax.ShapeDtypeStruct((ROWS_ALL, D_MODEL), BF16),
                   jax.ShapeDtypeStruct((8, S5_LANES), F32)],
        scratch_shapes=[pltpu.VMEM((S5_PAIR, 2 * rows, S5_OCT_CH), F32), pltpu.VMEM((S5_PAIR, 2 * rows, S5_OCT_CH), F32),
                        pltpu.VMEM((2 * rows, ps), F32),
                        pltpu.VMEM((2 * rows, S5_OCT_CH), F32), pltpu.VMEM((2 * rows, S5_OCT_CH), F32),
                        pltpu.VMEM((rows, S5_OCT_CH), F32), pltpu.VMEM((8, ps), F32)],
        compiler_params=pltpu.CompilerParams(
            dimension_semantics=("parallel", "arbitrary"), vmem_limit_bytes=VMEM_LIMIT),
        name="s5_scan_prompt",
    )(*([x_all] * S5_PAIR), h0, a1, a2, wb, wc, d_row)


def _s5_step_kernel(u_ref, h0r_ref, h0i_ref, ar_ref, ai_ref, wbr_ref, wbi_ref, wcr_ref, wci_ref, d_ref,
                    z_all_ref, z_ref, hr_ref, hi_ref):
    del z_all_ref
    u = u_ref[...]
    ub = u.astype(BF16)
    xr = _dot(ub, wbr_ref[...])
    xi = _dot(ub, wbi_ref[...])
    ar = ar_ref[...]
    ai = ai_ref[...]
    h0r = h0r_ref[...]
    h0i = h0i_ref[...]
    hr = xr + (ar * h0r - ai * h0i)
    hi = xi + (ar * h0i + ai * h0r)
    hr_ref[...] = hr
    hi_ref[...] = hi
    y = _dot(hr.astype(BF16), wcr_ref[...]) - _dot(hi.astype(BF16), wci_ref[...])
    y = y + d_ref[...] * u
    z_ref[...] = jax.nn.gelu(y).astype(BF16)


def _s5_step_sample(x_all, h0r, h0i, lj, abr, abi, wbr, wbi, wcr, wci, d_row, z_all):
    row_blk = ROWS_PROMPT // DEC_BATCH
    oct_w = pl.BlockSpec((None, S5_OCT_CH, S5_OCT_ST), lambda o: (o, 0, 0))
    oct_c = pl.BlockSpec((None, S5_OCT_ST, S5_OCT_CH), lambda o: (o, 0, 0))
    st_in = pl.BlockSpec((None, DEC_BATCH, S5_OCT_ST), lambda o: (lj, 0, o))
    st_out = pl.BlockSpec((DEC_BATCH, S5_OCT_ST), lambda o: (0, o))
    return pl.pallas_call(
        _s5_step_kernel,
        grid=(S5_NOCT,),
        in_specs=[
            pl.BlockSpec((DEC_BATCH, S5_OCT_CH), lambda o: (row_blk, o)),
            st_in, st_in,
            pl.BlockSpec((1, S5_OCT_ST), lambda o: (0, o)),
            pl.BlockSpec((1, S5_OCT_ST), lambda o: (0, o)),
            oct_w, oct_w, oct_c, oct_c,
            pl.BlockSpec((1, S5_OCT_CH), lambda o: (0, o)),
            pl.BlockSpec(memory_space=pl.ANY),
        ],
        out_specs=[pl.BlockSpec((DEC_BATCH, S5_OCT_CH), lambda o: (row_blk, o)), st_out, st_out],
        out_shape=[jax.ShapeDtypeStruct((ROWS_ALL, D_MODEL), BF16),
                   jax.ShapeDtypeStruct((DEC_BATCH, S5_LANES), F32),
                   jax.ShapeDtypeStruct((DEC_BATCH, S5_LANES), F32)],
        input_output_aliases={10: 0},
        compiler_params=pltpu.CompilerParams(
            dimension_semantics=("parallel",), vmem_limit_bytes=VMEM_LIMIT),
        name="s5_step_sample",
    )(x_all, h0r, h0i, abr, abi, wbr, wbi, wcr, wci, d_row, z_all)


def _mlstm_seq_kernel(q_ref, k_ref, v_ref, o_ref, gt_ref, bias_ref, ng_ref,
                      hn_ref, cf_ref, nf_ref, mf_ref, c_s, n_s, m_s, hn_s, *, nchunks):
    ci = pl.program_id(0)
    b = pl.program_id(1)
    L = ML_CHUNK
    rows = pl.ds(b, L, stride=BATCH)

    def seq_rows(ref, h, width):
        per_head = width // LANE
        return jnp.concatenate([ref[h * per_head + j, rows, :] for j in range(per_head)], axis=1)

    @pl.when(ci == 0)
    def _():
        c_s[b] = jnp.zeros(c_s.shape[1:], F32)
        n_s[b] = jnp.zeros(n_s.shape[1:], F32)
        m_s[b] = jnp.zeros(m_s.shape[1:], F32)

    gb = gt_ref[rows, :] + bias_ref[...]
    lfa = _log_sigmoid(gb)
    gbt = gb.T
    lft = lfa.T
    row = lax.broadcasted_iota(jnp.int32, (L, L), 0)
    col = lax.broadcasted_iota(jnp.int32, (L, L), 1)
    causal = col <= row
    tril = causal.astype(F32)
    triu = (row <= col).astype(F32)
    bc_cols = jnp.dot(tril, lfa, precision=lax.Precision.HIGHEST, preferred_element_type=F32)
    bc_rows = jnp.dot(lft[0:8, :], triu, precision=lax.Precision.HIGHEST, preferred_element_type=F32)

    for h in range(ML_HEADS):
        ig_row = gbt[h:h + 1, :]
        bc_row = bc_rows[ML_HEADS + h:ML_HEADS + h + 1, :]
        ig_col = gb[:, h:h + 1]
        bc_col = bc_cols[:, ML_HEADS + h:ML_HEADS + h + 1]
        m_prev = m_s[b, h:h + 1, 0:1]
        inter = bc_col + m_prev
        dmat = jnp.where(causal, bc_col - bc_row + ig_row, -jnp.inf)
        mt = jnp.maximum(inter, jnp.max(dmat, axis=1, keepdims=True))
        wts = jnp.exp(dmat - mt)

        qf = seq_rows(q_ref, h, ML_DK)
        kf = seq_rows(k_ref, h, ML_DK) * K_SCALE
        qb = qf.astype(BF16)
        vb = seq_rows(v_ref, h, ML_DV).astype(BF16)
        qk = lax.dot_general(qb, kf.astype(BF16), (((1,), (1,)), ((), ())), preferred_element_type=F32)
        sc = qk * wts
        scale = jnp.exp(inter - mt)
        cmat = c_s[b, h]
        n_row = n_s[b, h:h + 1, :]
        num = scale * _dot(qb, cmat.astype(BF16)) + _dot(sc.astype(BF16), vb)
        qn = jnp.sum(qb.astype(F32) * n_row, axis=1, keepdims=True)
        den = scale * qn + jnp.sum(sc, axis=1, keepdims=True)
        hh = num * (1.0 / jnp.maximum(jnp.abs(den), jnp.exp(-mt)))

        mu = jnp.mean(hh, axis=1, keepdims=True)
        hc = hh - mu
        var = jnp.mean(hc * hc, axis=1, keepdims=True)
        hn = hc * lax.rsqrt(var + LN_EPS) * ng_ref[:, h * ML_DV:(h + 1) * ML_DV]
        gated = jax.nn.sigmoid(seq_rows(o_ref, h, ML_DV)) * hn
        for j in range(ML_DV // LANE):
            hn_s[h * (ML_DV // LANE) + j, rows, :] = gated[:, j * LANE:(j + 1) * LANE]

        m_last = mt[L - 1:L, :]
        b_last = bc_col[L - 1:L, :]
        dec = jnp.exp(b_last - bc_col + ig_col - m_last)
        cscale = jnp.exp(b_last + m_prev - m_last)
        kd = kf * dec
        c_new = cscale * cmat + _dot(kd.T.astype(BF16), vb)
        n_new = cscale * n_row + jnp.sum(kd, axis=0, keepdims=True)
        c_s[b, h] = c_new
        n_s[b, h:h + 1, :] = n_new
        m_s[b, h:h + 1, :] = jnp.broadcast_to(m_last, (1, LANE))

    @pl.when(ci == nchunks - 1)
    def _():
        cf_ref[...] = c_s[b]
        nf_ref[...] = n_s[b, 0:ML_HEADS, :]
        mf_ref[...] = m_s[b]

    @pl.when(b == BATCH - 1)
    def _():
        for s in range(ML_HEADS * ML_DV // LANE):
            hn_ref[:, s * LANE:(s + 1) * LANE] = hn_s[s].astype(BF16)


def _mlstm_prompt(proj, gates, bias_row, norm_g, lj):
    nchunks = SEQ // ML_CHUNK
    rows = ML_CHUNK * BATCH
    qk_s = ML_HEADS * ML_DK // LANE
    v_s = ML_HEADS * ML_DV // LANE
    v_w = ML_HEADS * ML_DV
    last = nchunks - 1

    def state_idx(c, b):
        return jnp.where(c == last, b, 0)

    return pl.pallas_call(
        functools.partial(_mlstm_seq_kernel, nchunks=nchunks),
        grid=(nchunks, BATCH),
        in_specs=[
            pl.BlockSpec((qk_s, rows, LANE), lambda c, b: (0, c, 0)),
            pl.BlockSpec((qk_s, rows, LANE), lambda c, b: (1, c, 0)),
            pl.BlockSpec((v_s, rows, LANE), lambda c, b: (1, c, 0)),
            pl.BlockSpec((v_s, rows, LANE), lambda c, b: (2, c, 0)),
            pl.BlockSpec((rows, LANE), lambda c, b: (c, 0)),
            pl.BlockSpec((1, LANE), lambda c, b: (0, 0)),
            pl.BlockSpec((None, 1, v_w), lambda c, b: (lj, 0, 0)),
        ],
        out_specs=[
            pl.BlockSpec((rows, v_w), lambda c, b: (c, 0)),
            pl.BlockSpec((None, ML_HEADS, ML_DK, ML_DV), lambda c, b: (state_idx(c, b), 0, 0, 0)),
            pl.BlockSpec((None, ML_HEADS, ML_DK), lambda c, b: (state_idx(c, b), 0, 0)),
            pl.BlockSpec((None, 8, LANE), lambda c, b: (state_idx(c, b), 0, 0)),
        ],
        out_shape=[
            jax.ShapeDtypeStruct((ROWS_ALL, v_w), BF16),
            jax.ShapeDtypeStruct((BATCH, ML_HEADS, ML_DK, ML_DV), F32),
            jax.ShapeDtypeStruct((BATCH, ML_HEADS, ML_DK), F32),
            jax.ShapeDtypeStruct((BATCH, 8, LANE), F32),
        ],
        scratch_shapes=[pltpu.VMEM((BATCH, ML_HEADS, ML_DK, ML_DV), F32), pltpu.VMEM((BATCH, 8, ML_DK), F32),
                        pltpu.VMEM((BATCH, 8, LANE), F32), pltpu.VMEM((v_s, rows, LANE), F32)],
        compiler_params=pltpu.CompilerParams(
            dimension_semantics=("arbitrary", "arbitrary"), vmem_limit_bytes=VMEM_LIMIT),
        name="mlstm_prompt",
    )(proj, proj, proj, proj, gates, bias_row, norm_g)


def _mlstm_step_kernel(q_ref, k_ref, v_ref, o_ref, ka_ref, va_ref, gt_ref, bias_ref, ng_ref,
                       c_ref, n_ref, m_ref, hn_ref, co_ref, no_ref, mo_ref, kt_s, num_s):
    h = pl.program_id(0)
    i = pl.program_id(1)
    bb = ML_DEC_BB

    def cols(ref):
        return jnp.concatenate([ref[s] for s in range(ref.shape[0])], axis=1)

    @pl.when(i == 0)
    def _():
        kt_s[...] = (cols(ka_ref) * K_SCALE).T

    gb = gt_ref[...] + bias_ref[...]
    lane = lax.broadcasted_iota(jnp.int32, (bb, LANE), 1)
    ig = jnp.sum(jnp.where(lane == h, gb, 0.0), axis=1, keepdims=True)
    fp = jnp.sum(jnp.where(lane == h + ML_HEADS, gb, 0.0), axis=1, keepdims=True)
    lf = _log_sigmoid(fp)
    m4 = m_ref[...]
    hl = lax.broadcasted_iota(jnp.int32, (bb, ML_HEADS), 1)
    m_prev = jnp.sum(jnp.where(hl == h, m4, 0.0), axis=1, keepdims=True)
    inter = lf + m_prev
    mt = jnp.maximum(inter, ig)
    wts = jnp.exp(ig - mt)
    scale = jnp.exp(inter - mt)

    qb = cols(q_ref).astype(BF16)
    qf = qb.astype(F32)
    kf = cols(k_ref) * K_SCALE
    vf = cols(v_ref).astype(BF16).astype(F32)
    n_prev = n_ref[...]
    qk = jnp.sum(qf * kf.astype(BF16).astype(F32), axis=1, keepdims=True)
    sc = qk * wts
    qn = jnp.sum(qf * n_prev, axis=1, keepdims=True)
    den = scale * qn + sc
    rden = 1.0 / jnp.maximum(jnp.abs(den), jnp.exp(-mt))

    ktb = kt_s[...].astype(BF16)
    va = cols(va_ref)
    rowi = lax.broadcasted_iota(jnp.int32, (DEC_BATCH, ML_DV), 0)
    for j in range(bb):
        cj = c_ref[j]
        qc = _dot(qb, cj.astype(BF16))[j:j + 1, :]
        vsel = jnp.where(rowi == i * bb + j, va, 0.0).astype(BF16)
        outer = _dot(ktb, vsel)
        co_ref[j] = scale[j:j + 1, :] * cj + wts[j:j + 1, :] * outer
        num_s[j:j + 1, :] = scale[j:j + 1, :] * qc + sc[j:j + 1, :] * vf[j:j + 1, :]

    hh = num_s[...] * rden
    mu = jnp.mean(hh, axis=1, keepdims=True)
    hc = hh - mu
    var = jnp.mean(hc * hc, axis=1, keepdims=True)
    hn = hc * lax.rsqrt(var + LN_EPS) * ng_ref[...]
    hn_ref[...] = (jax.nn.sigmoid(cols(o_ref)) * hn).astype(BF16)
    no_ref[...] = scale * n_prev + wts * kf
    mo_ref[...] = jnp.broadcast_to(mt, (bb, LANE))


def _mlstm_sample(proj, gates, bias_row, norm_g, c_in, n_in, m_in, hn_all, c_acc, lj):
    bb = ML_DEC_BB
    rb = ROWS_PROMPT // bb
    r128 = ROWS_PROMPT // DEC_BATCH
    nh = ML_HEADS
    ks = ML_DK // LANE
    vs = ML_DV // LANE
    in_specs = [
        pl.BlockSpec((ks, bb, LANE), lambda h, i: (h, rb + i, 0)),
        pl.BlockSpec((ks, bb, LANE), lambda h, i: (nh + h, rb + i, 0)),
        pl.BlockSpec((vs, bb, LANE), lambda h, i: (nh + h, rb + i, 0)),
        pl.BlockSpec((vs, bb, LANE), lambda h, i: (2 * nh + h, rb + i, 0)),
        pl.BlockSpec((ks, DEC_BATCH, LANE), lambda h, i: (nh + h, r128, 0)),
        pl.BlockSpec((vs, DEC_BATCH, LANE), lambda h, i: (nh + h, r128, 0)),
        pl.BlockSpec((bb, LANE), lambda h, i: (rb + i, 0)),
        pl.BlockSpec((1, LANE), lambda h, i: (0, 0)),
        pl.BlockSpec((None, 1, ML_DV), lambda h, i: (lj, 0, h)),
        pl.BlockSpec((None, bb, None, ML_DK, ML_DV), lambda h, i: (lj, i, h, 0, 0)),
        pl.BlockSpec((None, bb, ML_DK), lambda h, i: (lj, i, h)),
        pl.BlockSpec((None, bb, nh), lambda h, i: (lj, i, 0)),
        pl.BlockSpec(memory_space=pl.ANY),
    ]
    args = [proj, proj, proj, proj, proj, proj, gates, bias_row, norm_g, c_in, n_in, m_in, hn_all]
    n_blocked = 12
    aliases = {n_blocked: 0}
    if c_acc is not None:
        in_specs.append(pl.BlockSpec(memory_space=pl.ANY))
        args.append(c_acc)
        aliases[n_blocked + 1] = 1

    def body(*refs):
        _mlstm_step_kernel(*refs[:n_blocked], *refs[len(args):])

    return pl.pallas_call(
        body,
        grid=(nh, DEC_BATCH // bb),
        in_specs=in_specs,
        out_specs=[
            pl.BlockSpec((bb, ML_DV), lambda h, i: (rb + i, h)),
            pl.BlockSpec((None, bb, None, ML_DK, ML_DV), lambda h, i: (lj, i, h, 0, 0)),
            pl.BlockSpec((bb, ML_DK), lambda h, i: (i, h)),
            pl.BlockSpec((None, bb, LANE), lambda h, i: (h, i, 0)),
        ],
        out_shape=[
            jax.ShapeDtypeStruct((ROWS_ALL, nh * ML_DV), BF16),
            jax.ShapeDtypeStruct(c_in.shape, F32),
            jax.ShapeDtypeStruct((DEC_BATCH, nh * ML_DK), F32),
            jax.ShapeDtypeStruct((nh, DEC_BATCH, LANE), F32),
        ],
        scratch_shapes=[pltpu.VMEM((ML_DK, DEC_BATCH), F32), pltpu.VMEM((bb, ML_DV), F32)],
        input_output_aliases=aliases,
        compiler_params=pltpu.CompilerParams(
            dimension_semantics=("arbitrary", "arbitrary"), vmem_limit_bytes=VMEM_LIMIT),
        name="mlstm_step_sample",
    )(*args)


def kernel(x_prompt, x_sample, state_s5_re, state_s5_im, state_mlstm_C, state_mlstm_n, state_mlstm_m,
           ln_g, ln_b, ffn_w_gate, ffn_w_up, ffn_w_down,
           s5_a_re, s5_a_im, s5_log_dt, s5_b_re, s5_b_im, s5_c_re, s5_c_im, s5_d, s5_w_a, s5_w_b,
           ml_w_in, ml_b_i, ml_b_f, ml_norm_g, ml_w_out):
    n_s5 = s5_a_re.shape[0]
    n_ml = ml_w_in.shape[0]

    ffn_f32 = (ffn_w_gate, ffn_w_up, ffn_w_down)
    ffn_w = tuple(w[0, 0].astype(BF16) for w in ffn_f32)
    glu_a = s5_w_a.astype(BF16)
    glu_b = s5_w_b.astype(BF16)
    w_in = ml_w_in.astype(BF16)
    w_in_gate = jnp.pad(ml_w_in[:, :, ML_QKVO:].astype(BF16), ((0, 0), (0, 0), (0, LANE - 2 * ML_HEADS)))
    w_out = ml_w_out.astype(BF16)
    ln_g4 = ln_g.reshape(DEPTH, 3, 1, D_MODEL)
    ln_b4 = ln_b.reshape(DEPTH, 3, 1, D_MODEL)
    ml_bias = jnp.pad(jnp.concatenate([ml_b_i, ml_b_f], axis=1), ((0, 0), (0, LANE - 2 * ML_HEADS)))
    ml_ng = ml_norm_g.reshape(n_ml, 1, ML_HEADS * ML_DV)
    st_r = state_s5_re.reshape(n_s5, DEC_BATCH, S5_LANES)
    st_i = state_s5_im.reshape(n_s5, DEC_BATCH, S5_LANES)
    st_n = state_mlstm_n.reshape(n_ml, DEC_BATCH, ML_HEADS * ML_DK)

    x = _to_rows(x_prompt, x_sample.reshape(DEC_BATCH, D_MODEL))

    p_s5r, p_s5i, s_s5r, s_s5i = [], [], [], []
    p_c, p_n, p_m, s_n, s_m = [], [], [], [], []
    s_c = None
    zero_state = jnp.zeros((8, S5_LANES), F32)
    for li in range(DEPTH):
        x, ffn_w = _ffn_ln(x, ffn_w, ln_g4, ln_b4, li, 0, ffn_f32, (li, 1))
        j = li // 2
        if li % 2 == 0:
            abr, abi, bbr, bbi = _s5_discretise(s5_a_re[j], s5_a_im[j], s5_log_dt[j], s5_b_re[j], s5_b_im[j])
            wbr, wbi = _block_diag_in(bbr), _block_diag_in(bbi)
            wcr, wci = _block_diag_out(s5_c_re[j]), _block_diag_out(s5_c_im[j])
            d_row = s5_d[j].reshape(1, D_MODEL)
            z, hp = _s5_scan_prompt(x, zero_state, abr, abi, wbr, wbi, wcr, wci, d_row)
            z, hsr, hsi = _s5_step_sample(x, st_r, st_i, j, abr, abi, wbr, wbi, wcr, wci, d_row, z)
            p_s5r.append(hp[:BATCH].reshape(BATCH, S5_GROUPS, S5_STATE))
            p_s5i.append(hp[BATCH:].reshape(BATCH, S5_GROUPS, S5_STATE))
            s_s5r.append(hsr.reshape(DEC_BATCH, S5_GROUPS, S5_STATE))
            s_s5i.append(hsi.reshape(DEC_BATCH, S5_GROUPS, S5_STATE))
            x = _proj_ln(x, z, (glu_a, glu_b), ln_g4, ln_b4, j, li)
        else:
            proj, gates = _in_proj(x, w_in, w_in_gate, j)
            bias_row = ml_bias[j].reshape(1, LANE)
            z, cp, npr, mp = _mlstm_prompt(proj, gates, bias_row, ml_ng, j)
            z, s_c, ns, ms = _mlstm_sample(proj, gates, bias_row, ml_ng, state_mlstm_C, st_n,
                                           state_mlstm_m, z, s_c, j)
            p_c.append(cp)
            p_n.append(npr)
            p_m.append(mp[:, :ML_HEADS, 0])
            s_n.append(ns.reshape(DEC_BATCH, ML_HEADS, ML_DK))
            s_m.append(jnp.swapaxes(ms[:, :, 0], 0, 1))
            x = _proj_ln(x, z, (w_out,), ln_g4, ln_b4, j, li)
        x, ffn_w = _ffn_ln(x, ffn_w, ln_g4, ln_b4, li, 2, ffn_f32, (li + 1, 0) if li + 1 < DEPTH else None)

    y_prompt, y_sample = _from_rows(x)
    y_sample = y_sample.reshape(DEC_BATCH, 1, D_MODEL)
    return (y_prompt, y_sample,
            jnp.stack(p_s5r), jnp.stack(p_s5i), jnp.stack(p_c), jnp.stack(p_n), jnp.stack(p_m),
            jnp.stack(s_s5r), jnp.stack(s_s5i), s_c, jnp.stack(s_n), jnp.stack(s_m))
```

```python
import functools

import jax
import jax.numpy as jnp
from jax import lax
from jax.experimental import pallas as pl
from jax.experimental.pallas import tpu as pltpu

F32 = jnp.float32
BF16 = jnp.bfloat16

D_MODEL = 2048
BATCH = 4
SEQ = 2048
DEPTH = 4
DEC_BATCH = 128
D_FF = 5504
S5_GROUPS = 128
S5_GROUP = 16
S5_STATE = 64
S5_LANES = S5_GROUPS * S5_STATE
ML_HEADS = 4
ML_DK = 256
ML_DV = 512
ML_CHUNK = 128
ML_QKVO = 2 * ML_HEADS * ML_DK + 2 * ML_HEADS * ML_DV
DN_ALPHA = (2.0 * DEPTH) ** 0.25
LN_EPS = 1e-5
K_SCALE = ML_DK ** -0.5

LANE = 128
ROWS_PROMPT = SEQ * BATCH
ROWS_ALL = ROWS_PROMPT + DEC_BATCH
TM = 640
TM_FF = 832
TF = 512
CV_BLOCKS = 8
TN = 512
PROJ_LN_GROUPS = 2
TM_IN = 1040
TN_IN = 1024
S5_OCT = 8
S5_NOCT = S5_GROUPS // S5_OCT
S5_OCT_CH = S5_OCT * S5_GROUP
S5_OCT_ST = S5_OCT * S5_STATE
S5_STEPS = 256
S5_PAIR = 4
RL_STEPS = 1024
ML_DEC_BB = 16
VMEM_LIMIT = 56 * 1024 * 1024


def _dot(a, b):
    return jnp.dot(a, b, preferred_element_type=F32)


def _layer_norm_rows(y, g, b):
    mu = jnp.mean(y, axis=-1, keepdims=True)
    yc = y - mu
    var = jnp.mean(yc * yc, axis=-1, keepdims=True)
    return yc * lax.rsqrt(var + LN_EPS) * g + b


def _row_groups(n_rows, n_groups):
    step = -(-n_rows // n_groups)
    step = -(-step // 16) * 16
    return [slice(lo, min(lo + step, n_rows)) for lo in range(0, n_rows, step)]


def _log_sigmoid(x):
    return jnp.minimum(x, 0.0) - jnp.log1p(jnp.exp(-jnp.abs(x)))


def _to_rows_kernel(xp_ref, xs_ref, o_ref, *, nt):
    r = pl.program_id(0)

    @pl.when(r < nt)
    def _():
        for b in range(BATCH):
            o_ref[pl.ds(b, RL_STEPS, stride=BATCH), :] = xp_ref[b]

    @pl.when(r == nt)
    def _():
        o_ref[0:DEC_BATCH, :] = xs_ref[...]


def _to_rows(x_prompt, x_sample):
    nt = SEQ // RL_STEPS
    ns = D_MODEL // LANE
    return pl.pallas_call(
        functools.partial(_to_rows_kernel, nt=nt),
        grid=(nt + 1, ns),
        in_specs=[
            pl.BlockSpec((BATCH, RL_STEPS, LANE),
                         lambda r, s: (0, jnp.minimum(r, nt - 1), jnp.where(r == nt, ns - 1, s))),
            pl.BlockSpec((DEC_BATCH, LANE), lambda r, s: (0, s)),
        ],
        out_specs=pl.BlockSpec((RL_STEPS * BATCH, LANE), lambda r, s: (r, s)),
        out_shape=jax.ShapeDtypeStruct((ROWS_ALL, D_MODEL), F32),
        compiler_params=pltpu.CompilerParams(
            dimension_semantics=("arbitrary", "arbitrary"), vmem_limit_bytes=VMEM_LIMIT),
        name="to_rows",
    )(x_prompt, x_sample)


def _from_rows_kernel(x_ref, yp_ref, ys_ref, *, nt):
    r = pl.program_id(0)

    @pl.when(r < nt)
    def _():
        for b in range(BATCH):
            yp_ref[b] = x_ref[pl.ds(b, RL_STEPS, stride=BATCH), :]

    @pl.when(r == nt)
    def _():
        ys_ref[...] = x_ref[0:DEC_BATCH, :]


def _from_rows(x):
    nt = SEQ // RL_STEPS
    ns = D_MODEL // LANE
    return pl.pallas_call(
        functools.partial(_from_rows_kernel, nt=nt),
        grid=(nt + 1, ns),
        in_specs=[pl.BlockSpec((RL_STEPS * BATCH, LANE), lambda r, s: (r, s))],
        out_specs=[
            pl.BlockSpec((BATCH, RL_STEPS, LANE),
                         lambda r, s: (0, jnp.minimum(r, nt - 1), jnp.where(r == nt, ns - 1, s))),
            pl.BlockSpec((DEC_BATCH, LANE), lambda r, s: (0, jnp.where(r == nt, s, 0))),
        ],
        out_shape=[jax.ShapeDtypeStruct((BATCH, SEQ, D_MODEL), F32),
                   jax.ShapeDtypeStruct((DEC_BATCH, D_MODEL), F32)],
        compiler_params=pltpu.CompilerParams(
            dimension_semantics=("arbitrary", "arbitrary"), vmem_limit_bytes=VMEM_LIMIT),
        name="from_rows",
    )(x)


def _ffn_kernel(x_ref, wg_ref, wu_ref, wd_ref, g_ref, b_ref, *refs, nf, convert_next):
    if convert_next:
        ng_ref, nu_ref, nd_ref, o_ref, cg_ref, cu_ref, cd_ref, xb_ref = refs
    else:
        o_ref, xb_ref = refs
    f = pl.program_id(1)

    @pl.when(f == 0)
    def _():
        x = x_ref[...]
        xb_ref[...] = x.astype(BF16)
        o_ref[...] = (2.0 * DN_ALPHA) * x

    def hidden_tile(width):
        xb = xb_ref[...]
        hg = _dot(xb, wg_ref[:, :width])
        hu = _dot(xb, wu_ref[:, :width])
        act = (hg * jax.nn.sigmoid(hg) * hu).astype(BF16)
        o_ref[...] += _dot(act, wd_ref[:width, :])
        if convert_next:
            cg_ref[:, :width] = ng_ref[:, :width].astype(BF16)
            cu_ref[:, :width] = nu_ref[:, :width].astype(BF16)
            cd_ref[:width, :] = nd_ref[:width, :].astype(BF16)

    @pl.when(f < nf - 1)
    def _():
        hidden_tile(TF)

    @pl.when(f == nf - 1)
    def _():
        hidden_tile(D_FF - (nf - 1) * TF)
        o_ref[...] = _layer_norm_rows(0.5 * o_ref[...], g_ref[...], b_ref[...])


def _ffn_ln(x, w_cur, ln_g, ln_b, li, kn, w_f32, nxt):
    m = x.shape[0]
    nf = pl.cdiv(D_FF, TF)
    n_row_tiles = m // TM_FF
    cv = D_MODEL // CV_BLOCKS
    assert n_row_tiles >= CV_BLOCKS
    in_specs = [
        pl.BlockSpec((TM_FF, D_MODEL), lambda i, f: (i, 0)),
        pl.BlockSpec((D_MODEL, TF), lambda i, f: (0, f)),
        pl.BlockSpec((D_MODEL, TF), lambda i, f: (0, f)),
        pl.BlockSpec((TF, D_MODEL), lambda i, f: (f, 0)),
        pl.BlockSpec((None, None, 1, D_MODEL), lambda i, f: (li, kn, 0, 0)),
        pl.BlockSpec((None, None, 1, D_MODEL), lambda i, f: (li, kn, 0, 0)),
    ]
    out_specs = [pl.BlockSpec((TM_FF, D_MODEL), lambda i, f: (i, 0))]
    out_shape = [jax.ShapeDtypeStruct((m, D_MODEL), F32)]
    args = [x, *w_cur, ln_g, ln_b]
    if nxt is not None:
        nli, nk = nxt

        def cv_row(i, f):
            return jnp.minimum(i, CV_BLOCKS - 1)

        def cv_col(i, f):
            return jnp.where(i < CV_BLOCKS, f, nf - 1)

        in_specs += [
            pl.BlockSpec((None, None, cv, TF), lambda i, f: (nli, nk, cv_row(i, f), cv_col(i, f))),
            pl.BlockSpec((None, None, cv, TF), lambda i, f: (nli, nk, cv_row(i, f), cv_col(i, f))),
            pl.BlockSpec((None, None, TF, cv), lambda i, f: (nli, nk, cv_col(i, f), cv_row(i, f))),
        ]
        out_specs += [
            pl.BlockSpec((cv, TF), lambda i, f: (cv_row(i, f), cv_col(i, f))),
            pl.BlockSpec((cv, TF), lambda i, f: (cv_row(i, f), cv_col(i, f))),
            pl.BlockSpec((TF, cv), lambda i, f: (cv_col(i, f), cv_row(i, f))),
        ]
        out_shape += [jax.ShapeDtypeStruct((D_MODEL, D_FF), BF16), jax.ShapeDtypeStruct((D_MODEL, D_FF), BF16),
                      jax.ShapeDtypeStruct((D_FF, D_MODEL), BF16)]
        args += list(w_f32)
    outs = pl.pallas_call(
        functools.partial(_ffn_kernel, nf=nf, convert_next=nxt is not None),
        grid=(n_row_tiles, nf),
        in_specs=in_specs,
        out_specs=out_specs,
        out_shape=out_shape,
        scratch_shapes=[pltpu.VMEM((TM_FF, D_MODEL), BF16)],
        compiler_params=pltpu.CompilerParams(
            dimension_semantics=("arbitrary", "arbitrary"), vmem_limit_bytes=VMEM_LIMIT),
        name="ffn_ln",
    )(*args)
    return outs[0], (tuple(outs[1:]) if nxt is not None else None)


def _proj_ln_kernel(x_ref, z_ref, *refs, gated, nn):
    w_refs, (g_ref, b_ref, o_ref, mix_ref) = refs[:-4], refs[-4:]
    n = pl.program_id(1)

    def mix_tile(rows):
        z = z_ref[rows, :]
        p = _dot(z, w_refs[0][...])
        if gated:
            p = p * jax.nn.sigmoid(_dot(z, w_refs[1][...]))
        return p

    @pl.when(n < nn - 1)
    def _():
        mix_ref[n] = mix_tile(slice(None))

    @pl.when(n == nn - 1)
    def _():
        def finish(rows, last_tile):
            tiles = [mix_ref[j, rows, :] for j in range(nn - 1)] + [last_tile]
            ys = [DN_ALPHA * x_ref[rows, j * TN:(j + 1) * TN] + tiles[j] for j in range(nn)]
            mu = sum(jnp.sum(y, axis=-1, keepdims=True) for y in ys) * (1.0 / D_MODEL)
            var = sum(jnp.sum((y - mu) * (y - mu), axis=-1, keepdims=True) for y in ys) * (1.0 / D_MODEL)
            r = lax.rsqrt(var + LN_EPS)
            for j in range(nn):
                sl = slice(j * TN, (j + 1) * TN)
                o_ref[rows, sl] = (ys[j] - mu) * r * g_ref[:, sl] + b_ref[:, sl]

        groups = _row_groups(o_ref.shape[0], PROJ_LN_GROUPS)
        pending = None
        for rows in groups:
            tile = mix_tile(rows)
            if pending is not None:
                finish(*pending)
            pending = (rows, tile)
        finish(*pending)


def _proj_ln(x, z, ws, ln_g, ln_b, lj, li):
    m = x.shape[0]
    nn = D_MODEL // TN
    gated = len(ws) == 2
    row_spec = pl.BlockSpec((TM, D_MODEL), lambda i, n: (i, 0))
    w_spec = pl.BlockSpec((None, D_MODEL, TN), lambda i, n: (lj, 0, n))
    ln_spec = pl.BlockSpec((None, None, 1, D_MODEL), lambda i, n: (li, 1, 0, 0))
    return pl.pallas_call(
        functools.partial(_proj_ln_kernel, gated=gated, nn=nn),
        grid=(m // TM, nn),
        in_specs=[row_spec, row_spec] + [w_spec] * len(ws) + [ln_spec, ln_spec],
        out_specs=row_spec,
        out_shape=jax.ShapeDtypeStruct((m, D_MODEL), F32),
        scratch_shapes=[pltpu.VMEM((nn - 1, TM, TN), F32)],
        compiler_params=pltpu.CompilerParams(
            dimension_semantics=("parallel", "arbitrary"), vmem_limit_bytes=VMEM_LIMIT),
        name="glu_ln" if gated else "out_ln",
    )(x, z, *ws, ln_g, ln_b)


def _in_proj_kernel(x_ref, w_ref, wg_ref, p_ref, gt_ref, xb_ref):
    n = pl.program_id(1)

    @pl.when(n == 0)
    def _():
        xb = x_ref[...].astype(BF16)
        xb_ref[...] = xb
        gt_ref[...] = _dot(xb, wg_ref[...])

    p = _dot(xb_ref[...], w_ref[...])
    for s in range(TN_IN // LANE):
        p_ref[s] = p[:, s * LANE:(s + 1) * LANE]


def _in_proj(x, w_in, w_gate, lj):
    m = x.shape[0]
    slabs = TN_IN // LANE
    return pl.pallas_call(
        _in_proj_kernel,
        grid=(m // TM_IN, ML_QKVO // TN_IN),
        in_specs=[
            pl.BlockSpec((TM_IN, D_MODEL), lambda i, n: (i, 0)),
            pl.BlockSpec((None, D_MODEL, TN_IN), lambda i, n: (lj, 0, n)),
            pl.BlockSpec((None, D_MODEL, LANE), lambda i, n: (lj, 0, 0)),
        ],
        out_specs=[
            pl.BlockSpec((slabs, TM_IN, LANE), lambda i, n: (n, i, 0)),
            pl.BlockSpec((TM_IN, LANE), lambda i, n: (i, 0)),
        ],
        out_shape=[jax.ShapeDtypeStruct((ML_QKVO // LANE, m, LANE), F32), jax.ShapeDtypeStruct((m, LANE), F32)],
        scratch_shapes=[pltpu.VMEM((TM_IN, D_MODEL), BF16)],
        compiler_params=pltpu.CompilerParams(
            dimension_semantics=("parallel", "arbitrary"), vmem_limit_bytes=VMEM_LIMIT),
        name="ml_in_proj",
    )(x, w_in, w_gate)


def _s5_prep_kernel(ar_ref, ai_ref, ldt_ref, br_ref, bi_ref, cr_ref, ci_ref, a1_ref, a2_ref, wb_ref, wc_ref):
    ch, st = S5_OCT_CH, S5_OCT_ST
    ar = ar_ref[...]
    ai = ai_ref[...]
    dt = jnp.exp(ldt_ref[...])
    mag = jnp.exp(dt * ar)
    abr = mag * jnp.cos(dt * ai)
    abi = mag * jnp.sin(dt * ai)
    zr = abr - 1.0
    zi = abi
    den = ar * ar + ai * ai
    gr = (zr * ar + zi * ai) / den
    gi = (zi * ar - zr * ai) / den
    br = br_ref[...]
    bi = bi_ref[...]
    bbr = gr * br - gi * bi
    bbi = gr * bi + gi * br

    def lanes(x):
        return jnp.concatenate([x[g] for g in range(S5_OCT)], axis=1)

    upper = lax.broadcasted_iota(jnp.int32, (8, st), 0) < BATCH
    a1_ref[...] = jnp.broadcast_to(lanes(abr), (8, st))
    a2_ref[...] = jnp.where(upper, -lanes(abi), lanes(abi))

    def block_diag(x, block_rows, block_cols):
        flat = x.reshape(S5_OCT * block_rows, block_cols)
        rep = jnp.concatenate([flat] * S5_OCT, axis=1)
        r = lax.broadcasted_iota(jnp.int32, rep.shape, 0) // block_rows
        c = lax.broadcasted_iota(jnp.int32, rep.shape, 1) // block_cols
        return jnp.where(r == c, rep, 0.0).astype(BF16)

    wb_ref[:ch, :] = block_diag(bbr, S5_GROUP, S5_STATE)
    wb_ref[ch:, :] = block_diag(bbi, S5_GROUP, S5_STATE)
    wc_ref[:, :ch] = block_diag(cr_ref[...], S5_STATE, S5_GROUP)
    wc_ref[:, ch:] = block_diag(-ci_ref[...], S5_STATE, S5_GROUP)


def _s5_prepare(a_re, a_im, log_dt, b_re, b_im, c_re, c_im):
    n_l = a_re.shape[0]
    n = n_l * S5_NOCT
    p, c = S5_STATE, S5_GROUP
    a_spec = pl.BlockSpec((None, S5_OCT, 1, p), lambda s: (s, 0, 0, 0))
    b_spec = pl.BlockSpec((None, S5_OCT, c, p), lambda s: (s, 0, 0, 0))
    c_spec = pl.BlockSpec((None, S5_OCT, p, c), lambda s: (s, 0, 0, 0))
    coef_spec = pl.BlockSpec((None, 8, S5_OCT_ST), lambda s: (s // S5_NOCT, 0, s % S5_NOCT))
    return pl.pallas_call(
        _s5_prep_kernel,
        grid=(n,),
        in_specs=[a_spec, a_spec, pl.BlockSpec((None, S5_OCT, 1, 1), lambda s: (s, 0, 0, 0)),
                  b_spec, b_spec, c_spec, c_spec],
        out_specs=[coef_spec, coef_spec,
                   pl.BlockSpec((None, 2 * S5_OCT_CH, S5_OCT_ST), lambda s: (s, 0, 0)),
                   pl.BlockSpec((None, S5_OCT_ST, 2 * S5_OCT_CH), lambda s: (s, 0, 0))],
        out_shape=[jax.ShapeDtypeStruct((n_l, 8, S5_LANES), F32), jax.ShapeDtypeStruct((n_l, 8, S5_LANES), F32),
                   jax.ShapeDtypeStruct((n, 2 * S5_OCT_CH, S5_OCT_ST), BF16),
                   jax.ShapeDtypeStruct((n, S5_OCT_ST, 2 * S5_OCT_CH), BF16)],
        compiler_params=pltpu.CompilerParams(dimension_semantics=("arbitrary",)),
        name="s5_prepare",
    )(a_re.reshape(n, S5_OCT, 1, p), a_im.reshape(n, S5_OCT, 1, p), log_dt.reshape(n, S5_OCT, 1, 1),
      jnp.swapaxes(b_re, 2, 3).reshape(n, S5_OCT, c, p), jnp.swapaxes(b_im, 2, 3).reshape(n, S5_OCT, c, p),
      jnp.swapaxes(c_re, 2, 3).reshape(n, S5_OCT, p, c), jnp.swapaxes(c_im, 2, 3).reshape(n, S5_OCT, p, c))


def _s5_seq_kernel(*refs, nsteps, nchunks):
    u_refs = refs[:S5_PAIR]
    (h0_ref, a1_ref, a2_ref, wb_ref, wc_ref, d_ref, z_ref, hf_ref,
     lhs_l, lhs_r, x8_s, zl_s, zr_s, zo_s, st_s) = refs[S5_PAIR:]
    c = pl.program_id(1)
    ch, st = S5_OCT_CH, S5_OCT_ST

    def seq_steps(b, stride, half=0):
        return pl.ds(half * BATCH + b, nsteps, stride=stride)

    @pl.when(c == 0)
    def _():
        st_s[...] = h0_ref[...]
        lhs_l[...] = jnp.zeros_like(lhs_l)
        lhs_r[...] = jnp.zeros_like(lhs_r)

    for o in range(S5_PAIR):
        for b in range(BATCH):
            ub = u_refs[o][seq_steps(b, BATCH), :]
            lhs_l[o, seq_steps(b, 8, 0), :] = ub
            lhs_r[o, seq_steps(b, 8, 1), :] = ub
        lhs = jnp.concatenate([lhs_l[o], lhs_r[o]], axis=1).astype(BF16)
        x8_s[:, o * st:(o + 1) * st] = _dot(lhs, wb_ref[o])

    a1 = a1_ref[...]
    a2 = a2_ref[...]

    def step(t, h):
        r = pl.multiple_of(t * 8, 8)
        hn = (a1 * h + x8_s[pl.ds(r, 8), :]) + a2 * pltpu.roll(h, BATCH, 0)
        x8_s[pl.ds(r, 8), :] = hn
        return hn

    h_last = lax.fori_loop(0, nsteps, step, st_s[...], unroll=4)
    st_s[...] = h_last

    for o in range(S5_PAIR):
        zz = _dot(x8_s[:, o * st:(o + 1) * st].astype(BF16), wc_ref[o])
        zl_s[...] = zz[:, :ch]
        zr_s[...] = zz[:, ch:]
        for b in range(BATCH):
            y = zl_s[seq_steps(b, 8, 0), :] + zr_s[seq_steps(b, 8, 1), :]
            y = y + d_ref[:, o * ch:(o + 1) * ch] * u_refs[o][seq_steps(b, BATCH), :]
            zo_s[seq_steps(b, BATCH), :] = jax.nn.gelu(y)
        z_ref[:, o * ch:(o + 1) * ch] = zo_s[...].astype(BF16)

    @pl.when(c == nchunks - 1)
    def _():
        hf_ref[...] = h_last


def _s5_scan_prompt(x_all, h0, a1, a2, wb, wc, d_row, lj):
    rows = S5_STEPS * BATCH
    nchunks = SEQ // S5_STEPS
    pc, ps = S5_PAIR * S5_OCT_CH, S5_PAIR * S5_OCT_ST
    groups = S5_NOCT // S5_PAIR
    st_spec = pl.BlockSpec((8, ps), lambda p, c: (0, p))
    coef_spec = pl.BlockSpec((None, 8, ps), lambda p, c: (lj, 0, p))
    return pl.pallas_call(
        functools.partial(_s5_seq_kernel, nsteps=S5_STEPS, nchunks=nchunks),
        grid=(groups, nchunks),
        in_specs=[pl.BlockSpec((rows, S5_OCT_CH), lambda p, c, o=o: (c, S5_PAIR * p + o)) for o in range(S5_PAIR)] + [
            st_spec, coef_spec, coef_spec,
            pl.BlockSpec((S5_PAIR, 2 * S5_OCT_CH, S5_OCT_ST), lambda p, c: (lj * groups + p, 0, 0)),
            pl.BlockSpec((S5_PAIR, S5_OCT_ST, 2 * S5_OCT_CH), lambda p, c: (lj * groups + p, 0, 0)),
            pl.BlockSpec((1, pc), lambda p, c: (0, p)),
        ],
        out_specs=[pl.BlockSpec((rows, pc), lambda p, c: (c, p)), st_spec],
        out_shape=[jax.ShapeDtypeStruct((ROWS_ALL, D_MODEL), BF16),
                   jax.ShapeDtypeStruct((8, S5_LANES), F32)],
        scratch_shapes=[pltpu.VMEM((S5_PAIR, 2 * rows, S5_OCT_CH), F32), pltpu.VMEM((S5_PAIR, 2 * rows, S5_OCT_CH), F32),
                        pltpu.VMEM((2 * rows, ps), F32),
                        pltpu.VMEM((2 * rows, S5_OCT_CH), F32), pltpu.VMEM((2 * rows, S5_OCT_CH), F32),
                        pltpu.VMEM((rows, S5_OCT_CH), F32), pltpu.VMEM((8, ps), F32)],
        compiler_params=pltpu.CompilerParams(
            dimension_semantics=("parallel", "arbitrary"), vmem_limit_bytes=VMEM_LIMIT),
        name="s5_scan_prompt",
    )(*([x_all] * S5_PAIR), h0, a1, a2, wb, wc, d_row)


def _s5_step_kernel(u_ref, h0r_ref, h0i_ref, a1_ref, a2_ref, wb_ref, wc_ref, d_ref,
                    z_all_ref, z_ref, hr_ref, hi_ref):
    del z_all_ref
    ch = S5_OCT_CH
    u = u_ref[...]
    ub = u.astype(BF16)
    xr = _dot(ub, wb_ref[:ch, :])
    xi = _dot(ub, wb_ref[ch:, :])
    ar = a1_ref[0:1, :]
    ai = a2_ref[BATCH:BATCH + 1, :]
    h0r = h0r_ref[...]
    h0i = h0i_ref[...]
    hr = xr + (ar * h0r - ai * h0i)
    hi = xi + (ar * h0i + ai * h0r)
    hr_ref[...] = hr
    hi_ref[...] = hi
    y = _dot(hr.astype(BF16), wc_ref[:, :ch]) + _dot(hi.astype(BF16), wc_ref[:, ch:])
    y = y + d_ref[...] * u
    z_ref[...] = jax.nn.gelu(y).astype(BF16)


def _s5_step_sample(x_all, h0r, h0i, lj, a1, a2, wb, wc, d_row, z_all):
    row_blk = ROWS_PROMPT // DEC_BATCH
    st_in = pl.BlockSpec((None, DEC_BATCH, S5_OCT_ST), lambda o: (lj, 0, o))
    st_out = pl.BlockSpec((DEC_BATCH, S5_OCT_ST), lambda o: (0, o))
    coef_spec = pl.BlockSpec((None, 8, S5_OCT_ST), lambda o: (lj, 0, o))
    return pl.pallas_call(
        _s5_step_kernel,
        grid=(S5_NOCT,),
        in_specs=[
            pl.BlockSpec((DEC_BATCH, S5_OCT_CH), lambda o: (row_blk, o)),
            st_in, st_in,
            coef_spec, coef_spec,
            pl.BlockSpec((None, 2 * S5_OCT_CH, S5_OCT_ST), lambda o: (lj * S5_NOCT + o, 0, 0)),
            pl.BlockSpec((None, S5_OCT_ST, 2 * S5_OCT_CH), lambda o: (lj * S5_NOCT + o, 0, 0)),
            pl.BlockSpec((1, S5_OCT_CH), lambda o: (0, o)),
            pl.BlockSpec(memory_space=pl.ANY),
        ],
        out_specs=[pl.BlockSpec((DEC_BATCH, S5_OCT_CH), lambda o: (row_blk, o)), st_out, st_out],
        out_shape=[jax.ShapeDtypeStruct((ROWS_ALL, D_MODEL), BF16),
                   jax.ShapeDtypeStruct((DEC_BATCH, S5_LANES), F32),
                   jax.ShapeDtypeStruct((DEC_BATCH, S5_LANES), F32)],
        input_output_aliases={8: 0},
        compiler_params=pltpu.CompilerParams(
            dimension_semantics=("parallel",), vmem_limit_bytes=VMEM_LIMIT),
        name="s5_step_sample",
    )(x_all, h0r, h0i, a1, a2, wb, wc, d_row, z_all)


def _mlstm_seq_kernel(q_ref, k_ref, v_ref, o_ref, gt_ref, bias_ref, ng_ref,
                      hn_ref, cf_ref, nf_ref, mf_ref, c_s, n_s, m_s, hn_s, *, nchunks):
    ci = pl.program_id(0)
    b = pl.program_id(1)
    L = ML_CHUNK
    rows = pl.ds(b, L, stride=BATCH)

    def seq_rows(ref, h, width):
        per_head = width // LANE
        return jnp.concatenate([ref[h * per_head + j, rows, :] for j in range(per_head)], axis=1)

    @pl.when(ci == 0)
    def _():
        c_s[b] = jnp.zeros(c_s.shape[1:], F32)
        n_s[b] = jnp.zeros(n_s.shape[1:], F32)
        m_s[b] = jnp.zeros(m_s.shape[1:], F32)

    gb = gt_ref[rows, :] + bias_ref[...]
    lfa = _log_sigmoid(gb)
    gbt = gb.T
    lft = lfa.T
    row = lax.broadcasted_iota(jnp.int32, (L, L), 0)
    col = lax.broadcasted_iota(jnp.int32, (L, L), 1)
    causal = col <= row
    tril = causal.astype(F32)
    triu = (row <= col).astype(F32)
    bc_cols = jnp.dot(tril, lfa, precision=lax.Precision.HIGHEST, preferred_element_type=F32)
    bc_rows = jnp.dot(lft[0:8, :], triu, precision=lax.Precision.HIGHEST, preferred_element_type=F32)

    for h in range(ML_HEADS):
        ig_row = gbt[h:h + 1, :]
        bc_row = bc_rows[ML_HEADS + h:ML_HEADS + h + 1, :]
        ig_col = gb[:, h:h + 1]
        bc_col = bc_cols[:, ML_HEADS + h:ML_HEADS + h + 1]
        m_prev = m_s[b, h:h + 1, 0:1]
        inter = bc_col + m_prev
        dmat = jnp.where(causal, bc_col - bc_row + ig_row, -jnp.inf)
        mt = jnp.maximum(inter, jnp.max(dmat, axis=1, keepdims=True))
        wts = jnp.exp(dmat - mt)

        qf = seq_rows(q_ref, h, ML_DK)
        kf = seq_rows(k_ref, h, ML_DK) * K_SCALE
        qb = qf.astype(BF16)
        vb = seq_rows(v_ref, h, ML_DV).astype(BF16)
        qk = lax.dot_general(qb, kf.astype(BF16), (((1,), (1,)), ((), ())), preferred_element_type=F32)
        sc = qk * wts
        scale = jnp.exp(inter - mt)
        cmat = c_s[b, h]
        n_row = n_s[b, h:h + 1, :]
        num = scale * _dot(qb, cmat.astype(BF16)) + _dot(sc.astype(BF16), vb)
        qn = jnp.sum(qb.astype(F32) * n_row, axis=1, keepdims=True)
        den = scale * qn + jnp.sum(sc, axis=1, keepdims=True)
        hh = num * (1.0 / jnp.maximum(jnp.abs(den), jnp.exp(-mt)))

        mu = jnp.mean(hh, axis=1, keepdims=True)
        hc = hh - mu
        var = jnp.mean(hc * hc, axis=1, keepdims=True)
        hn = hc * lax.rsqrt(var + LN_EPS) * ng_ref[:, h * ML_DV:(h + 1) * ML_DV]
        gated = jax.nn.sigmoid(seq_rows(o_ref, h, ML_DV)) * hn
        for j in range(ML_DV // LANE):
            hn_s[h * (ML_DV // LANE) + j, rows, :] = gated[:, j * LANE:(j + 1) * LANE]

        m_last = mt[L - 1:L, :]
        b_last = bc_col[L - 1:L, :]
        dec = jnp.exp(b_last - bc_col + ig_col - m_last)
        cscale = jnp.exp(b_last + m_prev - m_last)
        kd = kf * dec
        c_new = cscale * cmat + _dot(kd.T.astype(BF16), vb)
        n_new = cscale * n_row + jnp.sum(kd, axis=0, keepdims=True)
        c_s[b, h] = c_new
        n_s[b, h:h + 1, :] = n_new
        m_s[b, h:h + 1, :] = jnp.broadcast_to(m_last, (1, LANE))

    @pl.when(ci == nchunks - 1)
    def _():
        cf_ref[...] = c_s[b]
        nf_ref[...] = n_s[b, 0:ML_HEADS, :]
        mf_ref[...] = m_s[b]

    @pl.when(b == BATCH - 1)
    def _():
        for s in range(ML_HEADS * ML_DV // LANE):
            hn_ref[:, s * LANE:(s + 1) * LANE] = hn_s[s].astype(BF16)


def _mlstm_prompt(proj, gates, bias_row, norm_g, lj):
    nchunks = SEQ // ML_CHUNK
    rows = ML_CHUNK * BATCH
    qk_s = ML_HEADS * ML_DK // LANE
    v_s = ML_HEADS * ML_DV // LANE
    v_w = ML_HEADS * ML_DV
    last = nchunks - 1

    def state_idx(c, b):
        return jnp.where(c == last, b, 0)

    return pl.pallas_call(
        functools.partial(_mlstm_seq_kernel, nchunks=nchunks),
        grid=(nchunks, BATCH),
        in_specs=[
            pl.BlockSpec((qk_s, rows, LANE), lambda c, b: (0, c, 0)),
            pl.BlockSpec((qk_s, rows, LANE), lambda c, b: (1, c, 0)),
            pl.BlockSpec((v_s, rows, LANE), lambda c, b: (1, c, 0)),
            pl.BlockSpec((v_s, rows, LANE), lambda c, b: (2, c, 0)),
            pl.BlockSpec((rows, LANE), lambda c, b: (c, 0)),
            pl.BlockSpec((1, LANE), lambda c, b: (0, 0)),
            pl.BlockSpec((None, 1, v_w), lambda c, b: (lj, 0, 0)),
        ],
        out_specs=[
            pl.BlockSpec((rows, v_w), lambda c, b: (c, 0)),
            pl.BlockSpec((None, ML_HEADS, ML_DK, ML_DV), lambda c, b: (state_idx(c, b), 0, 0, 0)),
            pl.BlockSpec((None, ML_HEADS, ML_DK), lambda c, b: (state_idx(c, b), 0, 0)),
            pl.BlockSpec((None, 8, LANE), lambda c, b: (state_idx(c, b), 0, 0)),
        ],
        out_shape=[
            jax.ShapeDtypeStruct((ROWS_ALL, v_w), BF16),
            jax.ShapeDtypeStruct((BATCH, ML_HEADS, ML_DK, ML_DV), F32),
            jax.ShapeDtypeStruct((BATCH, ML_HEADS, ML_DK), F32),
            jax.ShapeDtypeStruct((BATCH, 8, LANE), F32),
        ],
        scratch_shapes=[pltpu.VMEM((BATCH, ML_HEADS, ML_DK, ML_DV), F32), pltpu.VMEM((BATCH, 8, ML_DK), F32),
                        pltpu.VMEM((BATCH, 8, LANE), F32), pltpu.VMEM((v_s, rows, LANE), F32)],
        compiler_params=pltpu.CompilerParams(
            dimension_semantics=("arbitrary", "arbitrary"), vmem_limit_bytes=VMEM_LIMIT),
        name="mlstm_prompt",
    )(proj, proj, proj, proj, gates, bias_row, norm_g)


def _mlstm_step_kernel(q_ref, k_ref, v_ref, o_ref, ka_ref, va_ref, gt_ref, bias_ref, ng_ref,
                       c_ref, n_ref, m_ref, hn_ref, co_ref, no_ref, mo_ref, kt_s, num_s):
    h = pl.program_id(0)
    i = pl.program_id(1)
    bb = ML_DEC_BB

    def cols(ref):
        return jnp.concatenate([ref[s] for s in range(ref.shape[0])], axis=1)

    @pl.when(i == 0)
    def _():
        kt_s[...] = (cols(ka_ref) * K_SCALE).T

    gb = gt_ref[...] + bias_ref[...]
    lane = lax.broadcasted_iota(jnp.int32, (bb, LANE), 1)
    ig = jnp.sum(jnp.where(lane == h, gb, 0.0), axis=1, keepdims=True)
    fp = jnp.sum(jnp.where(lane == h + ML_HEADS, gb, 0.0), axis=1, keepdims=True)
    lf = _log_sigmoid(fp)
    m4 = m_ref[...]
    hl = lax.broadcasted_iota(jnp.int32, (bb, ML_HEADS), 1)
    m_prev = jnp.sum(jnp.where(hl == h, m4, 0.0), axis=1, keepdims=True)
    inter = lf + m_prev
    mt = jnp.maximum(inter, ig)
    wts = jnp.exp(ig - mt)
    scale = jnp.exp(inter - mt)

    qb = cols(q_ref).astype(BF16)
    qf = qb.astype(F32)
    kf = cols(k_ref) * K_SCALE
    vf = cols(v_ref).astype(BF16).astype(F32)
    n_prev = n_ref[...]
    qk = jnp.sum(qf * kf.astype(BF16).astype(F32), axis=1, keepdims=True)
    sc = qk * wts
    qn = jnp.sum(qf * n_prev, axis=1, keepdims=True)
    den = scale * qn + sc
    rden = 1.0 / jnp.maximum(jnp.abs(den), jnp.exp(-mt))

    ktb = kt_s[...].astype(BF16)
    va = cols(va_ref)
    rowi = lax.broadcasted_iota(jnp.int32, (DEC_BATCH, ML_DV), 0)
    for j in range(bb):
        cj = c_ref[j]
        qc = _dot(qb, cj.astype(BF16))[j:j + 1, :]
        vsel = jnp.where(rowi == i * bb + j, va, 0.0).astype(BF16)
        outer = _dot(ktb, vsel)
        co_ref[j] = scale[j:j + 1, :] * cj + wts[j:j + 1, :] * outer
        num_s[j:j + 1, :] = scale[j:j + 1, :] * qc + sc[j:j + 1, :] * vf[j:j + 1, :]

    hh = num_s[...] * rden
    mu = jnp.mean(hh, axis=1, keepdims=True)
    hc = hh - mu
    var = jnp.mean(hc * hc, axis=1, keepdims=True)
    hn = hc * lax.rsqrt(var + LN_EPS) * ng_ref[...]
    hn_ref[...] = (jax.nn.sigmoid(cols(o_ref)) * hn).astype(BF16)
    no_ref[...] = scale * n_prev + wts * kf
    mo_ref[...] = jnp.broadcast_to(mt, (bb, LANE))


def _mlstm_sample(proj, gates, bias_row, norm_g, c_in, n_in, m_in, hn_all, c_acc, lj):
    bb = ML_DEC_BB
    rb = ROWS_PROMPT // bb
    r128 = ROWS_PROMPT // DEC_BATCH
    nh = ML_HEADS
    ks = ML_DK // LANE
    vs = ML_DV // LANE
    in_specs = [
        pl.BlockSpec((ks, bb, LANE), lambda h, i: (h, rb + i, 0)),
        pl.BlockSpec((ks, bb, LANE), lambda h, i: (nh + h, rb + i, 0)),
        pl.BlockSpec((vs, bb, LANE), lambda h, i: (nh + h, rb + i, 0)),
        pl.BlockSpec((vs, bb, LANE), lambda h, i: (2 * nh + h, rb + i, 0)),
        pl.BlockSpec((ks, DEC_BATCH, LANE), lambda h, i: (nh + h, r128, 0)),
        pl.BlockSpec((vs, DEC_BATCH, LANE), lambda h, i: (nh + h, r128, 0)),
        pl.BlockSpec((bb, LANE), lambda h, i: (rb + i, 0)),
        pl.BlockSpec((1, LANE), lambda h, i: (0, 0)),
        pl.BlockSpec((None, 1, ML_DV), lambda h, i: (lj, 0, h)),
        pl.BlockSpec((None, bb, None, ML_DK, ML_DV), lambda h, i: (lj, i, h, 0, 0)),
        pl.BlockSpec((None, bb, ML_DK), lambda h, i: (lj, i, h)),
        pl.BlockSpec((None, bb, nh), lambda h, i: (lj, i, 0)),
        pl.BlockSpec(memory_space=pl.ANY),
    ]
    args = [proj, proj, proj, proj, proj, proj, gates, bias_row, norm_g, c_in, n_in, m_in, hn_all]
    n_blocked = 12
    aliases = {n_blocked: 0}
    if c_acc is not None:
        in_specs.append(pl.BlockSpec(memory_space=pl.ANY))
        args.append(c_acc)
        aliases[n_blocked + 1] = 1

    def body(*refs):
        _mlstm_step_kernel(*refs[:n_blocked], *refs[len(args):])

    return pl.pallas_call(
        body,
        grid=(nh, DEC_BATCH // bb),
        in_specs=in_specs,
        out_specs=[
            pl.BlockSpec((bb, ML_DV), lambda h, i: (rb + i, h)),
            pl.BlockSpec((None, bb, None, ML_DK, ML_DV), lambda h, i: (lj, i, h, 0, 0)),
            pl.BlockSpec((bb, ML_DK), lambda h, i: (i, h)),
            pl.BlockSpec((None, bb, LANE), lambda h, i: (h, i, 0)),
        ],
        out_shape=[
            jax.ShapeDtypeStruct((ROWS_ALL, nh * ML_DV), BF16),
            jax.ShapeDtypeStruct(c_in.shape, F32),
            jax.ShapeDtypeStruct((DEC_BATCH, nh * ML_DK), F32),
            jax.ShapeDtypeStruct((nh, DEC_BATCH, LANE), F32),
        ],
        scratch_shapes=[pltpu.VMEM((ML_DK, DEC_BATCH), F32), pltpu.VMEM((bb, ML_DV), F32)],
        input_output_aliases=aliases,
        compiler_params=pltpu.CompilerParams(
            dimension_semantics=("arbitrary", "arbitrary"), vmem_limit_bytes=VMEM_LIMIT),
        name="mlstm_step_sample",
    )(*args)


def kernel(x_prompt, x_sample, state_s5_re, state_s5_im, state_mlstm_C, state_mlstm_n, state_mlstm_m,
           ln_g, ln_b, ffn_w_gate, ffn_w_up, ffn_w_down,
           s5_a_re, s5_a_im, s5_log_dt, s5_b_re, s5_b_im, s5_c_re, s5_c_im, s5_d, s5_w_a, s5_w_b,
           ml_w_in, ml_b_i, ml_b_f, ml_norm_g, ml_w_out):
    n_s5 = s5_a_re.shape[0]
    n_ml = ml_w_in.shape[0]

    ffn_f32 = (ffn_w_gate, ffn_w_up, ffn_w_down)
    ffn_w = tuple(w[0, 0].astype(BF16) for w in ffn_f32)
    glu_a = s5_w_a.astype(BF16)
    glu_b = s5_w_b.astype(BF16)
    w_in = ml_w_in.astype(BF16)
    w_in_gate = jnp.pad(ml_w_in[:, :, ML_QKVO:].astype(BF16), ((0, 0), (0, 0), (0, LANE - 2 * ML_HEADS)))
    w_out = ml_w_out.astype(BF16)
    ln_g4 = ln_g.reshape(DEPTH, 3, 1, D_MODEL)
    ln_b4 = ln_b.reshape(DEPTH, 3, 1, D_MODEL)
    ml_bias = jnp.pad(jnp.concatenate([ml_b_i, ml_b_f], axis=1), ((0, 0), (0, LANE - 2 * ML_HEADS)))
    ml_ng = ml_norm_g.reshape(n_ml, 1, ML_HEADS * ML_DV)
    st_r = state_s5_re.reshape(n_s5, DEC_BATCH, S5_LANES)
    st_i = state_s5_im.reshape(n_s5, DEC_BATCH, S5_LANES)
    st_n = state_mlstm_n.reshape(n_ml, DEC_BATCH, ML_HEADS * ML_DK)

    s5_a1, s5_a2, s5_wb, s5_wc = _s5_prepare(s5_a_re, s5_a_im, s5_log_dt, s5_b_re, s5_b_im, s5_c_re, s5_c_im)

    x = _to_rows(x_prompt, x_sample.reshape(DEC_BATCH, D_MODEL))

    p_s5r, p_s5i, s_s5r, s_s5i = [], [], [], []
    p_c, p_n, p_m, s_n, s_m = [], [], [], [], []
    s_c = None
    zero_state = jnp.zeros((8, S5_LANES), F32)
    for li in range(DEPTH):
        x, ffn_w = _ffn_ln(x, ffn_w, ln_g4, ln_b4, li, 0, ffn_f32, (li, 1))
        j = li // 2
        if li % 2 == 0:
            d_row = s5_d[j].reshape(1, D_MODEL)
            z, hp = _s5_scan_prompt(x, zero_state, s5_a1, s5_a2, s5_wb, s5_wc, d_row, j)
            z, hsr, hsi = _s5_step_sample(x, st_r, st_i, j, s5_a1, s5_a2, s5_wb, s5_wc, d_row, z)
            p_s5r.append(hp[:BATCH].reshape(BATCH, S5_GROUPS, S5_STATE))
            p_s5i.append(hp[BATCH:].reshape(BATCH, S5_GROUPS, S5_STATE))
            s_s5r.append(hsr.reshape(DEC_BATCH, S5_GROUPS, S5_STATE))
            s_s5i.append(hsi.reshape(DEC_BATCH, S5_GROUPS, S5_STATE))
            x = _proj_ln(x, z, (glu_a, glu_b), ln_g4, ln_b4, j, li)
        else:
            proj, gates = _in_proj(x, w_in, w_in_gate, j)
            bias_row = ml_bias[j].reshape(1, LANE)
            z, cp, npr, mp = _mlstm_prompt(proj, gates, bias_row, ml_ng, j)
            z, s_c, ns, ms = _mlstm_sample(proj, gates, bias_row, ml_ng, state_mlstm_C, st_n,
                                           state_mlstm_m, z, s_c, j)
            p_c.append(cp)
            p_n.append(npr)
            p_m.append(mp[:, :ML_HEADS, 0])
            s_n.append(ns.reshape(DEC_BATCH, ML_HEADS, ML_DK))
            s_m.append(jnp.swapaxes(ms[:, :, 0], 0, 1))
            x = _proj_ln(x, z, (w_out,), ln_g4, ln_b4, j, li)
        x, ffn_w = _ffn_ln(x, ffn_w, ln_g4, ln_b4, li, 2, ffn_f32, (li + 1, 0) if li + 1 < DEPTH else None)

    y_prompt, y_sample = _from_rows(x)
    y_sample = y_sample.reshape(DEC_BATCH, 1, D_MODEL)
    return (y_prompt, y_sample,
            jnp.stack(p_s5r), jnp.stack(p_s5i), jnp.stack(p_c), jnp.stack(p_n), jnp.stack(p_m),
            jnp.stack(s_s5r), jnp.stack(s_s5i), s_c, jnp.stack(s_n), jnp.stack(s_m))
```

```python
import functools

import jax
import jax.numpy as jnp
from jax import lax
from jax.experimental import pallas as pl
from jax.experimental.pallas import tpu as pltpu

F32 = jnp.float32
BF16 = jnp.bfloat16

D_MODEL = 2048
BATCH = 4
SEQ = 2048
DEPTH = 4
DEC_BATCH = 128
D_FF = 5504
S5_GROUPS = 128
S5_GROUP = 16
S5_STATE = 64
S5_LANES = S5_GROUPS * S5_STATE
ML_HEADS = 4
ML_DK = 256
ML_DV = 512
ML_CHUNK = 128
ML_QKVO = 2 * ML_HEADS * ML_DK + 2 * ML_HEADS * ML_DV
DN_ALPHA = (2.0 * DEPTH) ** 0.25
LN_EPS = 1e-5
K_SCALE = ML_DK ** -0.5

LANE = 128
ROWS_PROMPT = SEQ * BATCH
ROWS_ALL = ROWS_PROMPT + DEC_BATCH
TM = 640
TM_FF = 832
TF = 512
CV_BLOCKS = 8
TN = 512
PROJ_LN_GROUPS = 2
TM_IN = 1040
TN_IN = 1024
S5_OCT = 8
S5_NOCT = S5_GROUPS // S5_OCT
S5_OCT_CH = S5_OCT * S5_GROUP
S5_OCT_ST = S5_OCT * S5_STATE
S5_STEPS = 256
S5_PAIR = 4
RL_STEPS = 1024
ML_DEC_BB = 16
VMEM_LIMIT = 56 * 1024 * 1024


def _dot(a, b):
    return jnp.dot(a, b, preferred_element_type=F32)


def _layer_norm_rows(y, g, b):
    mu = jnp.mean(y, axis=-1, keepdims=True)
    yc = y - mu
    var = jnp.mean(yc * yc, axis=-1, keepdims=True)
    return yc * lax.rsqrt(var + LN_EPS) * g + b


def _row_groups(n_rows, n_groups):
    step = -(-n_rows // n_groups)
    step = -(-step // 16) * 16
    return [slice(lo, min(lo + step, n_rows)) for lo in range(0, n_rows, step)]


def _log_sigmoid(x):
    return jnp.minimum(x, 0.0) - jnp.log1p(jnp.exp(-jnp.abs(x)))


def _to_rows_kernel(xp_ref, xs_ref, o_ref, *, nt):
    r = pl.program_id(0)

    @pl.when(r < nt)
    def _():
        for b in range(BATCH):
            o_ref[pl.ds(b, RL_STEPS, stride=BATCH), :] = xp_ref[b]

    @pl.when(r == nt)
    def _():
        o_ref[0:DEC_BATCH, :] = xs_ref[...]


def _to_rows(x_prompt, x_sample):
    nt = SEQ // RL_STEPS
    ns = D_MODEL // LANE
    return pl.pallas_call(
        functools.partial(_to_rows_kernel, nt=nt),
        grid=(nt + 1, ns),
        in_specs=[
            pl.BlockSpec((BATCH, RL_STEPS, LANE),
                         lambda r, s: (0, jnp.minimum(r, nt - 1), jnp.where(r == nt, ns - 1, s))),
            pl.BlockSpec((DEC_BATCH, LANE), lambda r, s: (0, s)),
        ],
        out_specs=pl.BlockSpec((RL_STEPS * BATCH, LANE), lambda r, s: (r, s)),
        out_shape=jax.ShapeDtypeStruct((ROWS_ALL, D_MODEL), F32),
        compiler_params=pltpu.CompilerParams(
            dimension_semantics=("arbitrary", "arbitrary"), vmem_limit_bytes=VMEM_LIMIT),
        name="to_rows",
    )(x_prompt, x_sample)


def _from_rows_kernel(x_ref, yp_ref, ys_ref, *, nt):
    r = pl.program_id(0)

    @pl.when(r < nt)
    def _():
        for b in range(BATCH):
            yp_ref[b] = x_ref[pl.ds(b, RL_STEPS, stride=BATCH), :]

    @pl.when(r == nt)
    def _():
        ys_ref[...] = x_ref[0:DEC_BATCH, :]


def _from_rows(x):
    nt = SEQ // RL_STEPS
    ns = D_MODEL // LANE
    return pl.pallas_call(
        functools.partial(_from_rows_kernel, nt=nt),
        grid=(nt + 1, ns),
        in_specs=[pl.BlockSpec((RL_STEPS * BATCH, LANE), lambda r, s: (r, s))],
        out_specs=[
            pl.BlockSpec((BATCH, RL_STEPS, LANE),
                         lambda r, s: (0, jnp.minimum(r, nt - 1), jnp.where(r == nt, ns - 1, s))),
            pl.BlockSpec((DEC_BATCH, LANE), lambda r, s: (0, jnp.where(r == nt, s, 0))),
        ],
        out_shape=[jax.ShapeDtypeStruct((BATCH, SEQ, D_MODEL), F32),
                   jax.ShapeDtypeStruct((DEC_BATCH, D_MODEL), F32)],
        compiler_params=pltpu.CompilerParams(
            dimension_semantics=("arbitrary", "arbitrary"), vmem_limit_bytes=VMEM_LIMIT),
        name="from_rows",
    )(x)


def _ffn_kernel(x_ref, wg_ref, wu_ref, wd_ref, g_ref, b_ref, *refs, nf, convert_next):
    if convert_next:
        ng_ref, nu_ref, nd_ref, o_ref, cg_ref, cu_ref, cd_ref, xb_ref = refs
    else:
        o_ref, xb_ref = refs
    f = pl.program_id(1)

    def start_row_tile():
        x = x_ref[...]
        xb_ref[...] = x.astype(BF16)
        o_ref[...] = (2.0 * DN_ALPHA) * x

    def hidden_tile(width):
        xb = xb_ref[...]
        hg = _dot(xb, wg_ref[:, :width])
        hu = _dot(xb, wu_ref[:, :width])
        act = (hg * jax.nn.sigmoid(hg) * hu).astype(BF16)
        o_ref[...] += _dot(act, wd_ref[:width, :])
        if convert_next:
            cg_ref[:, :width] = ng_ref[:, :width].astype(BF16)
            cu_ref[:, :width] = nu_ref[:, :width].astype(BF16)
            cd_ref[:width, :] = nd_ref[:width, :].astype(BF16)

    @pl.when(f == 0)
    def _():
        start_row_tile()
        hidden_tile(TF)

    @pl.when(jnp.logical_and(f > 0, f < nf - 1))
    def _():
        hidden_tile(TF)

    @pl.when(f == nf - 1)
    def _():
        hidden_tile(D_FF - (nf - 1) * TF)
        o_ref[...] = _layer_norm_rows(0.5 * o_ref[...], g_ref[...], b_ref[...])


def _ffn_ln(x, w_cur, ln_g, ln_b, li, kn, w_f32, nxt):
    m = x.shape[0]
    nf = pl.cdiv(D_FF, TF)
    n_row_tiles = m // TM_FF
    cv = D_MODEL // CV_BLOCKS
    assert n_row_tiles >= CV_BLOCKS
    in_specs = [
        pl.BlockSpec((TM_FF, D_MODEL), lambda i, f: (i, 0)),
        pl.BlockSpec((D_MODEL, TF), lambda i, f: (0, f)),
        pl.BlockSpec((D_MODEL, TF), lambda i, f: (0, f)),
        pl.BlockSpec((TF, D_MODEL), lambda i, f: (f, 0)),
        pl.BlockSpec((None, None, 1, D_MODEL), lambda i, f: (li, kn, 0, 0)),
        pl.BlockSpec((None, None, 1, D_MODEL), lambda i, f: (li, kn, 0, 0)),
    ]
    out_specs = [pl.BlockSpec((TM_FF, D_MODEL), lambda i, f: (i, 0))]
    out_shape = [jax.ShapeDtypeStruct((m, D_MODEL), F32)]
    args = [x, *w_cur, ln_g, ln_b]
    if nxt is not None:
        nli, nk = nxt

        def cv_row(i, f):
            return jnp.minimum(i, CV_BLOCKS - 1)

        def cv_col(i, f):
            return jnp.where(i < CV_BLOCKS, f, nf - 1)

        in_specs += [
            pl.BlockSpec((None, None, cv, TF), lambda i, f: (nli, nk, cv_row(i, f), cv_col(i, f))),
            pl.BlockSpec((None, None, cv, TF), lambda i, f: (nli, nk, cv_row(i, f), cv_col(i, f))),
            pl.BlockSpec((None, None, TF, cv), lambda i, f: (nli, nk, cv_col(i, f), cv_row(i, f))),
        ]
        out_specs += [
            pl.BlockSpec((cv, TF), lambda i, f: (cv_row(i, f), cv_col(i, f))),
            pl.BlockSpec((cv, TF), lambda i, f: (cv_row(i, f), cv_col(i, f))),
            pl.BlockSpec((TF, cv), lambda i, f: (cv_col(i, f), cv_row(i, f))),
        ]
        out_shape += [jax.ShapeDtypeStruct((D_MODEL, D_FF), BF16), jax.ShapeDtypeStruct((D_MODEL, D_FF), BF16),
                      jax.ShapeDtypeStruct((D_FF, D_MODEL), BF16)]
        args += list(w_f32)
    outs = pl.pallas_call(
        functools.partial(_ffn_kernel, nf=nf, convert_next=nxt is not None),
        grid=(n_row_tiles, nf),
        in_specs=in_specs,
        out_specs=out_specs,
        out_shape=out_shape,
        scratch_shapes=[pltpu.VMEM((TM_FF, D_MODEL), BF16)],
        compiler_params=pltpu.CompilerParams(
            dimension_semantics=("arbitrary", "arbitrary"), vmem_limit_bytes=VMEM_LIMIT),
        name="ffn_ln",
    )(*args)
    return outs[0], (tuple(outs[1:]) if nxt is not None else None)


def _proj_ln_kernel(x_ref, z_ref, *refs, gated, nn):
    w_refs, (g_ref, b_ref, o_ref, mix_ref) = refs[:-4], refs[-4:]
    n = pl.program_id(1)

    def mix_tile(rows):
        z = z_ref[rows, :]
        p = _dot(z, w_refs[0][...])
        if gated:
            p = p * jax.nn.sigmoid(_dot(z, w_refs[1][...]))
        return p

    @pl.when(n < nn - 1)
    def _():
        mix_ref[n] = mix_tile(slice(None))

    @pl.when(n == nn - 1)
    def _():
        def finish(rows, last_tile):
            tiles = [mix_ref[j, rows, :] for j in range(nn - 1)] + [last_tile]
            ys = [DN_ALPHA * x_ref[rows, j * TN:(j + 1) * TN] + tiles[j] for j in range(nn)]
            mu = sum(jnp.sum(y, axis=-1, keepdims=True) for y in ys) * (1.0 / D_MODEL)
            var = sum(jnp.sum((y - mu) * (y - mu), axis=-1, keepdims=True) for y in ys) * (1.0 / D_MODEL)
            r = lax.rsqrt(var + LN_EPS)
            for j in range(nn):
                sl = slice(j * TN, (j + 1) * TN)
                o_ref[rows, sl] = (ys[j] - mu) * r * g_ref[:, sl] + b_ref[:, sl]

        groups = _row_groups(o_ref.shape[0], PROJ_LN_GROUPS)
        pending = None
        for rows in groups:
            tile = mix_tile(rows)
            if pending is not None:
                finish(*pending)
            pending = (rows, tile)
        finish(*pending)


def _proj_ln(x, z, ws, ln_g, ln_b, lj, li):
    m = x.shape[0]
    nn = D_MODEL // TN
    gated = len(ws) == 2
    row_spec = pl.BlockSpec((TM, D_MODEL), lambda i, n: (i, 0))
    w_spec = pl.BlockSpec((None, D_MODEL, TN), lambda i, n: (lj, 0, n))
    ln_spec = pl.BlockSpec((None, None, 1, D_MODEL), lambda i, n: (li, 1, 0, 0))
    return pl.pallas_call(
        functools.partial(_proj_ln_kernel, gated=gated, nn=nn),
        grid=(m // TM, nn),
        in_specs=[row_spec, row_spec] + [w_spec] * len(ws) + [ln_spec, ln_spec],
        out_specs=row_spec,
        out_shape=jax.ShapeDtypeStruct((m, D_MODEL), F32),
        scratch_shapes=[pltpu.VMEM((nn - 1, TM, TN), F32)],
        compiler_params=pltpu.CompilerParams(
            dimension_semantics=("parallel", "arbitrary"), vmem_limit_bytes=VMEM_LIMIT),
        name="glu_ln" if gated else "out_ln",
    )(x, z, *ws, ln_g, ln_b)


def _in_proj_kernel(x_ref, w_ref, wg_ref, p_ref, gt_ref, xb_ref):
    n = pl.program_id(1)

    @pl.when(n == 0)
    def _():
        xb = x_ref[...].astype(BF16)
        xb_ref[...] = xb
        gt_ref[...] = _dot(xb, wg_ref[...])

    p = _dot(xb_ref[...], w_ref[...])
    for s in range(TN_IN // LANE):
        p_ref[s] = p[:, s * LANE:(s + 1) * LANE]


def _in_proj(x, w_in, w_gate, lj):
    m = x.shape[0]
    slabs = TN_IN // LANE
    return pl.pallas_call(
        _in_proj_kernel,
        grid=(m // TM_IN, ML_QKVO // TN_IN),
        in_specs=[
            pl.BlockSpec((TM_IN, D_MODEL), lambda i, n: (i, 0)),
            pl.BlockSpec((None, D_MODEL, TN_IN), lambda i, n: (lj, 0, n)),
            pl.BlockSpec((None, D_MODEL, LANE), lambda i, n: (lj, 0, 0)),
        ],
        out_specs=[
            pl.BlockSpec((slabs, TM_IN, LANE), lambda i, n: (n, i, 0)),
            pl.BlockSpec((TM_IN, LANE), lambda i, n: (i, 0)),
        ],
        out_shape=[jax.ShapeDtypeStruct((ML_QKVO // LANE, m, LANE), F32), jax.ShapeDtypeStruct((m, LANE), F32)],
        scratch_shapes=[pltpu.VMEM((TM_IN, D_MODEL), BF16)],
        compiler_params=pltpu.CompilerParams(
            dimension_semantics=("parallel", "arbitrary"), vmem_limit_bytes=VMEM_LIMIT),
        name="ml_in_proj",
    )(x, w_in, w_gate)


def _s5_prep_kernel(ar_ref, ai_ref, ldt_ref, br_ref, bi_ref, cr_ref, ci_ref, a1_ref, a2_ref, wb_ref, wc_ref):
    ch, st = S5_OCT_CH, S5_OCT_ST
    ar = ar_ref[...]
    ai = ai_ref[...]
    dt = jnp.exp(ldt_ref[...])
    mag = jnp.exp(dt * ar)
    abr = mag * jnp.cos(dt * ai)
    abi = mag * jnp.sin(dt * ai)
    zr = abr - 1.0
    zi = abi
    den = ar * ar + ai * ai
    gr = (zr * ar + zi * ai) / den
    gi = (zi * ar - zr * ai) / den
    br = br_ref[...]
    bi = bi_ref[...]
    bbr = gr * br - gi * bi
    bbi = gr * bi + gi * br

    def lanes(x):
        return jnp.concatenate([x[g] for g in range(S5_OCT)], axis=1)

    upper = lax.broadcasted_iota(jnp.int32, (8, st), 0) < BATCH
    a1_ref[...] = jnp.broadcast_to(lanes(abr), (8, st))
    a2_ref[...] = jnp.where(upper, -lanes(abi), lanes(abi))

    def block_diag(x, block_rows, block_cols):
        flat = x.reshape(S5_OCT * block_rows, block_cols)
        rep = jnp.concatenate([flat] * S5_OCT, axis=1)
        r = lax.broadcasted_iota(jnp.int32, rep.shape, 0) // block_rows
        c = lax.broadcasted_iota(jnp.int32, rep.shape, 1) // block_cols
        return jnp.where(r == c, rep, 0.0).astype(BF16)

    wb_ref[:ch, :] = block_diag(bbr, S5_GROUP, S5_STATE)
    wb_ref[ch:, :] = block_diag(bbi, S5_GROUP, S5_STATE)
    wc_ref[:, :ch] = block_diag(cr_ref[...], S5_STATE, S5_GROUP)
    wc_ref[:, ch:] = block_diag(-ci_ref[...], S5_STATE, S5_GROUP)


def _s5_prepare(a_re, a_im, log_dt, b_re, b_im, c_re, c_im):
    n_l = a_re.shape[0]
    n = n_l * S5_NOCT
    p, c = S5_STATE, S5_GROUP
    a_spec = pl.BlockSpec((None, S5_OCT, 1, p), lambda s: (s, 0, 0, 0))
    b_spec = pl.BlockSpec((None, S5_OCT, c, p), lambda s: (s, 0, 0, 0))
    c_spec = pl.BlockSpec((None, S5_OCT, p, c), lambda s: (s, 0, 0, 0))
    coef_spec = pl.BlockSpec((None, 8, S5_OCT_ST), lambda s: (s // S5_NOCT, 0, s % S5_NOCT))
    return pl.pallas_call(
        _s5_prep_kernel,
        grid=(n,),
        in_specs=[a_spec, a_spec, pl.BlockSpec((None, S5_OCT, 1, 1), lambda s: (s, 0, 0, 0)),
                  b_spec, b_spec, c_spec, c_spec],
        out_specs=[coef_spec, coef_spec,
                   pl.BlockSpec((None, 2 * S5_OCT_CH, S5_OCT_ST), lambda s: (s, 0, 0)),
                   pl.BlockSpec((None, S5_OCT_ST, 2 * S5_OCT_CH), lambda s: (s, 0, 0))],
        out_shape=[jax.ShapeDtypeStruct((n_l, 8, S5_LANES), F32), jax.ShapeDtypeStruct((n_l, 8, S5_LANES), F32),
                   jax.ShapeDtypeStruct((n, 2 * S5_OCT_CH, S5_OCT_ST), BF16),
                   jax.ShapeDtypeStruct((n, S5_OCT_ST, 2 * S5_OCT_CH), BF16)],
        compiler_params=pltpu.CompilerParams(dimension_semantics=("arbitrary",)),
        name="s5_prepare",
    )(a_re.reshape(n, S5_OCT, 1, p), a_im.reshape(n, S5_OCT, 1, p), log_dt.reshape(n, S5_OCT, 1, 1),
      jnp.swapaxes(b_re, 2, 3).reshape(n, S5_OCT, c, p), jnp.swapaxes(b_im, 2, 3).reshape(n, S5_OCT, c, p),
      jnp.swapaxes(c_re, 2, 3).reshape(n, S5_OCT, p, c), jnp.swapaxes(c_im, 2, 3).reshape(n, S5_OCT, p, c))


def _s5_seq_kernel(*refs, nsteps, nchunks):
    u_refs = refs[:S5_PAIR]
    (h0_ref, a1_ref, a2_ref, wb_ref, wc_ref, d_ref, z_ref, hf_ref,
     lhs_l, lhs_r, x8_s, zl_s, zr_s, zo_s, st_s) = refs[S5_PAIR:]
    c = pl.program_id(1)
    ch, st = S5_OCT_CH, S5_OCT_ST

    def seq_steps(b, stride, half=0):
        return pl.ds(half * BATCH + b, nsteps, stride=stride)

    @pl.when(c == 0)
    def _():
        st_s[...] = h0_ref[...]
        lhs_l[...] = jnp.zeros_like(lhs_l)
        lhs_r[...] = jnp.zeros_like(lhs_r)

    for o in range(S5_PAIR):
        for b in range(BATCH):
            ub = u_refs[o][seq_steps(b, BATCH), :]
            lhs_l[o, seq_steps(b, 8, 0), :] = ub
            lhs_r[o, seq_steps(b, 8, 1), :] = ub
        lhs = jnp.concatenate([lhs_l[o], lhs_r[o]], axis=1).astype(BF16)
        x8_s[:, o * st:(o + 1) * st] = _dot(lhs, wb_ref[o])

    a1 = a1_ref[...]
    a2 = a2_ref[...]

    def step(t, h):
        r = pl.multiple_of(t * 8, 8)
        hn = (a1 * h + x8_s[pl.ds(r, 8), :]) + a2 * pltpu.roll(h, BATCH, 0)
        x8_s[pl.ds(r, 8), :] = hn
        return hn

    h_last = lax.fori_loop(0, nsteps, step, st_s[...], unroll=4)
    st_s[...] = h_last

    for o in range(S5_PAIR):
        zz = _dot(x8_s[:, o * st:(o + 1) * st].astype(BF16), wc_ref[o])
        zl_s[...] = zz[:, :ch]
        zr_s[...] = zz[:, ch:]
        for b in range(BATCH):
            y = zl_s[seq_steps(b, 8, 0), :] + zr_s[seq_steps(b, 8, 1), :]
            y = y + d_ref[:, o * ch:(o + 1) * ch] * u_refs[o][seq_steps(b, BATCH), :]
            zo_s[seq_steps(b, BATCH), :] = jax.nn.gelu(y)
        z_ref[:, o * ch:(o + 1) * ch] = zo_s[...].astype(BF16)

    @pl.when(c == nchunks - 1)
    def _():
        hf_ref[...] = h_last


def _s5_scan_prompt(x_all, h0, a1, a2, wb, wc, d_row, lj):
    rows = S5_STEPS * BATCH
    nchunks = SEQ // S5_STEPS
    pc, ps = S5_PAIR * S5_OCT_CH, S5_PAIR * S5_OCT_ST
    groups = S5_NOCT // S5_PAIR
    st_spec = pl.BlockSpec((8, ps), lambda p, c: (0, p))
    coef_spec = pl.BlockSpec((None, 8, ps), lambda p, c: (lj, 0, p))
    return pl.pallas_call(
        functools.partial(_s5_seq_kernel, nsteps=S5_STEPS, nchunks=nchunks),
        grid=(groups, nchunks),
        in_specs=[pl.BlockSpec((rows, S5_OCT_CH), lambda p, c, o=o: (c, S5_PAIR * p + o)) for o in range(S5_PAIR)] + [
            st_spec, coef_spec, coef_spec,
            pl.BlockSpec((S5_PAIR, 2 * S5_OCT_CH, S5_OCT_ST), lambda p, c: (lj * groups + p, 0, 0)),
            pl.BlockSpec((S5_PAIR, S5_OCT_ST, 2 * S5_OCT_CH), lambda p, c: (lj * groups + p, 0, 0)),
            pl.BlockSpec((1, pc), lambda p, c: (0, p)),
        ],
        out_specs=[pl.BlockSpec((rows, pc), lambda p, c: (c, p)), st_spec],
        out_shape=[jax.ShapeDtypeStruct((ROWS_ALL, D_MODEL), BF16),
                   jax.ShapeDtypeStruct((8, S5_LANES), F32)],
        scratch_shapes=[pltpu.VMEM((S5_PAIR, 2 * rows, S5_OCT_CH), F32), pltpu.VMEM((S5_PAIR, 2 * rows, S5_OCT_CH), F32),
                        pltpu.VMEM((2 * rows, ps), F32),
                        pltpu.VMEM((2 * rows, S5_OCT_CH), F32), pltpu.VMEM((2 * rows, S5_OCT_CH), F32),
                        pltpu.VMEM((rows, S5_OCT_CH), F32), pltpu.VMEM((8, ps), F32)],
        compiler_params=pltpu.CompilerParams(
            dimension_semantics=("parallel", "arbitrary"), vmem_limit_bytes=VMEM_LIMIT),
        name="s5_scan_prompt",
    )(*([x_all] * S5_PAIR), h0, a1, a2, wb, wc, d_row)


def _s5_step_kernel(u_ref, h0r_ref, h0i_ref, a1_ref, a2_ref, wb_ref, wc_ref, d_ref,
                    z_all_ref, z_ref, hr_ref, hi_ref):
    del z_all_ref
    ch = S5_OCT_CH
    u = u_ref[...]
    ub = u.astype(BF16)
    xr = _dot(ub, wb_ref[:ch, :])
    xi = _dot(ub, wb_ref[ch:, :])
    ar = a1_ref[0:1, :]
    ai = a2_ref[BATCH:BATCH + 1, :]
    h0r = h0r_ref[...]
    h0i = h0i_ref[...]
    hr = xr + (ar * h0r - ai * h0i)
    hi = xi + (ar * h0i + ai * h0r)
    hr_ref[...] = hr
    hi_ref[...] = hi
    y = _dot(hr.astype(BF16), wc_ref[:, :ch]) + _dot(hi.astype(BF16), wc_ref[:, ch:])
    y = y + d_ref[...] * u
    z_ref[...] = jax.nn.gelu(y).astype(BF16)


def _s5_step_sample(x_all, h0r, h0i, lj, a1, a2, wb, wc, d_row, z_all):
    row_blk = ROWS_PROMPT // DEC_BATCH
    st_in = pl.BlockSpec((None, DEC_BATCH, S5_OCT_ST), lambda o: (lj, 0, o))
    st_out = pl.BlockSpec((DEC_BATCH, S5_OCT_ST), lambda o: (0, o))
    coef_spec = pl.BlockSpec((None, 8, S5_OCT_ST), lambda o: (lj, 0, o))
    return pl.pallas_call(
        _s5_step_kernel,
        grid=(S5_NOCT,),
        in_specs=[
            pl.BlockSpec((DEC_BATCH, S5_OCT_CH), lambda o: (row_blk, o)),
            st_in, st_in,
            coef_spec, coef_spec,
            pl.BlockSpec((None, 2 * S5_OCT_CH, S5_OCT_ST), lambda o: (lj * S5_NOCT + o, 0, 0)),
            pl.BlockSpec((None, S5_OCT_ST, 2 * S5_OCT_CH), lambda o: (lj * S5_NOCT + o, 0, 0)),
            pl.BlockSpec((1, S5_OCT_CH), lambda o: (0, o)),
            pl.BlockSpec(memory_space=pl.ANY),
        ],
        out_specs=[pl.BlockSpec((DEC_BATCH, S5_OCT_CH), lambda o: (row_blk, o)), st_out, st_out],
        out_shape=[jax.ShapeDtypeStruct((ROWS_ALL, D_MODEL), BF16),
                   jax.ShapeDtypeStruct((DEC_BATCH, S5_LANES), F32),
                   jax.ShapeDtypeStruct((DEC_BATCH, S5_LANES), F32)],
        input_output_aliases={8: 0},
        compiler_params=pltpu.CompilerParams(
            dimension_semantics=("parallel",), vmem_limit_bytes=VMEM_LIMIT),
        name="s5_step_sample",
    )(x_all, h0r, h0i, a1, a2, wb, wc, d_row, z_all)


def _mlstm_seq_kernel(q_ref, k_ref, v_ref, o_ref, gt_ref, bias_ref, ng_ref,
                      hn_ref, cf_ref, nf_ref, mf_ref, c_s, n_s, m_s, hn_s, *, nchunks):
    ci = pl.program_id(0)
    b = pl.program_id(1)
    L = ML_CHUNK
    rows = pl.ds(b, L, stride=BATCH)

    def seq_rows(ref, h, width):
        per_head = width // LANE
        return jnp.concatenate([ref[h * per_head + j, rows, :] for j in range(per_head)], axis=1)

    @pl.when(ci == 0)
    def _():
        c_s[b] = jnp.zeros(c_s.shape[1:], F32)
        n_s[b] = jnp.zeros(n_s.shape[1:], F32)
        m_s[b] = jnp.zeros(m_s.shape[1:], F32)

    gb = gt_ref[rows, :] + bias_ref[...]
    lfa = _log_sigmoid(gb)
    gbt = gb.T
    lft = lfa.T
    row = lax.broadcasted_iota(jnp.int32, (L, L), 0)
    col = lax.broadcasted_iota(jnp.int32, (L, L), 1)
    causal = col <= row
    tril = causal.astype(F32)
    triu = (row <= col).astype(F32)
    bc_cols = jnp.dot(tril, lfa, precision=lax.Precision.HIGHEST, preferred_element_type=F32)
    bc_rows = jnp.dot(lft[0:8, :], triu, precision=lax.Precision.HIGHEST, preferred_element_type=F32)

    for h in range(ML_HEADS):
        ig_row = gbt[h:h + 1, :]
        bc_row = bc_rows[ML_HEADS + h:ML_HEADS + h + 1, :]
        ig_col = gb[:, h:h + 1]
        bc_col = bc_cols[:, ML_HEADS + h:ML_HEADS + h + 1]
        m_prev = m_s[b, h:h + 1, 0:1]
        inter = bc_col + m_prev
        dmat = jnp.where(causal, bc_col - bc_row + ig_row, -jnp.inf)
        mt = jnp.maximum(inter, jnp.max(dmat, axis=1, keepdims=True))
        wts = jnp.exp(dmat - mt)

        qf = seq_rows(q_ref, h, ML_DK)
        kf = seq_rows(k_ref, h, ML_DK) * K_SCALE
        qb = qf.astype(BF16)
        vb = seq_rows(v_ref, h, ML_DV).astype(BF16)
        qk = lax.dot_general(qb, kf.astype(BF16), (((1,), (1,)), ((), ())), preferred_element_type=F32)
        sc = qk * wts
        scale = jnp.exp(inter - mt)
        cmat = c_s[b, h]
        n_row = n_s[b, h:h + 1, :]
        num = scale * _dot(qb, cmat.astype(BF16)) + _dot(sc.astype(BF16), vb)
        qn = jnp.sum(qb.astype(F32) * n_row, axis=1, keepdims=True)
        den = scale * qn + jnp.sum(sc, axis=1, keepdims=True)
        hh = num * (1.0 / jnp.maximum(jnp.abs(den), jnp.exp(-mt)))

        mu = jnp.mean(hh, axis=1, keepdims=True)
        hc = hh - mu
        var = jnp.mean(hc * hc, axis=1, keepdims=True)
        hn = hc * lax.rsqrt(var + LN_EPS) * ng_ref[:, h * ML_DV:(h + 1) * ML_DV]
        gated = jax.nn.sigmoid(seq_rows(o_ref, h, ML_DV)) * hn
        for j in range(ML_DV // LANE):
            hn_s[h * (ML_DV // LANE) + j, rows, :] = gated[:, j * LANE:(j + 1) * LANE]

        m_last = mt[L - 1:L, :]
        b_last = bc_col[L - 1:L, :]
        dec = jnp.exp(b_last - bc_col + ig_col - m_last)
        cscale = jnp.exp(b_last + m_prev - m_last)
        kd = kf * dec
        c_new = cscale * cmat + _dot(kd.T.astype(BF16), vb)
        n_new = cscale * n_row + jnp.sum(kd, axis=0, keepdims=True)
        c_s[b, h] = c_new
        n_s[b, h:h + 1, :] = n_new
        m_s[b, h:h + 1, :] = jnp.broadcast_to(m_last, (1, LANE))

    @pl.when(ci == nchunks - 1)
    def _():
        cf_ref[...] = c_s[b]
        nf_ref[...] = n_s[b, 0:ML_HEADS, :]
        mf_ref[...] = m_s[b]

    @pl.when(b == BATCH - 1)
    def _():
        for s in range(ML_HEADS * ML_DV // LANE):
            hn_ref[:, s * LANE:(s + 1) * LANE] = hn_s[s].astype(BF16)


def _mlstm_prompt(proj, gates, bias_row, norm_g, lj):
    nchunks = SEQ // ML_CHUNK
    rows = ML_CHUNK * BATCH
    qk_s = ML_HEADS * ML_DK // LANE
    v_s = ML_HEADS * ML_DV // LANE
    v_w = ML_HEADS * ML_DV
    last = nchunks - 1

    def state_idx(c, b):
        return jnp.where(c == last, b, 0)

    return pl.pallas_call(
        functools.partial(_mlstm_seq_kernel, nchunks=nchunks),
        grid=(nchunks, BATCH),
        in_specs=[
            pl.BlockSpec((qk_s, rows, LANE), lambda c, b: (0, c, 0)),
            pl.BlockSpec((qk_s, rows, LANE), lambda c, b: (1, c, 0)),
            pl.BlockSpec((v_s, rows, LANE), lambda c, b: (1, c, 0)),
            pl.BlockSpec((v_s, rows, LANE), lambda c, b: (2, c, 0)),
            pl.BlockSpec((rows, LANE), lambda c, b: (c, 0)),
            pl.BlockSpec((1, LANE), lambda c, b: (0, 0)),
            pl.BlockSpec((None, 1, v_w), lambda c, b: (lj, 0, 0)),
        ],
        out_specs=[
            pl.BlockSpec((rows, v_w), lambda c, b: (c, 0)),
            pl.BlockSpec((None, ML_HEADS, ML_DK, ML_DV), lambda c, b: (state_idx(c, b), 0, 0, 0)),
            pl.BlockSpec((None, ML_HEADS, ML_DK), lambda c, b: (state_idx(c, b), 0, 0)),
            pl.BlockSpec((None, 8, LANE), lambda c, b: (state_idx(c, b), 0, 0)),
        ],
        out_shape=[
            jax.ShapeDtypeStruct((ROWS_ALL, v_w), BF16),
            jax.ShapeDtypeStruct((BATCH, ML_HEADS, ML_DK, ML_DV), F32),
            jax.ShapeDtypeStruct((BATCH, ML_HEADS, ML_DK), F32),
            jax.ShapeDtypeStruct((BATCH, 8, LANE), F32),
        ],
        scratch_shapes=[pltpu.VMEM((BATCH, ML_HEADS, ML_DK, ML_DV), F32), pltpu.VMEM((BATCH, 8, ML_DK), F32),
                        pltpu.VMEM((BATCH, 8, LANE), F32), pltpu.VMEM((v_s, rows, LANE), F32)],
        compiler_params=pltpu.CompilerParams(
            dimension_semantics=("arbitrary", "arbitrary"), vmem_limit_bytes=VMEM_LIMIT),
        name="mlstm_prompt",
    )(proj, proj, proj, proj, gates, bias_row, norm_g)


def _mlstm_step_kernel(q_ref, k_ref, v_ref, o_ref, ka_ref, va_ref, gt_ref, bias_ref, ng_ref,
                       c_ref, n_ref, m_ref, hn_ref, co_ref, no_ref, mo_ref, kt_s, num_s):
    h = pl.program_id(0)
    i = pl.program_id(1)
    bb = ML_DEC_BB

    def cols(ref):
        return jnp.concatenate([ref[s] for s in range(ref.shape[0])], axis=1)

    @pl.when(i == 0)
    def _():
        kt_s[...] = (cols(ka_ref) * K_SCALE).T

    gb = gt_ref[...] + bias_ref[...]
    lane = lax.broadcasted_iota(jnp.int32, (bb, LANE), 1)
    ig = jnp.sum(jnp.where(lane == h, gb, 0.0), axis=1, keepdims=True)
    fp = jnp.sum(jnp.where(lane == h + ML_HEADS, gb, 0.0), axis=1, keepdims=True)
    lf = _log_sigmoid(fp)
    m4 = m_ref[...]
    hl = lax.broadcasted_iota(jnp.int32, (bb, ML_HEADS), 1)
    m_prev = jnp.sum(jnp.where(hl == h, m4, 0.0), axis=1, keepdims=True)
    inter = lf + m_prev
    mt = jnp.maximum(inter, ig)
    wts = jnp.exp(ig - mt)
    scale = jnp.exp(inter - mt)

    qb = cols(q_ref).astype(BF16)
    qf = qb.astype(F32)
    kf = cols(k_ref) * K_SCALE
    vf = cols(v_ref).astype(BF16).astype(F32)
    n_prev = n_ref[...]
    qk = jnp.sum(qf * kf.astype(BF16).astype(F32), axis=1, keepdims=True)
    sc = qk * wts
    qn = jnp.sum(qf * n_prev, axis=1, keepdims=True)
    den = scale * qn + sc
    rden = 1.0 / jnp.maximum(jnp.abs(den), jnp.exp(-mt))

    ktb = kt_s[...].astype(BF16)
    va = cols(va_ref)
    rowi = lax.broadcasted_iota(jnp.int32, (DEC_BATCH, ML_DV), 0)
    for j in range(bb):
        cj = c_ref[j]
        qc = _dot(qb, cj.astype(BF16))[j:j + 1, :]
        vsel = jnp.where(rowi == i * bb + j, va, 0.0).astype(BF16)
        outer = _dot(ktb, vsel)
        co_ref[j] = scale[j:j + 1, :] * cj + wts[j:j + 1, :] * outer
        num_s[j:j + 1, :] = scale[j:j + 1, :] * qc + sc[j:j + 1, :] * vf[j:j + 1, :]

    hh = num_s[...] * rden
    mu = jnp.mean(hh, axis=1, keepdims=True)
    hc = hh - mu
    var = jnp.mean(hc * hc, axis=1, keepdims=True)
    hn = hc * lax.rsqrt(var + LN_EPS) * ng_ref[...]
    hn_ref[...] = (jax.nn.sigmoid(cols(o_ref)) * hn).astype(BF16)
    no_ref[...] = scale * n_prev + wts * kf
    mo_ref[...] = jnp.broadcast_to(mt, (bb, LANE))


def _mlstm_sample(proj, gates, bias_row, norm_g, c_in, n_in, m_in, hn_all, c_acc, lj):
    bb = ML_DEC_BB
    rb = ROWS_PROMPT // bb
    r128 = ROWS_PROMPT // DEC_BATCH
    nh = ML_HEADS
    ks = ML_DK // LANE
    vs = ML_DV // LANE
    in_specs = [
        pl.BlockSpec((ks, bb, LANE), lambda h, i: (h, rb + i, 0)),
        pl.BlockSpec((ks, bb, LANE), lambda h, i: (nh + h, rb + i, 0)),
        pl.BlockSpec((vs, bb, LANE), lambda h, i: (nh + h, rb + i, 0)),
        pl.BlockSpec((vs, bb, LANE), lambda h, i: (2 * nh + h, rb + i, 0)),
        pl.BlockSpec((ks, DEC_BATCH, LANE), lambda h, i: (nh + h, r128, 0)),
        pl.BlockSpec((vs, DEC_BATCH, LANE), lambda h, i: (nh + h, r128, 0)),
        pl.BlockSpec((bb, LANE), lambda h, i: (rb + i, 0)),
        pl.BlockSpec((1, LANE), lambda h, i: (0, 0)),
        pl.BlockSpec((None, 1, ML_DV), lambda h, i: (lj, 0, h)),
        pl.BlockSpec((None, bb, None, ML_DK, ML_DV), lambda h, i: (lj, i, h, 0, 0)),
        pl.BlockSpec((None, bb, ML_DK), lambda h, i: (lj, i, h)),
        pl.BlockSpec((None, bb, nh), lambda h, i: (lj, i, 0)),
        pl.BlockSpec(memory_space=pl.ANY),
    ]
    args = [proj, proj, proj, proj, proj, proj, gates, bias_row, norm_g, c_in, n_in, m_in, hn_all]
    n_blocked = 12
    aliases = {n_blocked: 0}
    if c_acc is not None:
        in_specs.append(pl.BlockSpec(memory_space=pl.ANY))
        args.append(c_acc)
        aliases[n_blocked + 1] = 1

    def body(*refs):
        _mlstm_step_kernel(*refs[:n_blocked], *refs[len(args):])

    return pl.pallas_call(
        body,
        grid=(nh, DEC_BATCH // bb),
        in_specs=in_specs,
        out_specs=[
            pl.BlockSpec((bb, ML_DV), lambda h, i: (rb + i, h)),
            pl.BlockSpec((None, bb, None, ML_DK, ML_DV), lambda h, i: (lj, i, h, 0, 0)),
            pl.BlockSpec((bb, ML_DK), lambda h, i: (i, h)),
            pl.BlockSpec((None, bb, LANE), lambda h, i: (h, i, 0)),
        ],
        out_shape=[
            jax.ShapeDtypeStruct((ROWS_ALL, nh * ML_DV), BF16),
            jax.ShapeDtypeStruct(c_in.shape, F32),
            jax.ShapeDtypeStruct((DEC_BATCH, nh * ML_DK), F32),
            jax.ShapeDtypeStruct((nh, DEC_BATCH, LANE), F32),
        ],
        scratch_shapes=[pltpu.VMEM((ML_DK, DEC_BATCH), F32), pltpu.VMEM((bb, ML_DV), F32)],
        input_output_aliases=aliases,
        compiler_params=pltpu.CompilerParams(
            dimension_semantics=("arbitrary", "arbitrary"), vmem_limit_bytes=VMEM_LIMIT),
        name="mlstm_step_sample",
    )(*args)


def kernel(x_prompt, x_sample, state_s5_re, state_s5_im, state_mlstm_C, state_mlstm_n, state_mlstm_m,
           ln_g, ln_b, ffn_w_gate, ffn_w_up, ffn_w_down,
           s5_a_re, s5_a_im, s5_log_dt, s5_b_re, s5_b_im, s5_c_re, s5_c_im, s5_d, s5_w_a, s5_w_b,
           ml_w_in, ml_b_i, ml_b_f, ml_norm_g, ml_w_out):
    n_s5 = s5_a_re.shape[0]
    n_ml = ml_w_in.shape[0]

    ffn_f32 = (ffn_w_gate, ffn_w_up, ffn_w_down)
    ffn_w = tuple(w[0, 0].astype(BF16) for w in ffn_f32)
    glu_a = s5_w_a.astype(BF16)
    glu_b = s5_w_b.astype(BF16)
    w_in = ml_w_in.astype(BF16)
    w_in_gate = jnp.pad(ml_w_in[:, :, ML_QKVO:].astype(BF16), ((0, 0), (0, 0), (0, LANE - 2 * ML_HEADS)))
    w_out = ml_w_out.astype(BF16)
    ln_g4 = ln_g.reshape(DEPTH, 3, 1, D_MODEL)
    ln_b4 = ln_b.reshape(DEPTH, 3, 1, D_MODEL)
    ml_bias = jnp.pad(jnp.concatenate([ml_b_i, ml_b_f], axis=1), ((0, 0), (0, LANE - 2 * ML_HEADS)))
    ml_ng = ml_norm_g.reshape(n_ml, 1, ML_HEADS * ML_DV)
    st_r = state_s5_re.reshape(n_s5, DEC_BATCH, S5_LANES)
    st_i = state_s5_im.reshape(n_s5, DEC_BATCH, S5_LANES)
    st_n = state_mlstm_n.reshape(n_ml, DEC_BATCH, ML_HEADS * ML_DK)

    s5_a1, s5_a2, s5_wb, s5_wc = _s5_prepare(s5_a_re, s5_a_im, s5_log_dt, s5_b_re, s5_b_im, s5_c_re, s5_c_im)

    x = _to_rows(x_prompt, x_sample.reshape(DEC_BATCH, D_MODEL))

    p_s5r, p_s5i, s_s5r, s_s5i = [], [], [], []
    p_c, p_n, p_m, s_n, s_m = [], [], [], [], []
    s_c = None
    zero_state = jnp.zeros((8, S5_LANES), F32)
    for li in range(DEPTH):
        x, ffn_w = _ffn_ln(x, ffn_w, ln_g4, ln_b4, li, 0, ffn_f32, (li, 1))
        j = li // 2
        if li % 2 == 0:
            d_row = s5_d[j].reshape(1, D_MODEL)
            z, hp = _s5_scan_prompt(x, zero_state, s5_a1, s5_a2, s5_wb, s5_wc, d_row, j)
            z, hsr, hsi = _s5_step_sample(x, st_r, st_i, j, s5_a1, s5_a2, s5_wb, s5_wc, d_row, z)
            p_s5r.append(hp[:BATCH].reshape(BATCH, S5_GROUPS, S5_STATE))
            p_s5i.append(hp[BATCH:].reshape(BATCH, S5_GROUPS, S5_STATE))
            s_s5r.append(hsr.reshape(DEC_BATCH, S5_GROUPS, S5_STATE))
            s_s5i.append(hsi.reshape(DEC_BATCH, S5_GROUPS, S5_STATE))
            x = _proj_ln(x, z, (glu_a, glu_b), ln_g4, ln_b4, j, li)
        else:
            proj, gates = _in_proj(x, w_in, w_in_gate, j)
            bias_row = ml_bias[j].reshape(1, LANE)
            z, cp, npr, mp = _mlstm_prompt(proj, gates, bias_row, ml_ng, j)
            z, s_c, ns, ms = _mlstm_sample(proj, gates, bias_row, ml_ng, state_mlstm_C, st_n,
                                           state_mlstm_m, z, s_c, j)
            p_c.append(cp)
            p_n.append(npr)
            p_m.append(mp[:, :ML_HEADS, 0])
            s_n.append(ns.reshape(DEC_BATCH, ML_HEADS, ML_DK))
            s_m.append(jnp.swapaxes(ms[:, :, 0], 0, 1))
            x = _proj_ln(x, z, (w_out,), ln_g4, ln_b4, j, li)
        x, ffn_w = _ffn_ln(x, ffn_w, ln_g4, ln_b4, li, 2, ffn_f32, (li + 1, 0) if li + 1 < DEPTH else None)

    y_prompt, y_sample = _from_rows(x)
    y_sample = y_sample.reshape(DEC_BATCH, 1, D_MODEL)
    return (y_prompt, y_sample,
            jnp.stack(p_s5r), jnp.stack(p_s5i), jnp.stack(p_c), jnp.stack(p_n), jnp.stack(p_m),
            jnp.stack(s_s5r), jnp.stack(s_s5i), s_c, jnp.stack(s_n), jnp.stack(s_m))
```

```python
import functools

import jax
import jax.numpy as jnp
from jax import lax
from jax.experimental import pallas as pl
from jax.experimental.pallas import tpu as pltpu

F32 = jnp.float32
BF16 = jnp.bfloat16

D_MODEL = 2048
BATCH = 4
SEQ = 2048
DEPTH = 4
DEC_BATCH = 128
D_FF = 5504
S5_GROUPS = 128
S5_GROUP = 16
S5_STATE = 64
S5_LANES = S5_GROUPS * S5_STATE
ML_HEADS = 4
ML_DK = 256
ML_DV = 512
ML_CHUNK = 128
ML_QKVO = 2 * ML_HEADS * ML_DK + 2 * ML_HEADS * ML_DV
DN_ALPHA = (2.0 * DEPTH) ** 0.25
LN_EPS = 1e-5
K_SCALE = ML_DK ** -0.5

LANE = 128
ROWS_PROMPT = SEQ * BATCH
ROWS_ALL = ROWS_PROMPT + DEC_BATCH
TM = 640
TM_FF = 832
TF = 512
CV_BLOCKS = 8
TN = 512
PROJ_LN_GROUPS = 2
TM_IN = 1040
TN_IN = 1024
S5_OCT = 8
S5_NOCT = S5_GROUPS // S5_OCT
S5_OCT_CH = S5_OCT * S5_GROUP
S5_OCT_ST = S5_OCT * S5_STATE
S5_STEPS = 256
S5_PAIR = 4
RL_STEPS = 1024
ML_DEC_BB = 16
ML_SEQS = 2
VMEM_LIMIT = 56 * 1024 * 1024


def _dot(a, b):
    return jnp.dot(a, b, preferred_element_type=F32)


def _layer_norm_rows(y, g, b):
    mu = jnp.mean(y, axis=-1, keepdims=True)
    yc = y - mu
    var = jnp.mean(yc * yc, axis=-1, keepdims=True)
    return yc * lax.rsqrt(var + LN_EPS) * g + b


def _row_groups(n_rows, n_groups):
    step = -(-n_rows // n_groups)
    step = -(-step // 16) * 16
    return [slice(lo, min(lo + step, n_rows)) for lo in range(0, n_rows, step)]


def _log_sigmoid(x):
    return jnp.minimum(x, 0.0) - jnp.log1p(jnp.exp(-jnp.abs(x)))


def _to_rows_kernel(xp_ref, xs_ref, o_ref, *, nt):
    r = pl.program_id(0)

    @pl.when(r < nt)
    def _():
        for b in range(BATCH):
            o_ref[pl.ds(b, RL_STEPS, stride=BATCH), :] = xp_ref[b]

    @pl.when(r == nt)
    def _():
        o_ref[0:DEC_BATCH, :] = xs_ref[...]


def _to_rows(x_prompt, x_sample):
    nt = SEQ // RL_STEPS
    ns = D_MODEL // LANE
    return pl.pallas_call(
        functools.partial(_to_rows_kernel, nt=nt),
        grid=(nt + 1, ns),
        in_specs=[
            pl.BlockSpec((BATCH, RL_STEPS, LANE),
                         lambda r, s: (0, jnp.minimum(r, nt - 1), jnp.where(r == nt, ns - 1, s))),
            pl.BlockSpec((DEC_BATCH, LANE), lambda r, s: (0, s)),
        ],
        out_specs=pl.BlockSpec((RL_STEPS * BATCH, LANE), lambda r, s: (r, s)),
        out_shape=jax.ShapeDtypeStruct((ROWS_ALL, D_MODEL), F32),
        compiler_params=pltpu.CompilerParams(
            dimension_semantics=("arbitrary", "arbitrary"), vmem_limit_bytes=VMEM_LIMIT),
        name="to_rows",
    )(x_prompt, x_sample)


def _from_rows_kernel(x_ref, yp_ref, ys_ref, *, nt):
    r = pl.program_id(0)

    @pl.when(r < nt)
    def _():
        for b in range(BATCH):
            yp_ref[b] = x_ref[pl.ds(b, RL_STEPS, stride=BATCH), :]

    @pl.when(r == nt)
    def _():
        ys_ref[...] = x_ref[0:DEC_BATCH, :]


def _from_rows(x):
    nt = SEQ // RL_STEPS
    ns = D_MODEL // LANE
    return pl.pallas_call(
        functools.partial(_from_rows_kernel, nt=nt),
        grid=(nt + 1, ns),
        in_specs=[pl.BlockSpec((RL_STEPS * BATCH, LANE), lambda r, s: (r, s))],
        out_specs=[
            pl.BlockSpec((BATCH, RL_STEPS, LANE),
                         lambda r, s: (0, jnp.minimum(r, nt - 1), jnp.where(r == nt, ns - 1, s))),
            pl.BlockSpec((DEC_BATCH, LANE), lambda r, s: (0, jnp.where(r == nt, s, 0))),
        ],
        out_shape=[jax.ShapeDtypeStruct((BATCH, SEQ, D_MODEL), F32),
                   jax.ShapeDtypeStruct((DEC_BATCH, D_MODEL), F32)],
        compiler_params=pltpu.CompilerParams(
            dimension_semantics=("arbitrary", "arbitrary"), vmem_limit_bytes=VMEM_LIMIT),
        name="from_rows",
    )(x)


def _ffn_kernel(x_ref, wg_ref, wu_ref, wd_ref, g_ref, b_ref, *refs, nf, convert_next):
    if convert_next:
        ng_ref, nu_ref, nd_ref, o_ref, cg_ref, cu_ref, cd_ref, xb_ref = refs
    else:
        o_ref, xb_ref = refs
    f = pl.program_id(1)

    def start_row_tile():
        x = x_ref[...]
        xb_ref[...] = x.astype(BF16)
        o_ref[...] = (2.0 * DN_ALPHA) * x

    def hidden_tile(width):
        xb = xb_ref[...]
        hg = _dot(xb, wg_ref[:, :width])
        hu = _dot(xb, wu_ref[:, :width])
        act = (hg * jax.nn.sigmoid(hg) * hu).astype(BF16)
        o_ref[...] += _dot(act, wd_ref[:width, :])
        if convert_next:
            cg_ref[:, :width] = ng_ref[:, :width].astype(BF16)
            cu_ref[:, :width] = nu_ref[:, :width].astype(BF16)
            cd_ref[:width, :] = nd_ref[:width, :].astype(BF16)

    @pl.when(f == 0)
    def _():
        start_row_tile()
        hidden_tile(TF)

    @pl.when(jnp.logical_and(f > 0, f < nf - 1))
    def _():
        hidden_tile(TF)

    @pl.when(f == nf - 1)
    def _():
        hidden_tile(D_FF - (nf - 1) * TF)
        o_ref[...] = _layer_norm_rows(0.5 * o_ref[...], g_ref[...], b_ref[...])


def _ffn_ln(x, w_cur, ln_g, ln_b, li, kn, w_f32, nxt):
    m = x.shape[0]
    nf = pl.cdiv(D_FF, TF)
    n_row_tiles = m // TM_FF
    cv = D_MODEL // CV_BLOCKS
    assert n_row_tiles >= CV_BLOCKS
    in_specs = [
        pl.BlockSpec((TM_FF, D_MODEL), lambda i, f: (i, 0)),
        pl.BlockSpec((D_MODEL, TF), lambda i, f: (0, f)),
        pl.BlockSpec((D_MODEL, TF), lambda i, f: (0, f)),
        pl.BlockSpec((TF, D_MODEL), lambda i, f: (f, 0)),
        pl.BlockSpec((None, None, 1, D_MODEL), lambda i, f: (li, kn, 0, 0)),
        pl.BlockSpec((None, None, 1, D_MODEL), lambda i, f: (li, kn, 0, 0)),
    ]
    out_specs = [pl.BlockSpec((TM_FF, D_MODEL), lambda i, f: (i, 0))]
    out_shape = [jax.ShapeDtypeStruct((m, D_MODEL), F32)]
    args = [x, *w_cur, ln_g, ln_b]
    if nxt is not None:
        nli, nk = nxt

        def cv_row(i, f):
            return jnp.minimum(i, CV_BLOCKS - 1)

        def cv_col(i, f):
            return jnp.where(i < CV_BLOCKS, f, nf - 1)

        in_specs += [
            pl.BlockSpec((None, None, cv, TF), lambda i, f: (nli, nk, cv_row(i, f), cv_col(i, f))),
            pl.BlockSpec((None, None, cv, TF), lambda i, f: (nli, nk, cv_row(i, f), cv_col(i, f))),
            pl.BlockSpec((None, None, TF, cv), lambda i, f: (nli, nk, cv_col(i, f), cv_row(i, f))),
        ]
        out_specs += [
            pl.BlockSpec((cv, TF), lambda i, f: (cv_row(i, f), cv_col(i, f))),
            pl.BlockSpec((cv, TF), lambda i, f: (cv_row(i, f), cv_col(i, f))),
            pl.BlockSpec((TF, cv), lambda i, f: (cv_col(i, f), cv_row(i, f))),
        ]
        out_shape += [jax.ShapeDtypeStruct((D_MODEL, D_FF), BF16), jax.ShapeDtypeStruct((D_MODEL, D_FF), BF16),
                      jax.ShapeDtypeStruct((D_FF, D_MODEL), BF16)]
        args += list(w_f32)
    outs = pl.pallas_call(
        functools.partial(_ffn_kernel, nf=nf, convert_next=nxt is not None),
        grid=(n_row_tiles, nf),
        in_specs=in_specs,
        out_specs=out_specs,
        out_shape=out_shape,
        scratch_shapes=[pltpu.VMEM((TM_FF, D_MODEL), BF16)],
        compiler_params=pltpu.CompilerParams(
            dimension_semantics=("arbitrary", "arbitrary"), vmem_limit_bytes=VMEM_LIMIT),
        name="ffn_ln",
    )(*args)
    return outs[0], (tuple(outs[1:]) if nxt is not None else None)


def _proj_ln_kernel(x_ref, z_ref, *refs, gated, nn):
    w_refs, (g_ref, b_ref, o_ref, mix_ref) = refs[:-4], refs[-4:]
    n = pl.program_id(1)

    def mix_tile(rows):
        z = z_ref[rows, :]
        p = _dot(z, w_refs[0][...])
        if gated:
            p = p * jax.nn.sigmoid(_dot(z, w_refs[1][...]))
        return p

    @pl.when(n < nn - 1)
    def _():
        mix_ref[n] = mix_tile(slice(None))

    @pl.when(n == nn - 1)
    def _():
        def finish(rows, last_tile):
            tiles = [mix_ref[j, rows, :] for j in range(nn - 1)] + [last_tile]
            ys = [DN_ALPHA * x_ref[rows, j * TN:(j + 1) * TN] + tiles[j] for j in range(nn)]
            mu = sum(jnp.sum(y, axis=-1, keepdims=True) for y in ys) * (1.0 / D_MODEL)
            var = sum(jnp.sum((y - mu) * (y - mu), axis=-1, keepdims=True) for y in ys) * (1.0 / D_MODEL)
            r = lax.rsqrt(var + LN_EPS)
            for j in range(nn):
                sl = slice(j * TN, (j + 1) * TN)
                o_ref[rows, sl] = (ys[j] - mu) * r * g_ref[:, sl] + b_ref[:, sl]

        groups = _row_groups(o_ref.shape[0], PROJ_LN_GROUPS)
        pending = None
        for rows in groups:
            tile = mix_tile(rows)
            if pending is not None:
                finish(*pending)
            pending = (rows, tile)
        finish(*pending)


def _proj_ln(x, z, ws, ln_g, ln_b, lj, li):
    m = x.shape[0]
    nn = D_MODEL // TN
    gated = len(ws) == 2
    row_spec = pl.BlockSpec((TM, D_MODEL), lambda i, n: (i, 0))
    w_spec = pl.BlockSpec((None, D_MODEL, TN), lambda i, n: (lj, 0, n))
    ln_spec = pl.BlockSpec((None, None, 1, D_MODEL), lambda i, n: (li, 1, 0, 0))
    return pl.pallas_call(
        functools.partial(_proj_ln_kernel, gated=gated, nn=nn),
        grid=(m // TM, nn),
        in_specs=[row_spec, row_spec] + [w_spec] * len(ws) + [ln_spec, ln_spec],
        out_specs=row_spec,
        out_shape=jax.ShapeDtypeStruct((m, D_MODEL), F32),
        scratch_shapes=[pltpu.VMEM((nn - 1, TM, TN), F32)],
        compiler_params=pltpu.CompilerParams(
            dimension_semantics=("parallel", "arbitrary"), vmem_limit_bytes=VMEM_LIMIT),
        name="glu_ln" if gated else "out_ln",
    )(x, z, *ws, ln_g, ln_b)


def _in_proj_kernel(x_ref, w_ref, wg_ref, p_ref, gt_ref, xb_ref):
    n = pl.program_id(1)

    def column_tile():
        p = _dot(xb_ref[...], w_ref[...])
        for s in range(TN_IN // LANE):
            p_ref[s] = p[:, s * LANE:(s + 1) * LANE]

    @pl.when(n == 0)
    def _():
        xb = x_ref[...].astype(BF16)
        xb_ref[...] = xb
        gt_ref[...] = _dot(xb, wg_ref[...])
        column_tile()

    @pl.when(n > 0)
    def _():
        column_tile()


def _in_proj(x, w_in, w_gate, lj):
    m = x.shape[0]
    slabs = TN_IN // LANE
    return pl.pallas_call(
        _in_proj_kernel,
        grid=(m // TM_IN, ML_QKVO // TN_IN),
        in_specs=[
            pl.BlockSpec((TM_IN, D_MODEL), lambda i, n: (i, 0)),
            pl.BlockSpec((None, D_MODEL, TN_IN), lambda i, n: (lj, 0, n)),
            pl.BlockSpec((None, D_MODEL, LANE), lambda i, n: (lj, 0, 0)),
        ],
        out_specs=[
            pl.BlockSpec((slabs, TM_IN, LANE), lambda i, n: (n, i, 0)),
            pl.BlockSpec((TM_IN, LANE), lambda i, n: (i, 0)),
        ],
        out_shape=[jax.ShapeDtypeStruct((ML_QKVO // LANE, m, LANE), F32), jax.ShapeDtypeStruct((m, LANE), F32)],
        scratch_shapes=[pltpu.VMEM((TM_IN, D_MODEL), BF16)],
        compiler_params=pltpu.CompilerParams(
            dimension_semantics=("parallel", "arbitrary"), vmem_limit_bytes=VMEM_LIMIT),
        name="ml_in_proj",
    )(x, w_in, w_gate)


def _s5_prep_kernel(ar_ref, ai_ref, ldt_ref, br_ref, bi_ref, cr_ref, ci_ref, a1_ref, a2_ref, wb_ref, wc_ref):
    ch, st = S5_OCT_CH, S5_OCT_ST
    ar = ar_ref[...]
    ai = ai_ref[...]
    dt = jnp.exp(ldt_ref[...])
    mag = jnp.exp(dt * ar)
    abr = mag * jnp.cos(dt * ai)
    abi = mag * jnp.sin(dt * ai)
    zr = abr - 1.0
    zi = abi
    den = ar * ar + ai * ai
    gr = (zr * ar + zi * ai) / den
    gi = (zi * ar - zr * ai) / den
    br = br_ref[...]
    bi = bi_ref[...]
    bbr = gr * br - gi * bi
    bbi = gr * bi + gi * br

    def lanes(x):
        return jnp.concatenate([x[g] for g in range(S5_OCT)], axis=1)

    upper = lax.broadcasted_iota(jnp.int32, (8, st), 0) < BATCH
    a1_ref[...] = jnp.broadcast_to(lanes(abr), (8, st))
    a2_ref[...] = jnp.where(upper, -lanes(abi), lanes(abi))

    def block_diag(x, block_rows, block_cols):
        flat = x.reshape(S5_OCT * block_rows, block_cols)
        rep = jnp.concatenate([flat] * S5_OCT, axis=1)
        r = lax.broadcasted_iota(jnp.int32, rep.shape, 0) // block_rows
        c = lax.broadcasted_iota(jnp.int32, rep.shape, 1) // block_cols
        return jnp.where(r == c, rep, 0.0).astype(BF16)

    wb_ref[:ch, :] = block_diag(bbr, S5_GROUP, S5_STATE)
    wb_ref[ch:, :] = block_diag(bbi, S5_GROUP, S5_STATE)
    wc_ref[:, :ch] = block_diag(cr_ref[...], S5_STATE, S5_GROUP)
    wc_ref[:, ch:] = block_diag(-ci_ref[...], S5_STATE, S5_GROUP)


def _s5_prepare(a_re, a_im, log_dt, b_re, b_im, c_re, c_im):
    n_l = a_re.shape[0]
    n = n_l * S5_NOCT
    p, c = S5_STATE, S5_GROUP
    a_spec = pl.BlockSpec((None, S5_OCT, 1, p), lambda s: (s, 0, 0, 0))
    b_spec = pl.BlockSpec((None, S5_OCT, c, p), lambda s: (s, 0, 0, 0))
    c_spec = pl.BlockSpec((None, S5_OCT, p, c), lambda s: (s, 0, 0, 0))
    coef_spec = pl.BlockSpec((None, 8, S5_OCT_ST), lambda s: (s // S5_NOCT, 0, s % S5_NOCT))
    return pl.pallas_call(
        _s5_prep_kernel,
        grid=(n,),
        in_specs=[a_spec, a_spec, pl.BlockSpec((None, S5_OCT, 1, 1), lambda s: (s, 0, 0, 0)),
                  b_spec, b_spec, c_spec, c_spec],
        out_specs=[coef_spec, coef_spec,
                   pl.BlockSpec((None, 2 * S5_OCT_CH, S5_OCT_ST), lambda s: (s, 0, 0)),
                   pl.BlockSpec((None, S5_OCT_ST, 2 * S5_OCT_CH), lambda s: (s, 0, 0))],
        out_shape=[jax.ShapeDtypeStruct((n_l, 8, S5_LANES), F32), jax.ShapeDtypeStruct((n_l, 8, S5_LANES), F32),
                   jax.ShapeDtypeStruct((n, 2 * S5_OCT_CH, S5_OCT_ST), BF16),
                   jax.ShapeDtypeStruct((n, S5_OCT_ST, 2 * S5_OCT_CH), BF16)],
        compiler_params=pltpu.CompilerParams(dimension_semantics=("arbitrary",)),
        name="s5_prepare",
    )(a_re.reshape(n, S5_OCT, 1, p), a_im.reshape(n, S5_OCT, 1, p), log_dt.reshape(n, S5_OCT, 1, 1),
      jnp.swapaxes(b_re, 2, 3).reshape(n, S5_OCT, c, p), jnp.swapaxes(b_im, 2, 3).reshape(n, S5_OCT, c, p),
      jnp.swapaxes(c_re, 2, 3).reshape(n, S5_OCT, p, c), jnp.swapaxes(c_im, 2, 3).reshape(n, S5_OCT, p, c))


def _s5_seq_kernel(*refs, nsteps, nchunks):
    u_refs = refs[:S5_PAIR]
    (h0_ref, a1_ref, a2_ref, wb_ref, wc_ref, d_ref, z_ref, hf_ref,
     lhs_l, lhs_r, x8_s, zl_s, zr_s, zo_s, st_s) = refs[S5_PAIR:]
    c = pl.program_id(1)
    ch, st = S5_OCT_CH, S5_OCT_ST

    def seq_steps(b, stride, half=0):
        return pl.ds(half * BATCH + b, nsteps, stride=stride)

    @pl.when(c == 0)
    def _():
        st_s[...] = h0_ref[...]
        lhs_l[...] = jnp.zeros_like(lhs_l)
        lhs_r[...] = jnp.zeros_like(lhs_r)

    for o in range(S5_PAIR):
        for b in range(BATCH):
            ub = u_refs[o][seq_steps(b, BATCH), :]
            lhs_l[o, seq_steps(b, 8, 0), :] = ub
            lhs_r[o, seq_steps(b, 8, 1), :] = ub
        lhs = jnp.concatenate([lhs_l[o], lhs_r[o]], axis=1).astype(BF16)
        x8_s[:, o * st:(o + 1) * st] = _dot(lhs, wb_ref[o])

    a1 = a1_ref[...]
    a2 = a2_ref[...]

    def step(t, h):
        r = pl.multiple_of(t * 8, 8)
        hn = (a1 * h + x8_s[pl.ds(r, 8), :]) + a2 * pltpu.roll(h, BATCH, 0)
        x8_s[pl.ds(r, 8), :] = hn
        return hn

    h_last = lax.fori_loop(0, nsteps, step, st_s[...], unroll=4)
    st_s[...] = h_last

    for o in range(S5_PAIR):
        zz = _dot(x8_s[:, o * st:(o + 1) * st].astype(BF16), wc_ref[o])
        zl_s[...] = zz[:, :ch]
        zr_s[...] = zz[:, ch:]
        for b in range(BATCH):
            y = zl_s[seq_steps(b, 8, 0), :] + zr_s[seq_steps(b, 8, 1), :]
            y = y + d_ref[:, o * ch:(o + 1) * ch] * u_refs[o][seq_steps(b, BATCH), :]
            zo_s[seq_steps(b, BATCH), :] = jax.nn.gelu(y)
        z_ref[:, o * ch:(o + 1) * ch] = zo_s[...].astype(BF16)

    @pl.when(c == nchunks - 1)
    def _():
        hf_ref[...] = h_last


def _s5_scan_prompt(x_all, h0, a1, a2, wb, wc, d_row, lj):
    rows = S5_STEPS * BATCH
    nchunks = SEQ // S5_STEPS
    pc, ps = S5_PAIR * S5_OCT_CH, S5_PAIR * S5_OCT_ST
    groups = S5_NOCT // S5_PAIR
    st_spec = pl.BlockSpec((8, ps), lambda p, c: (0, p))
    coef_spec = pl.BlockSpec((None, 8, ps), lambda p, c: (lj, 0, p))
    return pl.pallas_call(
        functools.partial(_s5_seq_kernel, nsteps=S5_STEPS, nchunks=nchunks),
        grid=(groups, nchunks),
        in_specs=[pl.BlockSpec((rows, S5_OCT_CH), lambda p, c, o=o: (c, S5_PAIR * p + o)) for o in range(S5_PAIR)] + [
            st_spec, coef_spec, coef_spec,
            pl.BlockSpec((S5_PAIR, 2 * S5_OCT_CH, S5_OCT_ST), lambda p, c: (lj * groups + p, 0, 0)),
            pl.BlockSpec((S5_PAIR, S5_OCT_ST, 2 * S5_OCT_CH), lambda p, c: (lj * groups + p, 0, 0)),
            pl.BlockSpec((1, pc), lambda p, c: (0, p)),
        ],
        out_specs=[pl.BlockSpec((rows, pc), lambda p, c: (c, p)), st_spec],
        out_shape=[jax.ShapeDtypeStruct((ROWS_ALL, D_MODEL), BF16),
                   jax.ShapeDtypeStruct((8, S5_LANES), F32)],
        scratch_shapes=[pltpu.VMEM((S5_PAIR, 2 * rows, S5_OCT_CH), F32), pltpu.VMEM((S5_PAIR, 2 * rows, S5_OCT_CH), F32),
                        pltpu.VMEM((2 * rows, ps), F32),
                        pltpu.VMEM((2 * rows, S5_OCT_CH), F32), pltpu.VMEM((2 * rows, S5_OCT_CH), F32),
                        pltpu.VMEM((rows, S5_OCT_CH), F32), pltpu.VMEM((8, ps), F32)],
        compiler_params=pltpu.CompilerParams(
            dimension_semantics=("parallel", "arbitrary"), vmem_limit_bytes=VMEM_LIMIT),
        name="s5_scan_prompt",
    )(*([x_all] * S5_PAIR), h0, a1, a2, wb, wc, d_row)


def _s5_step_kernel(u_ref, h0r_ref, h0i_ref, a1_ref, a2_ref, wb_ref, wc_ref, d_ref,
                    z_all_ref, z_ref, hr_ref, hi_ref):
    del z_all_ref
    ch = S5_OCT_CH
    u = u_ref[...]
    ub = u.astype(BF16)
    xr = _dot(ub, wb_ref[:ch, :])
    xi = _dot(ub, wb_ref[ch:, :])
    ar = a1_ref[0:1, :]
    ai = a2_ref[BATCH:BATCH + 1, :]
    h0r = h0r_ref[...]
    h0i = h0i_ref[...]
    hr = xr + (ar * h0r - ai * h0i)
    hi = xi + (ar * h0i + ai * h0r)
    hr_ref[...] = hr
    hi_ref[...] = hi
    y = _dot(hr.astype(BF16), wc_ref[:, :ch]) + _dot(hi.astype(BF16), wc_ref[:, ch:])
    y = y + d_ref[...] * u
    z_ref[...] = jax.nn.gelu(y).astype(BF16)


def _s5_step_sample(x_all, h0r, h0i, lj, a1, a2, wb, wc, d_row, z_all):
    row_blk = ROWS_PROMPT // DEC_BATCH
    st_in = pl.BlockSpec((None, DEC_BATCH, S5_OCT_ST), lambda o: (lj, 0, o))
    st_out = pl.BlockSpec((DEC_BATCH, S5_OCT_ST), lambda o: (0, o))
    coef_spec = pl.BlockSpec((None, 8, S5_OCT_ST), lambda o: (lj, 0, o))
    return pl.pallas_call(
        _s5_step_kernel,
        grid=(S5_NOCT,),
        in_specs=[
            pl.BlockSpec((DEC_BATCH, S5_OCT_CH), lambda o: (row_blk, o)),
            st_in, st_in,
            coef_spec, coef_spec,
            pl.BlockSpec((None, 2 * S5_OCT_CH, S5_OCT_ST), lambda o: (lj * S5_NOCT + o, 0, 0)),
            pl.BlockSpec((None, S5_OCT_ST, 2 * S5_OCT_CH), lambda o: (lj * S5_NOCT + o, 0, 0)),
            pl.BlockSpec((1, S5_OCT_CH), lambda o: (0, o)),
            pl.BlockSpec(memory_space=pl.ANY),
        ],
        out_specs=[pl.BlockSpec((DEC_BATCH, S5_OCT_CH), lambda o: (row_blk, o)), st_out, st_out],
        out_shape=[jax.ShapeDtypeStruct((ROWS_ALL, D_MODEL), BF16),
                   jax.ShapeDtypeStruct((DEC_BATCH, S5_LANES), F32),
                   jax.ShapeDtypeStruct((DEC_BATCH, S5_LANES), F32)],
        input_output_aliases={8: 0},
        compiler_params=pltpu.CompilerParams(
            dimension_semantics=("parallel",), vmem_limit_bytes=VMEM_LIMIT),
        name="s5_step_sample",
    )(x_all, h0r, h0i, a1, a2, wb, wc, d_row, z_all)


def _mlstm_seq_kernel(q_ref, k_ref, v_ref, o_ref, gt_ref, bias_ref, ng_ref,
                      hn_ref, cf_ref, nf_ref, mf_ref, c_s, n_s, m_s, hn_s, *, nchunks):
    ci = pl.program_id(0)
    bp = pl.program_id(1)
    L = ML_CHUNK
    seqs = [bp * ML_SEQS + i for i in range(ML_SEQS)]

    @pl.when(ci == 0)
    def _():
        for b in seqs:
            c_s[b] = jnp.zeros(c_s.shape[1:], F32)
            n_s[b] = jnp.zeros(n_s.shape[1:], F32)
            m_s[b] = jnp.zeros(m_s.shape[1:], F32)

    row = lax.broadcasted_iota(jnp.int32, (L, L), 0)
    col = lax.broadcasted_iota(jnp.int32, (L, L), 1)
    causal = col <= row
    tril = causal.astype(F32)
    triu = (row <= col).astype(F32)

    for b in seqs:
        _mlstm_chunk(b, causal, tril, triu, q_ref, k_ref, v_ref, o_ref, gt_ref, bias_ref, ng_ref, c_s, n_s, m_s, hn_s)

    @pl.when(ci == nchunks - 1)
    def _():
        for i, b in enumerate(seqs):
            cf_ref[i] = c_s[b]
            nf_ref[i] = n_s[b, 0:ML_HEADS, :]
            mf_ref[i] = m_s[b]

    @pl.when(bp == BATCH // ML_SEQS - 1)
    def _():
        for s in range(ML_HEADS * ML_DV // LANE):
            hn_ref[:, s * LANE:(s + 1) * LANE] = hn_s[s].astype(BF16)


def _mlstm_chunk(b, causal, tril, triu, q_ref, k_ref, v_ref, o_ref, gt_ref, bias_ref, ng_ref, c_s, n_s, m_s, hn_s):
    L = ML_CHUNK
    rows = pl.ds(b, L, stride=BATCH)

    def seq_rows(ref, h, width):
        per_head = width // LANE
        return jnp.concatenate([ref[h * per_head + j, rows, :] for j in range(per_head)], axis=1)

    gb = gt_ref[rows, :] + bias_ref[...]
    lfa = _log_sigmoid(gb)
    gbt = gb.T
    lft = lfa.T
    bc_cols = jnp.dot(tril, lfa, precision=lax.Precision.HIGHEST, preferred_element_type=F32)
    bc_rows = jnp.dot(lft[0:8, :], triu, precision=lax.Precision.HIGHEST, preferred_element_type=F32)

    for h in range(ML_HEADS):
        ig_row = gbt[h:h + 1, :]
        bc_row = bc_rows[ML_HEADS + h:ML_HEADS + h + 1, :]
        ig_col = gb[:, h:h + 1]
        bc_col = bc_cols[:, ML_HEADS + h:ML_HEADS + h + 1]
        m_prev = m_s[b, h:h + 1, 0:1]
        inter = bc_col + m_prev
        dmat = jnp.where(causal, bc_col - bc_row + ig_row, -jnp.inf)
        mt = jnp.maximum(inter, jnp.max(dmat, axis=1, keepdims=True))
        wts = jnp.exp(dmat - mt)

        qf = seq_rows(q_ref, h, ML_DK)
        kf = seq_rows(k_ref, h, ML_DK) * K_SCALE
        qb = qf.astype(BF16)
        vb = seq_rows(v_ref, h, ML_DV).astype(BF16)
        qk = lax.dot_general(qb, kf.astype(BF16), (((1,), (1,)), ((), ())), preferred_element_type=F32)
        sc = qk * wts
        scale = jnp.exp(inter - mt)
        cmat = c_s[b, h]
        n_row = n_s[b, h:h + 1, :]
        num = scale * _dot(qb, cmat.astype(BF16)) + _dot(sc.astype(BF16), vb)
        qn = jnp.sum(qb.astype(F32) * n_row, axis=1, keepdims=True)
        den = scale * qn + jnp.sum(sc, axis=1, keepdims=True)
        hh = num * (1.0 / jnp.maximum(jnp.abs(den), jnp.exp(-mt)))

        mu = jnp.mean(hh, axis=1, keepdims=True)
        hc = hh - mu
        var = jnp.mean(hc * hc, axis=1, keepdims=True)
        hn = hc * lax.rsqrt(var + LN_EPS) * ng_ref[:, h * ML_DV:(h + 1) * ML_DV]
        gated = jax.nn.sigmoid(seq_rows(o_ref, h, ML_DV)) * hn
        for j in range(ML_DV // LANE):
            hn_s[h * (ML_DV // LANE) + j, rows, :] = gated[:, j * LANE:(j + 1) * LANE]

        m_last = mt[L - 1:L, :]
        b_last = bc_col[L - 1:L, :]
        dec = jnp.exp(b_last - bc_col + ig_col - m_last)
        cscale = jnp.exp(b_last + m_prev - m_last)
        kd = kf * dec
        c_new = cscale * cmat + _dot(kd.T.astype(BF16), vb)
        n_new = cscale * n_row + jnp.sum(kd, axis=0, keepdims=True)
        c_s[b, h] = c_new
        n_s[b, h:h + 1, :] = n_new
        m_s[b, h:h + 1, :] = jnp.broadcast_to(m_last, (1, LANE))


def _mlstm_prompt(proj, gates, bias_row, norm_g, lj):
    nchunks = SEQ // ML_CHUNK
    rows = ML_CHUNK * BATCH
    qk_s = ML_HEADS * ML_DK // LANE
    v_s = ML_HEADS * ML_DV // LANE
    v_w = ML_HEADS * ML_DV
    last = nchunks - 1

    def state_idx(c, b):
        return jnp.where(c == last, b, 0)

    return pl.pallas_call(
        functools.partial(_mlstm_seq_kernel, nchunks=nchunks),
        grid=(nchunks, BATCH // ML_SEQS),
        in_specs=[
            pl.BlockSpec((qk_s, rows, LANE), lambda c, b: (0, c, 0)),
            pl.BlockSpec((qk_s, rows, LANE), lambda c, b: (1, c, 0)),
            pl.BlockSpec((v_s, rows, LANE), lambda c, b: (1, c, 0)),
            pl.BlockSpec((v_s, rows, LANE), lambda c, b: (2, c, 0)),
            pl.BlockSpec((rows, LANE), lambda c, b: (c, 0)),
            pl.BlockSpec((1, LANE), lambda c, b: (0, 0)),
            pl.BlockSpec((None, 1, v_w), lambda c, b: (lj, 0, 0)),
        ],
        out_specs=[
            pl.BlockSpec((rows, v_w), lambda c, b: (c, 0)),
            pl.BlockSpec((ML_SEQS, ML_HEADS, ML_DK, ML_DV), lambda c, b: (state_idx(c, b), 0, 0, 0)),
            pl.BlockSpec((ML_SEQS, ML_HEADS, ML_DK), lambda c, b: (state_idx(c, b), 0, 0)),
            pl.BlockSpec((ML_SEQS, 8, LANE), lambda c, b: (state_idx(c, b), 0, 0)),
        ],
        out_shape=[
            jax.ShapeDtypeStruct((ROWS_ALL, v_w), BF16),
            jax.ShapeDtypeStruct((BATCH, ML_HEADS, ML_DK, ML_DV), F32),
            jax.ShapeDtypeStruct((BATCH, ML_HEADS, ML_DK), F32),
            jax.ShapeDtypeStruct((BATCH, 8, LANE), F32),
        ],
        scratch_shapes=[pltpu.VMEM((BATCH, ML_HEADS, ML_DK, ML_DV), F32), pltpu.VMEM((BATCH, 8, ML_DK), F32),
                        pltpu.VMEM((BATCH, 8, LANE), F32), pltpu.VMEM((v_s, rows, LANE), F32)],
        compiler_params=pltpu.CompilerParams(
            dimension_semantics=("arbitrary", "arbitrary"), vmem_limit_bytes=VMEM_LIMIT),
        name="mlstm_prompt",
    )(proj, proj, proj, proj, gates, bias_row, norm_g)


def _mlstm_step_kernel(q_ref, k_ref, v_ref, o_ref, ka_ref, va_ref, gt_ref, bias_ref, ng_ref,
                       c_ref, n_ref, m_ref, hn_ref, co_ref, no_ref, mo_ref, kt_s, num_s):
    h = pl.program_id(0)
    i = pl.program_id(1)
    bb = ML_DEC_BB

    def cols(ref):
        return jnp.concatenate([ref[s] for s in range(ref.shape[0])], axis=1)

    @pl.when(i == 0)
    def _():
        kt_s[...] = (cols(ka_ref) * K_SCALE).T

    gb = gt_ref[...] + bias_ref[...]
    lane = lax.broadcasted_iota(jnp.int32, (bb, LANE), 1)
    ig = jnp.sum(jnp.where(lane == h, gb, 0.0), axis=1, keepdims=True)
    fp = jnp.sum(jnp.where(lane == h + ML_HEADS, gb, 0.0), axis=1, keepdims=True)
    lf = _log_sigmoid(fp)
    m4 = m_ref[...]
    hl = lax.broadcasted_iota(jnp.int32, (bb, ML_HEADS), 1)
    m_prev = jnp.sum(jnp.where(hl == h, m4, 0.0), axis=1, keepdims=True)
    inter = lf + m_prev
    mt = jnp.maximum(inter, ig)
    wts = jnp.exp(ig - mt)
    scale = jnp.exp(inter - mt)

    qb = cols(q_ref).astype(BF16)
    qf = qb.astype(F32)
    kf = cols(k_ref) * K_SCALE
    vf = cols(v_ref).astype(BF16).astype(F32)
    n_prev = n_ref[...]
    qk = jnp.sum(qf * kf.astype(BF16).astype(F32), axis=1, keepdims=True)
    sc = qk * wts
    qn = jnp.sum(qf * n_prev, axis=1, keepdims=True)
    den = scale * qn + sc
    rden = 1.0 / jnp.maximum(jnp.abs(den), jnp.exp(-mt))

    ktb = kt_s[...].astype(BF16)
    va = cols(va_ref)
    rowi = lax.broadcasted_iota(jnp.int32, (DEC_BATCH, ML_DV), 0)
    for j in range(bb):
        cj = c_ref[j]
        qc = _dot(qb, cj.astype(BF16))[j:j + 1, :]
        vsel = jnp.where(rowi == i * bb + j, va, 0.0).astype(BF16)
        outer = _dot(ktb, vsel)
        co_ref[j] = scale[j:j + 1, :] * cj + wts[j:j + 1, :] * outer
        num_s[j:j + 1, :] = scale[j:j + 1, :] * qc + sc[j:j + 1, :] * vf[j:j + 1, :]

    hh = num_s[...] * rden
    mu = jnp.mean(hh, axis=1, keepdims=True)
    hc = hh - mu
    var = jnp.mean(hc * hc, axis=1, keepdims=True)
    hn = hc * lax.rsqrt(var + LN_EPS) * ng_ref[...]
    hn_ref[...] = (jax.nn.sigmoid(cols(o_ref)) * hn).astype(BF16)
    no_ref[...] = scale * n_prev + wts * kf
    mo_ref[...] = jnp.broadcast_to(mt, (bb, LANE))


def _mlstm_sample(proj, gates, bias_row, norm_g, c_in, n_in, m_in, hn_all, c_acc, lj):
    bb = ML_DEC_BB
    rb = ROWS_PROMPT // bb
    r128 = ROWS_PROMPT // DEC_BATCH
    nh = ML_HEADS
    ks = ML_DK // LANE
    vs = ML_DV // LANE
    in_specs = [
        pl.BlockSpec((ks, bb, LANE), lambda h, i: (h, rb + i, 0)),
        pl.BlockSpec((ks, bb, LANE), lambda h, i: (nh + h, rb + i, 0)),
        pl.BlockSpec((vs, bb, LANE), lambda h, i: (nh + h, rb + i, 0)),
        pl.BlockSpec((vs, bb, LANE), lambda h, i: (2 * nh + h, rb + i, 0)),
        pl.BlockSpec((ks, DEC_BATCH, LANE), lambda h, i: (nh + h, r128, 0)),
        pl.BlockSpec((vs, DEC_BATCH, LANE), lambda h, i: (nh + h, r128, 0)),
        pl.BlockSpec((bb, LANE), lambda h, i: (rb + i, 0)),
        pl.BlockSpec((1, LANE), lambda h, i: (0, 0)),
        pl.BlockSpec((None, 1, ML_DV), lambda h, i: (lj, 0, h)),
        pl.BlockSpec((None, bb, None, ML_DK, ML_DV), lambda h, i: (lj, i, h, 0, 0)),
        pl.BlockSpec((None, bb, ML_DK), lambda h, i: (lj, i, h)),
        pl.BlockSpec((None, bb, nh), lambda h, i: (lj, i, 0)),
        pl.BlockSpec(memory_space=pl.ANY),
    ]
    args = [proj, proj, proj, proj, proj, proj, gates, bias_row, norm_g, c_in, n_in, m_in, hn_all]
    n_blocked = 12
    aliases = {n_blocked: 0}
    if c_acc is not None:
        in_specs.append(pl.BlockSpec(memory_space=pl.ANY))
        args.append(c_acc)
        aliases[n_blocked + 1] = 1

    def body(*refs):
        _mlstm_step_kernel(*refs[:n_blocked], *refs[len(args):])

    return pl.pallas_call(
        body,
        grid=(nh, DEC_BATCH // bb),
        in_specs=in_specs,
        out_specs=[
            pl.BlockSpec((bb, ML_DV), lambda h, i: (rb + i, h)),
            pl.BlockSpec((None, bb, None, ML_DK, ML_DV), lambda h, i: (lj, i, h, 0, 0)),
            pl.BlockSpec((bb, ML_DK), lambda h, i: (i, h)),
            pl.BlockSpec((None, bb, LANE), lambda h, i: (h, i, 0)),
        ],
        out_shape=[
            jax.ShapeDtypeStruct((ROWS_ALL, nh * ML_DV), BF16),
            jax.ShapeDtypeStruct(c_in.shape, F32),
            jax.ShapeDtypeStruct((DEC_BATCH, nh * ML_DK), F32),
            jax.ShapeDtypeStruct((nh, DEC_BATCH, LANE), F32),
        ],
        scratch_shapes=[pltpu.VMEM((ML_DK, DEC_BATCH), F32), pltpu.VMEM((bb, ML_DV), F32)],
        input_output_aliases=aliases,
        compiler_params=pltpu.CompilerParams(
            dimension_semantics=("arbitrary", "arbitrary"), vmem_limit_bytes=VMEM_LIMIT),
        name="mlstm_step_sample",
    )(*args)


def kernel(x_prompt, x_sample, state_s5_re, state_s5_im, state_mlstm_C, state_mlstm_n, state_mlstm_m,
           ln_g, ln_b, ffn_w_gate, ffn_w_up, ffn_w_down,
           s5_a_re, s5_a_im, s5_log_dt, s5_b_re, s5_b_im, s5_c_re, s5_c_im, s5_d, s5_w_a, s5_w_b,
           ml_w_in, ml_b_i, ml_b_f, ml_norm_g, ml_w_out):
    n_s5 = s5_a_re.shape[0]
    n_ml = ml_w_in.shape[0]

    ffn_f32 = (ffn_w_gate, ffn_w_up, ffn_w_down)
    ffn_w = tuple(w[0, 0].astype(BF16) for w in ffn_f32)
    glu_a = s5_w_a.astype(BF16)
    glu_b = s5_w_b.astype(BF16)
    w_in = ml_w_in.astype(BF16)
    w_in_gate = jnp.pad(ml_w_in[:, :, ML_QKVO:].astype(BF16), ((0, 0), (0, 0), (0, LANE - 2 * ML_HEADS)))
    w_out = ml_w_out.astype(BF16)
    ln_g4 = ln_g.reshape(DEPTH, 3, 1, D_MODEL)
    ln_b4 = ln_b.reshape(DEPTH, 3, 1, D_MODEL)
    ml_bias = jnp.pad(jnp.concatenate([ml_b_i, ml_b_f], axis=1), ((0, 0), (0, LANE - 2 * ML_HEADS)))
    ml_ng = ml_norm_g.reshape(n_ml, 1, ML_HEADS * ML_DV)
    st_r = state_s5_re.reshape(n_s5, DEC_BATCH, S5_LANES)
    st_i = state_s5_im.reshape(n_s5, DEC_BATCH, S5_LANES)
    st_n = state_mlstm_n.reshape(n_ml, DEC_BATCH, ML_HEADS * ML_DK)

    s5_a1, s5_a2, s5_wb, s5_wc = _s5_prepare(s5_a_re, s5_a_im, s5_log_dt, s5_b_re, s5_b_im, s5_c_re, s5_c_im)

    x = _to_rows(x_prompt, x_sample.reshape(DEC_BATCH, D_MODEL))

    p_s5r, p_s5i, s_s5r, s_s5i = [], [], [], []
    p_c, p_n, p_m, s_n, s_m = [], [], [], [], []
    s_c = None
    zero_state = jnp.zeros((8, S5_LANES), F32)
    for li in range(DEPTH):
        x, ffn_w = _ffn_ln(x, ffn_w, ln_g4, ln_b4, li, 0, ffn_f32, (li, 1))
        j = li // 2
        if li % 2 == 0:
            d_row = s5_d[j].reshape(1, D_MODEL)
            z, hp = _s5_scan_prompt(x, zero_state, s5_a1, s5_a2, s5_wb, s5_wc, d_row, j)
            z, hsr, hsi = _s5_step_sample(x, st_r, st_i, j, s5_a1, s5_a2, s5_wb, s5_wc, d_row, z)
            p_s5r.append(hp[:BATCH].reshape(BATCH, S5_GROUPS, S5_STATE))
            p_s5i.append(hp[BATCH:].reshape(BATCH, S5_GROUPS, S5_STATE))
            s_s5r.append(hsr.reshape(DEC_BATCH, S5_GROUPS, S5_STATE))
            s_s5i.append(hsi.reshape(DEC_BATCH, S5_GROUPS, S5_STATE))
            x = _proj_ln(x, z, (glu_a, glu_b), ln_g4, ln_b4, j, li)
        else:
            proj, gates = _in_proj(x, w_in, w_in_gate, j)
            bias_row = ml_bias[j].reshape(1, LANE)
            z, cp, npr, mp = _mlstm_prompt(proj, gates, bias_row, ml_ng, j)
            z, s_c, ns, ms = _mlstm_sample(proj, gates, bias_row, ml_ng, state_mlstm_C, st_n,
                                           state_mlstm_m, z, s_c, j)
            p_c.append(cp)
            p_n.append(npr)
            p_m.append(mp[:, :ML_HEADS, 0])
            s_n.append(ns.reshape(DEC_BATCH, ML_HEADS, ML_DK))
            s_m.append(jnp.swapaxes(ms[:, :, 0], 0, 1))
            x = _proj_ln(x, z, (w_out,), ln_g4, ln_b4, j, li)
        x, ffn_w = _ffn_ln(x, ffn_w, ln_g4, ln_b4, li, 2, ffn_f32, (li + 1, 0) if li + 1 < DEPTH else None)

    y_prompt, y_sample = _from_rows(x)
    y_sample = y_sample.reshape(DEC_BATCH, 1, D_MODEL)
    return (y_prompt, y_sample,
            jnp.stack(p_s5r), jnp.stack(p_s5i), jnp.stack(p_c), jnp.stack(p_n), jnp.stack(p_m),
            jnp.stack(s_s5r), jnp.stack(s_s5i), s_c, jnp.stack(s_n), jnp.stack(s_m))
```

```python
import functools

import jax
import jax.numpy as jnp
from jax import lax
from jax.experimental import pallas as pl
from jax.experimental.pallas import tpu as pltpu

F32 = jnp.float32
BF16 = jnp.bfloat16

D_MODEL = 2048
BATCH = 4
SEQ = 2048
DEPTH = 4
DEC_BATCH = 128
D_FF = 5504
S5_GROUPS = 128
S5_GROUP = 16
S5_STATE = 64
S5_LANES = S5_GROUPS * S5_STATE
ML_HEADS = 4
ML_DK = 256
ML_DV = 512
ML_CHUNK = 128
ML_QKVO = 2 * ML_HEADS * ML_DK + 2 * ML_HEADS * ML_DV
DN_ALPHA = (2.0 * DEPTH) ** 0.25
LN_EPS = 1e-5
K_SCALE = ML_DK ** -0.5

LANE = 128
ROWS_PROMPT = SEQ * BATCH
ROWS_ALL = ROWS_PROMPT + DEC_BATCH
TM = 640
TM_FF = 832
TF = 512
CV_BLOCKS = 8
TN = 512
PROJ_LN_GROUPS = 2
TM_IN = 1040
TN_IN = 1024
S5_OCT = 8
S5_NOCT = S5_GROUPS // S5_OCT
S5_OCT_CH = S5_OCT * S5_GROUP
S5_OCT_ST = S5_OCT * S5_STATE
S5_STEPS = 256
S5_PAIR = 4
RL_STEPS = 1024
ML_DEC_BB = 16
ML_SEQS = 4
VMEM_LIMIT = 56 * 1024 * 1024


def _dot(a, b):
    return jnp.dot(a, b, preferred_element_type=F32)


def _layer_norm_rows(y, g, b):
    mu = jnp.mean(y, axis=-1, keepdims=True)
    yc = y - mu
    var = jnp.mean(yc * yc, axis=-1, keepdims=True)
    return yc * lax.rsqrt(var + LN_EPS) * g + b


def _row_groups(n_rows, n_groups):
    step = -(-n_rows // n_groups)
    step = -(-step // 16) * 16
    return [slice(lo, min(lo + step, n_rows)) for lo in range(0, n_rows, step)]


def _log_sigmoid(x):
    return jnp.minimum(x, 0.0) - jnp.log1p(jnp.exp(-jnp.abs(x)))


def _to_rows_kernel(xp_ref, xs_ref, o_ref, *, nt):
    r = pl.program_id(0)

    @pl.when(r < nt)
    def _():
        for b in range(BATCH):
            o_ref[pl.ds(b, RL_STEPS, stride=BATCH), :] = xp_ref[b]

    @pl.when(r == nt)
    def _():
        o_ref[0:DEC_BATCH, :] = xs_ref[...]


def _to_rows(x_prompt, x_sample):
    nt = SEQ // RL_STEPS
    ns = D_MODEL // LANE
    return pl.pallas_call(
        functools.partial(_to_rows_kernel, nt=nt),
        grid=(nt + 1, ns),
        in_specs=[
            pl.BlockSpec((BATCH, RL_STEPS, LANE),
                         lambda r, s: (0, jnp.minimum(r, nt - 1), jnp.where(r == nt, ns - 1, s))),
            pl.BlockSpec((DEC_BATCH, LANE), lambda r, s: (0, s)),
        ],
        out_specs=pl.BlockSpec((RL_STEPS * BATCH, LANE), lambda r, s: (r, s)),
        out_shape=jax.ShapeDtypeStruct((ROWS_ALL, D_MODEL), F32),
        compiler_params=pltpu.CompilerParams(
            dimension_semantics=("arbitrary", "arbitrary"), vmem_limit_bytes=VMEM_LIMIT),
        name="to_rows",
    )(x_prompt, x_sample)


def _from_rows_kernel(x_ref, yp_ref, ys_ref, *, nt):
    r = pl.program_id(0)

    @pl.when(r < nt)
    def _():
        for b in range(BATCH):
            yp_ref[b] = x_ref[pl.ds(b, RL_STEPS, stride=BATCH), :]

    @pl.when(r == nt)
    def _():
        ys_ref[...] = x_ref[0:DEC_BATCH, :]


def _from_rows(x):
    nt = SEQ // RL_STEPS
    ns = D_MODEL // LANE
    return pl.pallas_call(
        functools.partial(_from_rows_kernel, nt=nt),
        grid=(nt + 1, ns),
        in_specs=[pl.BlockSpec((RL_STEPS * BATCH, LANE), lambda r, s: (r, s))],
        out_specs=[
            pl.BlockSpec((BATCH, RL_STEPS, LANE),
                         lambda r, s: (0, jnp.minimum(r, nt - 1), jnp.where(r == nt, ns - 1, s))),
            pl.BlockSpec((DEC_BATCH, LANE), lambda r, s: (0, jnp.where(r == nt, s, 0))),
        ],
        out_shape=[jax.ShapeDtypeStruct((BATCH, SEQ, D_MODEL), F32),
                   jax.ShapeDtypeStruct((DEC_BATCH, D_MODEL), F32)],
        compiler_params=pltpu.CompilerParams(
            dimension_semantics=("arbitrary", "arbitrary"), vmem_limit_bytes=VMEM_LIMIT),
        name="from_rows",
    )(x)


def _ffn_kernel(x_ref, wg_ref, wu_ref, wd_ref, g_ref, b_ref, *refs, nf, convert_next):
    if convert_next:
        ng_ref, nu_ref, nd_ref, o_ref, cg_ref, cu_ref, cd_ref, xb_ref = refs
    else:
        o_ref, xb_ref = refs
    f = pl.program_id(1)

    def start_row_tile():
        x = x_ref[...]
        xb_ref[...] = x.astype(BF16)
        o_ref[...] = (2.0 * DN_ALPHA) * x

    def hidden_tile(width):
        xb = xb_ref[...]
        hg = _dot(xb, wg_ref[:, :width])
        hu = _dot(xb, wu_ref[:, :width])
        act = (hg * jax.nn.sigmoid(hg) * hu).astype(BF16)
        o_ref[...] += _dot(act, wd_ref[:width, :])
        if convert_next:
            cg_ref[:, :width] = ng_ref[:, :width].astype(BF16)
            cu_ref[:, :width] = nu_ref[:, :width].astype(BF16)
            cd_ref[:width, :] = nd_ref[:width, :].astype(BF16)

    @pl.when(f == 0)
    def _():
        start_row_tile()
        hidden_tile(TF)

    @pl.when(jnp.logical_and(f > 0, f < nf - 1))
    def _():
        hidden_tile(TF)

    @pl.when(f == nf - 1)
    def _():
        hidden_tile(D_FF - (nf - 1) * TF)
        o_ref[...] = _layer_norm_rows(0.5 * o_ref[...], g_ref[...], b_ref[...])


def _ffn_ln(x, w_cur, ln_g, ln_b, li, kn, w_f32, nxt):
    m = x.shape[0]
    nf = pl.cdiv(D_FF, TF)
    n_row_tiles = m // TM_FF
    cv = D_MODEL // CV_BLOCKS
    assert n_row_tiles >= CV_BLOCKS
    in_specs = [
        pl.BlockSpec((TM_FF, D_MODEL), lambda i, f: (i, 0)),
        pl.BlockSpec((D_MODEL, TF), lambda i, f: (0, f)),
        pl.BlockSpec((D_MODEL, TF), lambda i, f: (0, f)),
        pl.BlockSpec((TF, D_MODEL), lambda i, f: (f, 0)),
        pl.BlockSpec((None, None, 1, D_MODEL), lambda i, f: (li, kn, 0, 0)),
        pl.BlockSpec((None, None, 1, D_MODEL), lambda i, f: (li, kn, 0, 0)),
    ]
    out_specs = [pl.BlockSpec((TM_FF, D_MODEL), lambda i, f: (i, 0))]
    out_shape = [jax.ShapeDtypeStruct((m, D_MODEL), F32)]
    args = [x, *w_cur, ln_g, ln_b]
    if nxt is not None:
        nli, nk = nxt

        def cv_row(i, f):
            return jnp.minimum(i, CV_BLOCKS - 1)

        def cv_col(i, f):
            return jnp.where(i < CV_BLOCKS, f, nf - 1)

        in_specs += [
            pl.BlockSpec((None, None, cv, TF), lambda i, f: (nli, nk, cv_row(i, f), cv_col(i, f))),
            pl.BlockSpec((None, None, cv, TF), lambda i, f: (nli, nk, cv_row(i, f), cv_col(i, f))),
            pl.BlockSpec((None, None, TF, cv), lambda i, f: (nli, nk, cv_col(i, f), cv_row(i, f))),
        ]
        out_specs += [
            pl.BlockSpec((cv, TF), lambda i, f: (cv_row(i, f), cv_col(i, f))),
            pl.BlockSpec((cv, TF), lambda i, f: (cv_row(i, f), cv_col(i, f))),
            pl.BlockSpec((TF, cv), lambda i, f: (cv_col(i, f), cv_row(i, f))),
        ]
        out_shape += [jax.ShapeDtypeStruct((D_MODEL, D_FF), BF16), jax.ShapeDtypeStruct((D_MODEL, D_FF), BF16),
                      jax.ShapeDtypeStruct((D_FF, D_MODEL), BF16)]
        args += list(w_f32)
    outs = pl.pallas_call(
        functools.partial(_ffn_kernel, nf=nf, convert_next=nxt is not None),
        grid=(n_row_tiles, nf),
        in_specs=in_specs,
        out_specs=out_specs,
        out_shape=out_shape,
        scratch_shapes=[pltpu.VMEM((TM_FF, D_MODEL), BF16)],
        compiler_params=pltpu.CompilerParams(
            dimension_semantics=("arbitrary", "arbitrary"), vmem_limit_bytes=VMEM_LIMIT),
        name="ffn_ln",
    )(*args)
    return outs[0], (tuple(outs[1:]) if nxt is not None else None)


def _proj_ln_kernel(x_ref, z_ref, *refs, gated, nn):
    w_refs, (g_ref, b_ref, o_ref, mix_ref) = refs[:-4], refs[-4:]
    n = pl.program_id(1)

    def mix_tile(rows):
        z = z_ref[rows, :]
        p = _dot(z, w_refs[0][...])
        if gated:
            p = p * jax.nn.sigmoid(_dot(z, w_refs[1][...]))
        return p

    @pl.when(n < nn - 1)
    def _():
        mix_ref[n] = mix_tile(slice(None))

    @pl.when(n == nn - 1)
    def _():
        def finish(rows, last_tile):
            tiles = [mix_ref[j, rows, :] for j in range(nn - 1)] + [last_tile]
            ys = [DN_ALPHA * x_ref[rows, j * TN:(j + 1) * TN] + tiles[j] for j in range(nn)]
            mu = sum(jnp.sum(y, axis=-1, keepdims=True) for y in ys) * (1.0 / D_MODEL)
            var = sum(jnp.sum((y - mu) * (y - mu), axis=-1, keepdims=True) for y in ys) * (1.0 / D_MODEL)
            r = lax.rsqrt(var + LN_EPS)
            for j in range(nn):
                sl = slice(j * TN, (j + 1) * TN)
                o_ref[rows, sl] = (ys[j] - mu) * r * g_ref[:, sl] + b_ref[:, sl]

        groups = _row_groups(o_ref.shape[0], PROJ_LN_GROUPS)
        pending = None
        for rows in groups:
            tile = mix_tile(rows)
            if pending is not None:
                finish(*pending)
            pending = (rows, tile)
        finish(*pending)


def _proj_ln(x, z, ws, ln_g, ln_b, lj, li):
    m = x.shape[0]
    nn = D_MODEL // TN
    gated = len(ws) == 2
    row_spec = pl.BlockSpec((TM, D_MODEL), lambda i, n: (i, 0))
    w_spec = pl.BlockSpec((None, D_MODEL, TN), lambda i, n: (lj, 0, n))
    ln_spec = pl.BlockSpec((None, None, 1, D_MODEL), lambda i, n: (li, 1, 0, 0))
    return pl.pallas_call(
        functools.partial(_proj_ln_kernel, gated=gated, nn=nn),
        grid=(m // TM, nn),
        in_specs=[row_spec, row_spec] + [w_spec] * len(ws) + [ln_spec, ln_spec],
        out_specs=row_spec,
        out_shape=jax.ShapeDtypeStruct((m, D_MODEL), F32),
        scratch_shapes=[pltpu.VMEM((nn - 1, TM, TN), F32)],
        compiler_params=pltpu.CompilerParams(
            dimension_semantics=("parallel", "arbitrary"), vmem_limit_bytes=VMEM_LIMIT),
        name="glu_ln" if gated else "out_ln",
    )(x, z, *ws, ln_g, ln_b)


def _in_proj_kernel(x_ref, w_ref, wg_ref, p_ref, gt_ref, xb_ref):
    n = pl.program_id(1)

    def column_tile():
        p = _dot(xb_ref[...], w_ref[...])
        for s in range(TN_IN // LANE):
            p_ref[s] = p[:, s * LANE:(s + 1) * LANE]

    @pl.when(n == 0)
    def _():
        xb = x_ref[...].astype(BF16)
        xb_ref[...] = xb
        gt_ref[...] = _dot(xb, wg_ref[...])
        column_tile()

    @pl.when(n > 0)
    def _():
        column_tile()


def _in_proj(x, w_in, w_gate, lj):
    m = x.shape[0]
    slabs = TN_IN // LANE
    return pl.pallas_call(
        _in_proj_kernel,
        grid=(m // TM_IN, ML_QKVO // TN_IN),
        in_specs=[
            pl.BlockSpec((TM_IN, D_MODEL), lambda i, n: (i, 0)),
            pl.BlockSpec((None, D_MODEL, TN_IN), lambda i, n: (lj, 0, n)),
            pl.BlockSpec((None, D_MODEL, LANE), lambda i, n: (lj, 0, 0)),
        ],
        out_specs=[
            pl.BlockSpec((slabs, TM_IN, LANE), lambda i, n: (n, i, 0)),
            pl.BlockSpec((TM_IN, LANE), lambda i, n: (i, 0)),
        ],
        out_shape=[jax.ShapeDtypeStruct((ML_QKVO // LANE, m, LANE), F32), jax.ShapeDtypeStruct((m, LANE), F32)],
        scratch_shapes=[pltpu.VMEM((TM_IN, D_MODEL), BF16)],
        compiler_params=pltpu.CompilerParams(
            dimension_semantics=("parallel", "arbitrary"), vmem_limit_bytes=VMEM_LIMIT),
        name="ml_in_proj",
    )(x, w_in, w_gate)


def _s5_prep_kernel(ar_ref, ai_ref, ldt_ref, br_ref, bi_ref, cr_ref, ci_ref, a1_ref, a2_ref, wb_ref, wc_ref):
    ch, st = S5_OCT_CH, S5_OCT_ST
    ar = ar_ref[...]
    ai = ai_ref[...]
    dt = jnp.exp(ldt_ref[...])
    mag = jnp.exp(dt * ar)
    abr = mag * jnp.cos(dt * ai)
    abi = mag * jnp.sin(dt * ai)
    zr = abr - 1.0
    zi = abi
    den = ar * ar + ai * ai
    gr = (zr * ar + zi * ai) / den
    gi = (zi * ar - zr * ai) / den
    br = br_ref[...]
    bi = bi_ref[...]
    bbr = gr * br - gi * bi
    bbi = gr * bi + gi * br

    def lanes(x):
        return jnp.concatenate([x[g] for g in range(S5_OCT)], axis=1)

    upper = lax.broadcasted_iota(jnp.int32, (8, st), 0) < BATCH
    a1_ref[...] = jnp.broadcast_to(lanes(abr), (8, st))
    a2_ref[...] = jnp.where(upper, -lanes(abi), lanes(abi))

    def block_diag(x, block_rows, block_cols):
        flat = x.reshape(S5_OCT * block_rows, block_cols)
        rep = jnp.concatenate([flat] * S5_OCT, axis=1)
        r = lax.broadcasted_iota(jnp.int32, rep.shape, 0) // block_rows
        c = lax.broadcasted_iota(jnp.int32, rep.shape, 1) // block_cols
        return jnp.where(r == c, rep, 0.0).astype(BF16)

    wb_ref[:ch, :] = block_diag(bbr, S5_GROUP, S5_STATE)
    wb_ref[ch:, :] = block_diag(bbi, S5_GROUP, S5_STATE)
    wc_ref[:, :ch] = block_diag(cr_ref[...], S5_STATE, S5_GROUP)
    wc_ref[:, ch:] = block_diag(-ci_ref[...], S5_STATE, S5_GROUP)


def _s5_prepare(a_re, a_im, log_dt, b_re, b_im, c_re, c_im):
    n_l = a_re.shape[0]
    n = n_l * S5_NOCT
    p, c = S5_STATE, S5_GROUP
    a_spec = pl.BlockSpec((None, S5_OCT, 1, p), lambda s: (s, 0, 0, 0))
    b_spec = pl.BlockSpec((None, S5_OCT, c, p), lambda s: (s, 0, 0, 0))
    c_spec = pl.BlockSpec((None, S5_OCT, p, c), lambda s: (s, 0, 0, 0))
    coef_spec = pl.BlockSpec((None, 8, S5_OCT_ST), lambda s: (s // S5_NOCT, 0, s % S5_NOCT))
    return pl.pallas_call(
        _s5_prep_kernel,
        grid=(n,),
        in_specs=[a_spec, a_spec, pl.BlockSpec((None, S5_OCT, 1, 1), lambda s: (s, 0, 0, 0)),
                  b_spec, b_spec, c_spec, c_spec],
        out_specs=[coef_spec, coef_spec,
                   pl.BlockSpec((None, 2 * S5_OCT_CH, S5_OCT_ST), lambda s: (s, 0, 0)),
                   pl.BlockSpec((None, S5_OCT_ST, 2 * S5_OCT_CH), lambda s: (s, 0, 0))],
        out_shape=[jax.ShapeDtypeStruct((n_l, 8, S5_LANES), F32), jax.ShapeDtypeStruct((n_l, 8, S5_LANES), F32),
                   jax.ShapeDtypeStruct((n, 2 * S5_OCT_CH, S5_OCT_ST), BF16),
                   jax.ShapeDtypeStruct((n, S5_OCT_ST, 2 * S5_OCT_CH), BF16)],
        compiler_params=pltpu.CompilerParams(dimension_semantics=("arbitrary",)),
        name="s5_prepare",
    )(a_re.reshape(n, S5_OCT, 1, p), a_im.reshape(n, S5_OCT, 1, p), log_dt.reshape(n, S5_OCT, 1, 1),
      jnp.swapaxes(b_re, 2, 3).reshape(n, S5_OCT, c, p), jnp.swapaxes(b_im, 2, 3).reshape(n, S5_OCT, c, p),
      jnp.swapaxes(c_re, 2, 3).reshape(n, S5_OCT, p, c), jnp.swapaxes(c_im, 2, 3).reshape(n, S5_OCT, p, c))


def _s5_seq_kernel(*refs, nsteps, nchunks):
    u_refs = refs[:S5_PAIR]
    (h0_ref, a1_ref, a2_ref, wb_ref, wc_ref, d_ref, z_ref, hf_ref,
     lhs_l, lhs_r, x8_s, zl_s, zr_s, zo_s, st_s) = refs[S5_PAIR:]
    c = pl.program_id(1)
    ch, st = S5_OCT_CH, S5_OCT_ST

    def seq_steps(b, stride, half=0):
        return pl.ds(half * BATCH + b, nsteps, stride=stride)

    @pl.when(c == 0)
    def _():
        st_s[...] = h0_ref[...]
        lhs_l[...] = jnp.zeros_like(lhs_l)
        lhs_r[...] = jnp.zeros_like(lhs_r)

    for o in range(S5_PAIR):
        for b in range(BATCH):
            ub = u_refs[o][seq_steps(b, BATCH), :]
            lhs_l[o, seq_steps(b, 8, 0), :] = ub
            lhs_r[o, seq_steps(b, 8, 1), :] = ub
        lhs = jnp.concatenate([lhs_l[o], lhs_r[o]], axis=1).astype(BF16)
        x8_s[:, o * st:(o + 1) * st] = _dot(lhs, wb_ref[o])

    a1 = a1_ref[...]
    a2 = a2_ref[...]

    def step(t, h):
        r = pl.multiple_of(t * 8, 8)
        hn = (a1 * h + x8_s[pl.ds(r, 8), :]) + a2 * pltpu.roll(h, BATCH, 0)
        x8_s[pl.ds(r, 8), :] = hn
        return hn

    h_last = lax.fori_loop(0, nsteps, step, st_s[...], unroll=4)
    st_s[...] = h_last

    for o in range(S5_PAIR):
        zz = _dot(x8_s[:, o * st:(o + 1) * st].astype(BF16), wc_ref[o])
        zl_s[...] = zz[:, :ch]
        zr_s[...] = zz[:, ch:]
        for b in range(BATCH):
            y = zl_s[seq_steps(b, 8, 0), :] + zr_s[seq_steps(b, 8, 1), :]
            y = y + d_ref[:, o * ch:(o + 1) * ch] * u_refs[o][seq_steps(b, BATCH), :]
            zo_s[seq_steps(b, BATCH), :] = jax.nn.gelu(y)
        z_ref[:, o * ch:(o + 1) * ch] = zo_s[...].astype(BF16)

    @pl.when(c == nchunks - 1)
    def _():
        hf_ref[...] = h_last


def _s5_scan_prompt(x_all, h0, a1, a2, wb, wc, d_row, lj):
    rows = S5_STEPS * BATCH
    nchunks = SEQ // S5_STEPS
    pc, ps = S5_PAIR * S5_OCT_CH, S5_PAIR * S5_OCT_ST
    groups = S5_NOCT // S5_PAIR
    st_spec = pl.BlockSpec((8, ps), lambda p, c: (0, p))
    coef_spec = pl.BlockSpec((None, 8, ps), lambda p, c: (lj, 0, p))
    return pl.pallas_call(
        functools.partial(_s5_seq_kernel, nsteps=S5_STEPS, nchunks=nchunks),
        grid=(groups, nchunks),
        in_specs=[pl.BlockSpec((rows, S5_OCT_CH), lambda p, c, o=o: (c, S5_PAIR * p + o)) for o in range(S5_PAIR)] + [
            st_spec, coef_spec, coef_spec,
            pl.BlockSpec((S5_PAIR, 2 * S5_OCT_CH, S5_OCT_ST), lambda p, c: (lj * groups + p, 0, 0)),
            pl.BlockSpec((S5_PAIR, S5_OCT_ST, 2 * S5_OCT_CH), lambda p, c: (lj * groups + p, 0, 0)),
            pl.BlockSpec((1, pc), lambda p, c: (0, p)),
        ],
        out_specs=[pl.BlockSpec((rows, pc), lambda p, c: (c, p)), st_spec],
        out_shape=[jax.ShapeDtypeStruct((ROWS_ALL, D_MODEL), BF16),
                   jax.ShapeDtypeStruct((8, S5_LANES), F32)],
        scratch_shapes=[pltpu.VMEM((S5_PAIR, 2 * rows, S5_OCT_CH), F32), pltpu.VMEM((S5_PAIR, 2 * rows, S5_OCT_CH), F32),
                        pltpu.VMEM((2 * rows, ps), F32),
                        pltpu.VMEM((2 * rows, S5_OCT_CH), F32), pltpu.VMEM((2 * rows, S5_OCT_CH), F32),
                        pltpu.VMEM((rows, S5_OCT_CH), F32), pltpu.VMEM((8, ps), F32)],
        compiler_params=pltpu.CompilerParams(
            dimension_semantics=("parallel", "arbitrary"), vmem_limit_bytes=VMEM_LIMIT),
        name="s5_scan_prompt",
    )(*([x_all] * S5_PAIR), h0, a1, a2, wb, wc, d_row)


def _s5_step_kernel(u_ref, h0r_ref, h0i_ref, a1_ref, a2_ref, wb_ref, wc_ref, d_ref,
                    z_all_ref, z_ref, hr_ref, hi_ref):
    del z_all_ref
    ch = S5_OCT_CH
    u = u_ref[...]
    ub = u.astype(BF16)
    xr = _dot(ub, wb_ref[:ch, :])
    xi = _dot(ub, wb_ref[ch:, :])
    ar = a1_ref[0:1, :]
    ai = a2_ref[BATCH:BATCH + 1, :]
    h0r = h0r_ref[...]
    h0i = h0i_ref[...]
    hr = xr + (ar * h0r - ai * h0i)
    hi = xi + (ar * h0i + ai * h0r)
    hr_ref[...] = hr
    hi_ref[...] = hi
    y = _dot(hr.astype(BF16), wc_ref[:, :ch]) + _dot(hi.astype(BF16), wc_ref[:, ch:])
    y = y + d_ref[...] * u
    z_ref[...] = jax.nn.gelu(y).astype(BF16)


def _s5_step_sample(x_all, h0r, h0i, lj, a1, a2, wb, wc, d_row, z_all):
    row_blk = ROWS_PROMPT // DEC_BATCH
    st_in = pl.BlockSpec((None, DEC_BATCH, S5_OCT_ST), lambda o: (lj, 0, o))
    st_out = pl.BlockSpec((DEC_BATCH, S5_OCT_ST), lambda o: (0, o))
    coef_spec = pl.BlockSpec((None, 8, S5_OCT_ST), lambda o: (lj, 0, o))
    return pl.pallas_call(
        _s5_step_kernel,
        grid=(S5_NOCT,),
        in_specs=[
            pl.BlockSpec((DEC_BATCH, S5_OCT_CH), lambda o: (row_blk, o)),
            st_in, st_in,
            coef_spec, coef_spec,
            pl.BlockSpec((None, 2 * S5_OCT_CH, S5_OCT_ST), lambda o: (lj * S5_NOCT + o, 0, 0)),
            pl.BlockSpec((None, S5_OCT_ST, 2 * S5_OCT_CH), lambda o: (lj * S5_NOCT + o, 0, 0)),
            pl.BlockSpec((1, S5_OCT_CH), lambda o: (0, o)),
            pl.BlockSpec(memory_space=pl.ANY),
        ],
        out_specs=[pl.BlockSpec((DEC_BATCH, S5_OCT_CH), lambda o: (row_blk, o)), st_out, st_out],
        out_shape=[jax.ShapeDtypeStruct((ROWS_ALL, D_MODEL), BF16),
                   jax.ShapeDtypeStruct((DEC_BATCH, S5_LANES), F32),
                   jax.ShapeDtypeStruct((DEC_BATCH, S5_LANES), F32)],
        input_output_aliases={8: 0},
        compiler_params=pltpu.CompilerParams(
            dimension_semantics=("parallel",), vmem_limit_bytes=VMEM_LIMIT),
        name="s5_step_sample",
    )(x_all, h0r, h0i, a1, a2, wb, wc, d_row, z_all)


def _mlstm_seq_kernel(q_ref, k_ref, v_ref, o_ref, gt_ref, bias_ref, ng_ref,
                      hn_ref, cf_ref, nf_ref, mf_ref, n_s, m_s, hn_s, *, nchunks):
    ci = pl.program_id(0)
    bp = pl.program_id(1)
    L = ML_CHUNK
    seqs = [bp * ML_SEQS + i for i in range(ML_SEQS)]
    c_s = cf_ref

    @pl.when(ci == 0)
    def _():
        for b in seqs:
            c_s[b] = jnp.zeros(c_s.shape[1:], F32)
            n_s[b] = jnp.zeros(n_s.shape[1:], F32)
            m_s[b] = jnp.zeros(m_s.shape[1:], F32)

    row = lax.broadcasted_iota(jnp.int32, (L, L), 0)
    col = lax.broadcasted_iota(jnp.int32, (L, L), 1)
    causal = col <= row
    tril = causal.astype(F32)
    triu = (row <= col).astype(F32)

    for b in seqs:
        _mlstm_chunk(b, causal, tril, triu, q_ref, k_ref, v_ref, o_ref, gt_ref, bias_ref, ng_ref, c_s, n_s, m_s, hn_s)

    @pl.when(ci == nchunks - 1)
    def _():
        for i, b in enumerate(seqs):
            nf_ref[i] = n_s[b, 0:ML_HEADS, :]
            mf_ref[i] = m_s[b]

    @pl.when(bp == BATCH // ML_SEQS - 1)
    def _():
        for s in range(ML_HEADS * ML_DV // LANE):
            hn_ref[:, s * LANE:(s + 1) * LANE] = hn_s[s].astype(BF16)


def _mlstm_chunk(b, causal, tril, triu, q_ref, k_ref, v_ref, o_ref, gt_ref, bias_ref, ng_ref, c_s, n_s, m_s, hn_s):
    L = ML_CHUNK
    rows = pl.ds(b, L, stride=BATCH)

    def seq_rows(ref, h, width):
        per_head = width // LANE
        return jnp.concatenate([ref[h * per_head + j, rows, :] for j in range(per_head)], axis=1)

    gb = gt_ref[rows, :] + bias_ref[...]
    lfa = _log_sigmoid(gb)
    gbt = gb.T
    lft = lfa.T
    bc_cols = jnp.dot(tril, lfa, precision=lax.Precision.HIGHEST, preferred_element_type=F32)
    bc_rows = jnp.dot(lft[0:8, :], triu, precision=lax.Precision.HIGHEST, preferred_element_type=F32)

    for h in range(ML_HEADS):
        ig_row = gbt[h:h + 1, :]
        bc_row = bc_rows[ML_HEADS + h:ML_HEADS + h + 1, :]
        ig_col = gb[:, h:h + 1]
        bc_col = bc_cols[:, ML_HEADS + h:ML_HEADS + h + 1]
        m_prev = m_s[b, h:h + 1, 0:1]
        inter = bc_col + m_prev
        dmat = jnp.where(causal, bc_col - bc_row + ig_row, -jnp.inf)
        mt = jnp.maximum(inter, jnp.max(dmat, axis=1, keepdims=True))
        wts = jnp.exp(dmat - mt)

        qf = seq_rows(q_ref, h, ML_DK)
        kf = seq_rows(k_ref, h, ML_DK) * K_SCALE
        qb = qf.astype(BF16)
        vb = seq_rows(v_ref, h, ML_DV).astype(BF16)
        qk = lax.dot_general(qb, kf.astype(BF16), (((1,), (1,)), ((), ())), preferred_element_type=F32)
        sc = qk * wts
        scale = jnp.exp(inter - mt)
        cmat = c_s[b, h]
        n_row = n_s[b, h:h + 1, :]
        num = scale * _dot(qb, cmat.astype(BF16)) + _dot(sc.astype(BF16), vb)
        qn = jnp.sum(qb.astype(F32) * n_row, axis=1, keepdims=True)
        den = scale * qn + jnp.sum(sc, axis=1, keepdims=True)
        hh = num * (1.0 / jnp.maximum(jnp.abs(den), jnp.exp(-mt)))

        mu = jnp.mean(hh, axis=1, keepdims=True)
        hc = hh - mu
        var = jnp.mean(hc * hc, axis=1, keepdims=True)
        hn = hc * lax.rsqrt(var + LN_EPS) * ng_ref[:, h * ML_DV:(h + 1) * ML_DV]
        gated = jax.nn.sigmoid(seq_rows(o_ref, h, ML_DV)) * hn
        for j in range(ML_DV // LANE):
            hn_s[h * (ML_DV // LANE) + j, rows, :] = gated[:, j * LANE:(j + 1) * LANE]

        m_last = mt[L - 1:L, :]
        b_last = bc_col[L - 1:L, :]
        dec = jnp.exp(b_last - bc_col + ig_col - m_last)
        cscale = jnp.exp(b_last + m_prev - m_last)
        kd = kf * dec
        c_new = cscale * cmat + _dot(kd.T.astype(BF16), vb)
        n_new = cscale * n_row + jnp.sum(kd, axis=0, keepdims=True)
        c_s[b, h] = c_new
        n_s[b, h:h + 1, :] = n_new
        m_s[b, h:h + 1, :] = jnp.broadcast_to(m_last, (1, LANE))


def _mlstm_prompt(proj, gates, bias_row, norm_g, lj):
    nchunks = SEQ // ML_CHUNK
    rows = ML_CHUNK * BATCH
    qk_s = ML_HEADS * ML_DK // LANE
    v_s = ML_HEADS * ML_DV // LANE
    v_w = ML_HEADS * ML_DV
    last = nchunks - 1

    def state_idx(c, b):
        return jnp.where(c == last, b, 0)

    return pl.pallas_call(
        functools.partial(_mlstm_seq_kernel, nchunks=nchunks),
        grid=(nchunks, BATCH // ML_SEQS),
        in_specs=[
            pl.BlockSpec((qk_s, rows, LANE), lambda c, b: (0, c, 0)),
            pl.BlockSpec((qk_s, rows, LANE), lambda c, b: (1, c, 0)),
            pl.BlockSpec((v_s, rows, LANE), lambda c, b: (1, c, 0)),
            pl.BlockSpec((v_s, rows, LANE), lambda c, b: (2, c, 0)),
            pl.BlockSpec((rows, LANE), lambda c, b: (c, 0)),
            pl.BlockSpec((1, LANE), lambda c, b: (0, 0)),
            pl.BlockSpec((None, 1, v_w), lambda c, b: (lj, 0, 0)),
        ],
        out_specs=[
            pl.BlockSpec((rows, v_w), lambda c, b: (c, 0)),
            pl.BlockSpec((BATCH, ML_HEADS, ML_DK, ML_DV), lambda c, b: (0, 0, 0, 0)),
            pl.BlockSpec((ML_SEQS, ML_HEADS, ML_DK), lambda c, b: (state_idx(c, b), 0, 0)),
            pl.BlockSpec((ML_SEQS, 8, LANE), lambda c, b: (state_idx(c, b), 0, 0)),
        ],
        out_shape=[
            jax.ShapeDtypeStruct((ROWS_ALL, v_w), BF16),
            jax.ShapeDtypeStruct((BATCH, ML_HEADS, ML_DK, ML_DV), F32),
            jax.ShapeDtypeStruct((BATCH, ML_HEADS, ML_DK), F32),
            jax.ShapeDtypeStruct((BATCH, 8, LANE), F32),
        ],
        scratch_shapes=[pltpu.VMEM((BATCH, 8, ML_DK), F32),
                        pltpu.VMEM((BATCH, 8, LANE), F32), pltpu.VMEM((v_s, rows, LANE), F32)],
        compiler_params=pltpu.CompilerParams(
            dimension_semantics=("arbitrary", "arbitrary"), vmem_limit_bytes=VMEM_LIMIT),
        name="mlstm_prompt",
    )(proj, proj, proj, proj, gates, bias_row, norm_g)


def _mlstm_step_kernel(q_ref, k_ref, v_ref, o_ref, ka_ref, va_ref, gt_ref, bias_ref, ng_ref,
                       c_ref, n_ref, m_ref, hn_ref, co_ref, no_ref, mo_ref, kt_s, num_s):
    h = pl.program_id(0)
    i = pl.program_id(1)
    bb = ML_DEC_BB

    def cols(ref):
        return jnp.concatenate([ref[s] for s in range(ref.shape[0])], axis=1)

    @pl.when(i == 0)
    def _():
        kt_s[...] = (cols(ka_ref) * K_SCALE).T

    gb = gt_ref[...] + bias_ref[...]
    lane = lax.broadcasted_iota(jnp.int32, (bb, LANE), 1)
    ig = jnp.sum(jnp.where(lane == h, gb, 0.0), axis=1, keepdims=True)
    fp = jnp.sum(jnp.where(lane == h + ML_HEADS, gb, 0.0), axis=1, keepdims=True)
    lf = _log_sigmoid(fp)
    m4 = m_ref[...]
    hl = lax.broadcasted_iota(jnp.int32, (bb, ML_HEADS), 1)
    m_prev = jnp.sum(jnp.where(hl == h, m4, 0.0), axis=1, keepdims=True)
    inter = lf + m_prev
    mt = jnp.maximum(inter, ig)
    wts = jnp.exp(ig - mt)
    scale = jnp.exp(inter - mt)

    qb = cols(q_ref).astype(BF16)
    qf = qb.astype(F32)
    kf = cols(k_ref) * K_SCALE
    vf = cols(v_ref).astype(BF16).astype(F32)
    n_prev = n_ref[...]
    qk = jnp.sum(qf * kf.astype(BF16).astype(F32), axis=1, keepdims=True)
    sc = qk * wts
    qn = jnp.sum(qf * n_prev, axis=1, keepdims=True)
    den = scale * qn + sc
    rden = 1.0 / jnp.maximum(jnp.abs(den), jnp.exp(-mt))

    ktb = kt_s[...].astype(BF16)
    va = cols(va_ref)
    rowi = lax.broadcasted_iota(jnp.int32, (DEC_BATCH, ML_DV), 0)
    for j in range(bb):
        cj = c_ref[j]
        qc = _dot(qb, cj.astype(BF16))[j:j + 1, :]
        vsel = jnp.where(rowi == i * bb + j, va, 0.0).astype(BF16)
        outer = _dot(ktb, vsel)
        co_ref[j] = scale[j:j + 1, :] * cj + wts[j:j + 1, :] * outer
        num_s[j:j + 1, :] = scale[j:j + 1, :] * qc + sc[j:j + 1, :] * vf[j:j + 1, :]

    hh = num_s[...] * rden
    mu = jnp.mean(hh, axis=1, keepdims=True)
    hc = hh - mu
    var = jnp.mean(hc * hc, axis=1, keepdims=True)
    hn = hc * lax.rsqrt(var + LN_EPS) * ng_ref[...]
    hn_ref[...] = (jax.nn.sigmoid(cols(o_ref)) * hn).astype(BF16)
    no_ref[...] = scale * n_prev + wts * kf
    mo_ref[...] = jnp.broadcast_to(mt, (bb, LANE))


def _mlstm_sample(proj, gates, bias_row, norm_g, c_in, n_in, m_in, hn_all, c_acc, lj):
    bb = ML_DEC_BB
    rb = ROWS_PROMPT // bb
    r128 = ROWS_PROMPT // DEC_BATCH
    nh = ML_HEADS
    ks = ML_DK // LANE
    vs = ML_DV // LANE
    in_specs = [
        pl.BlockSpec((ks, bb, LANE), lambda h, i: (h, rb + i, 0)),
        pl.BlockSpec((ks, bb, LANE), lambda h, i: (nh + h, rb + i, 0)),
        pl.BlockSpec((vs, bb, LANE), lambda h, i: (nh + h, rb + i, 0)),
        pl.BlockSpec((vs, bb, LANE), lambda h, i: (2 * nh + h, rb + i, 0)),
        pl.BlockSpec((ks, DEC_BATCH, LANE), lambda h, i: (nh + h, r128, 0)),
        pl.BlockSpec((vs, DEC_BATCH, LANE), lambda h, i: (nh + h, r128, 0)),
        pl.BlockSpec((bb, LANE), lambda h, i: (rb + i, 0)),
        pl.BlockSpec((1, LANE), lambda h, i: (0, 0)),
        pl.BlockSpec((None, 1, ML_DV), lambda h, i: (lj, 0, h)),
        pl.BlockSpec((None, bb, None, ML_DK, ML_DV), lambda h, i: (lj, i, h, 0, 0)),
        pl.BlockSpec((None, bb, ML_DK), lambda h, i: (lj, i, h)),
        pl.BlockSpec((None, bb, nh), lambda h, i: (lj, i, 0)),
        pl.BlockSpec(memory_space=pl.ANY),
    ]
    args = [proj, proj, proj, proj, proj, proj, gates, bias_row, norm_g, c_in, n_in, m_in, hn_all]
    n_blocked = 12
    aliases = {n_blocked: 0}
    if c_acc is not None:
        in_specs.append(pl.BlockSpec(memory_space=pl.ANY))
        args.append(c_acc)
        aliases[n_blocked + 1] = 1

    def body(*refs):
        _mlstm_step_kernel(*refs[:n_blocked], *refs[len(args):])

    return pl.pallas_call(
        body,
        grid=(nh, DEC_BATCH // bb),
        in_specs=in_specs,
        out_specs=[
            pl.BlockSpec((bb, ML_DV), lambda h, i: (rb + i, h)),
            pl.BlockSpec((None, bb, None, ML_DK, ML_DV), lambda h, i: (lj, i, h, 0, 0)),
            pl.BlockSpec((bb, ML_DK), lambda h, i: (i, h)),
            pl.BlockSpec((None, bb, LANE), lambda h, i: (h, i, 0)),
        ],
        out_shape=[
            jax.ShapeDtypeStruct((ROWS_ALL, nh * ML_DV), BF16),
            jax.ShapeDtypeStruct(c_in.shape, F32),
            jax.ShapeDtypeStruct((DEC_BATCH, nh * ML_DK), F32),
            jax.ShapeDtypeStruct((nh, DEC_BATCH, LANE), F32),
        ],
        scratch_shapes=[pltpu.VMEM((ML_DK, DEC_BATCH), F32), pltpu.VMEM((bb, ML_DV), F32)],
        input_output_aliases=aliases,
        compiler_params=pltpu.CompilerParams(
            dimension_semantics=("arbitrary", "arbitrary"), vmem_limit_bytes=VMEM_LIMIT),
        name="mlstm_step_sample",
    )(*args)


def kernel(x_prompt, x_sample, state_s5_re, state_s5_im, state_mlstm_C, state_mlstm_n, state_mlstm_m,
           ln_g, ln_b, ffn_w_gate, ffn_w_up, ffn_w_down,
           s5_a_re, s5_a_im, s5_log_dt, s5_b_re, s5_b_im, s5_c_re, s5_c_im, s5_d, s5_w_a, s5_w_b,
           ml_w_in, ml_b_i, ml_b_f, ml_norm_g, ml_w_out):
    n_s5 = s5_a_re.shape[0]
    n_ml = ml_w_in.shape[0]

    ffn_f32 = (ffn_w_gate, ffn_w_up, ffn_w_down)
    ffn_w = tuple(w[0, 0].astype(BF16) for w in ffn_f32)
    glu_a = s5_w_a.astype(BF16)
    glu_b = s5_w_b.astype(BF16)
    w_in = ml_w_in.astype(BF16)
    w_in_gate = jnp.pad(ml_w_in[:, :, ML_QKVO:].astype(BF16), ((0, 0), (0, 0), (0, LANE - 2 * ML_HEADS)))
    w_out = ml_w_out.astype(BF16)
    ln_g4 = ln_g.reshape(DEPTH, 3, 1, D_MODEL)
    ln_b4 = ln_b.reshape(DEPTH, 3, 1, D_MODEL)
    ml_bias = jnp.pad(jnp.concatenate([ml_b_i, ml_b_f], axis=1), ((0, 0), (0, LANE - 2 * ML_HEADS)))
    ml_ng = ml_norm_g.reshape(n_ml, 1, ML_HEADS * ML_DV)
    st_r = state_s5_re.reshape(n_s5, DEC_BATCH, S5_LANES)
    st_i = state_s5_im.reshape(n_s5, DEC_BATCH, S5_LANES)
    st_n = state_mlstm_n.reshape(n_ml, DEC_BATCH, ML_HEADS * ML_DK)

    s5_a1, s5_a2, s5_wb, s5_wc = _s5_prepare(s5_a_re, s5_a_im, s5_log_dt, s5_b_re, s5_b_im, s5_c_re, s5_c_im)

    x = _to_rows(x_prompt, x_sample.reshape(DEC_BATCH, D_MODEL))

    p_s5r, p_s5i, s_s5r, s_s5i = [], [], [], []
    p_c, p_n, p_m, s_n, s_m = [], [], [], [], []
    s_c = None
    zero_state = jnp.zeros((8, S5_LANES), F32)
    for li in range(DEPTH):
        x, ffn_w = _ffn_ln(x, ffn_w, ln_g4, ln_b4, li, 0, ffn_f32, (li, 1))
        j = li // 2
        if li % 2 == 0:
            d_row = s5_d[j].reshape(1, D_MODEL)
            z, hp = _s5_scan_prompt(x, zero_state, s5_a1, s5_a2, s5_wb, s5_wc, d_row, j)
            z, hsr, hsi = _s5_step_sample(x, st_r, st_i, j, s5_a1, s5_a2, s5_wb, s5_wc, d_row, z)
            p_s5r.append(hp[:BATCH].reshape(BATCH, S5_GROUPS, S5_STATE))
            p_s5i.append(hp[BATCH:].reshape(BATCH, S5_GROUPS, S5_STATE))
            s_s5r.append(hsr.reshape(DEC_BATCH, S5_GROUPS, S5_STATE))
            s_s5i.append(hsi.reshape(DEC_BATCH, S5_GROUPS, S5_STATE))
            x = _proj_ln(x, z, (glu_a, glu_b), ln_g4, ln_b4, j, li)
        else:
            proj, gates = _in_proj(x, w_in, w_in_gate, j)
            bias_row = ml_bias[j].reshape(1, LANE)
            z, cp, npr, mp = _mlstm_prompt(proj, gates, bias_row, ml_ng, j)
            z, s_c, ns, ms = _mlstm_sample(proj, gates, bias_row, ml_ng, state_mlstm_C, st_n,
                                           state_mlstm_m, z, s_c, j)
            p_c.append(cp)
            p_n.append(npr)
            p_m.append(mp[:, :ML_HEADS, 0])
            s_n.append(ns.reshape(DEC_BATCH, ML_HEADS, ML_DK))
            s_m.append(jnp.swapaxes(ms[:, :, 0], 0, 1))
            x = _proj_ln(x, z, (w_out,), ln_g4, ln_b4, j, li)
        x, ffn_w = _ffn_ln(x, ffn_w, ln_g4, ln_b4, li, 2, ffn_f32, (li + 1, 0) if li + 1 < DEPTH else None)

    y_prompt, y_sample = _from_rows(x)
    y_sample = y_sample.reshape(DEC_BATCH, 1, D_MODEL)
    return (y_prompt, y_sample,
            jnp.stack(p_s5r), jnp.stack(p_s5i), jnp.stack(p_c), jnp.stack(p_n), jnp.stack(p_m),
            jnp.stack(s_s5r), jnp.stack(s_s5i), s_c, jnp.stack(s_n), jnp.stack(s_m))
```
